```python
import math
import jax
import jax.numpy as jnp
from jax import lax
import numpy as np

D_MODEL = 1024
BATCH = 16
SEQ = 256
DEPTH = 1
DEC_BATCH = 2
DEC_SEQ = 1024
PAST_LEN = 512

GRID_W = 64
EPS = 1e-6
N_MOD = 6
A_HEADS = 8
A_HEAD_DIM = 64
A_V_DIM = 2 * A_HEAD_DIM
A_QK_W = A_HEADS * 2 * A_HEAD_DIM
A_V_W = A_HEADS * A_V_DIM
Q_BLOCK = 128
ROPE_BASE = 10000.0
S_INNER = 2 * D_MODEL
S_HEADDIM = 64
S_HEADS = S_INNER // S_HEADDIM
S_GROUPS = 4
S_HPG = S_HEADS // S_GROUPS
S_STATE = 128
S_CONV = 3
S_CHUNK = 128
S_CONV_CH = S_INNER + 2 * S_GROUPS * S_STATE
N_EXPERTS = 16
EC_FACTOR = 2
D_FF = 2 * D_MODEL
IN_SIZES = (A_QK_W, A_QK_W, A_V_W, S_INNER, S_CONV_CH, 2 * S_HEADS, 2 * D_MODEL)
IN_SPLIT = tuple(sum(IN_SIZES[:i + 1]) for i in range(len(IN_SIZES) - 1))
IN_WIDTH = sum(IN_SIZES)

kernel_name = 'hybrid_diffattn_ssd_ec_diffusion_step'


def rmsnorm(x, w):
    xf = x.astype(jnp.float32)
    xf = xf * lax.rsqrt(jnp.mean(xf * xf, axis=-1, keepdims=True) + EPS)
    return (xf * w.astype(jnp.float32)).astype(x.dtype)


def axial_rope(x):
    n = x.shape[1]
    rows = n // GRID_W
    row = jnp.repeat(jnp.arange(rows), GRID_W)
    col = jnp.tile(jnp.arange(GRID_W), rows)
    pos = jnp.stack([row, col], axis=-1).astype(jnp.float32)
    half = A_HEAD_DIM // 2
    nf = half // 2
    inv_freq = jnp.power(ROPE_BASE, -jnp.arange(nf, dtype=jnp.float32) / nf)
    ang = pos[:, :, None] * inv_freq
    cos = jnp.cos(ang)[None, :, None, None].astype(x.dtype)
    sin = jnp.sin(ang)[None, :, None, None].astype(x.dtype)
    xr = x.reshape(x.shape[:-1] + (2, half))
    x1, x2 = xr[..., :nf], xr[..., nf:]
    out = jnp.concatenate([x1 * cos - x2 * sin, x2 * cos + x1 * sin], axis=-1)
    return out.reshape(x.shape)


def diff_attention(q, k, v, lam, lam_init, w_subln):
    b, nq = q.shape[:2]
    nblk = nq // Q_BLOCK
    scale = A_HEAD_DIM ** -0.5
    qb = q.reshape(b, nblk, Q_BLOCK, A_HEADS, 2, A_HEAD_DIM).transpose(1, 0, 2, 3, 4, 5)

    def block(qi):
        s = jnp.einsum('bqhmd,bkhmd->bhmqk', qi, k).astype(jnp.float32) * scale
        pr = jax.nn.softmax(s, axis=-1)
        a = pr[:, :, 0] - lam * pr[:, :, 1]
        return jnp.einsum('bhqk,bkhe->bqhe', a.astype(v.dtype), v)

    o = lax.map(block, qb)
    o = o.transpose(1, 0, 2, 3, 4).reshape(b, nq, A_HEADS, A_V_DIM)
    o = rmsnorm(o, w_subln) * (1.0 - lam_init)
    return o.reshape(b, nq, A_V_W)


def dwconv_centred(x, w, bias):
    y = lax.conv_general_dilated(
        x, w[:, None, :].astype(x.dtype), window_strides=(1,),
        padding=[((S_CONV - 1) // 2, S_CONV // 2)],
        dimension_numbers=('NWC', 'WIO', 'NWC'), feature_group_count=x.shape[-1])
    return y + bias.astype(x.dtype)


def segsum_exp(a):
    l = a.shape[-1]
    cs = jnp.cumsum(a, axis=-1)
    diff = cs[..., :, None] - cs[..., None, :]
    mask = jnp.tril(jnp.ones((l, l), dtype=bool))
    return jnp.where(mask, jnp.exp(jnp.where(mask, diff, 0.0)), 0.0)


def ssd_scan(x, dt, a, bm, cm, h0):
    b, n = x.shape[:2]
    nc = n // S_CHUNK
    f32 = jnp.float32
    xdt = (x.astype(f32) * dt[..., None]).reshape(b, nc, S_CHUNK, S_GROUPS, S_HPG, S_HEADDIM)
    bc = bm.astype(f32).reshape(b, nc, S_CHUNK, S_GROUPS, S_STATE)
    cc = cm.astype(f32).reshape(b, nc, S_CHUNK, S_GROUPS, S_STATE)
    da = (dt * a).reshape(b, nc, S_CHUNK, S_GROUPS, S_HPG).transpose(0, 3, 4, 1, 2)
    a_cs = jnp.cumsum(da, axis=-1)
    lmat = segsum_exp(da)
    cb = jnp.einsum('bclgn,bcsgn->bcgls', cc, bc)
    y_diag = jnp.einsum('bcgls,bghcls,bcsghp->bclghp', cb, lmat, xdt)
    decay_states = jnp.exp(a_cs[..., -1:] - a_cs)
    states = jnp.einsum('bclgn,bghcl,bclghp->bcghpn', bc, decay_states, xdt)
    states = jnp.concatenate([h0.astype(f32)[:, None], states], axis=1)
    chunk_tot = jnp.pad(a_cs[..., -1], ((0, 0), (0, 0), (0, 0), (1, 0)))
    decay_chunk = segsum_exp(chunk_tot)
    new_states = jnp.einsum('bghzc,bcghpn->bzghpn', decay_chunk, states)
    states_in, final = new_states[:, :-1], new_states[:, -1]
    y_off = jnp.einsum('bclgn,bcghpn,bghcl->bclghp', cc, states_in, jnp.exp(a_cs))
    y = (y_diag + y_off).reshape(b, n, S_GROUPS, S_HPG, S_HEADDIM)
    return y, final


def ssd_branch(z, xbc, dt_raw, p, h0_f, h0_b):
    b, n, _ = z.shape
    f32 = jnp.float32
    xbc = jax.nn.silu(dwconv_centred(xbc, p['conv_w'], p['conv_b']))
    xs, bm, cm = jnp.split(xbc, (S_INNER, S_INNER + S_GROUPS * S_STATE), axis=-1)
    xs = xs.reshape(b, n, S_GROUPS, S_HPG, S_HEADDIM)
    bm = bm.reshape(b, n, S_GROUPS, S_STATE)
    cm = cm.reshape(b, n, S_GROUPS, S_STATE)
    dt = jax.nn.softplus(dt_raw.astype(f32).reshape(b, n, 2, S_GROUPS, S_HPG)
                         + p['dt_bias'].astype(f32).reshape(2, S_GROUPS, S_HPG))
    a = -jnp.exp(p['a_log'].astype(f32)).reshape(2, S_GROUPS, S_HPG)
    y_f, s_f = ssd_scan(xs, dt[:, :, 0], a[0], bm, cm, h0_f)
    y_b, s_b = ssd_scan(xs[:, ::-1], dt[:, ::-1, 1], a[1], bm[:, ::-1], cm[:, ::-1], h0_b)
    y = y_f + y_b[:, ::-1] + xs.astype(f32) * p['d_skip'].astype(f32).reshape(S_GROUPS, S_HPG, 1)
    y = y.reshape(b, n, S_GROUPS, S_INNER // S_GROUPS) * jax.nn.silu(z.astype(f32)).reshape(b, n, S_GROUPS, S_INNER // S_GROUPS)
    y = rmsnorm(y, p['w_ssd_norm'].reshape(S_GROUPS, S_INNER // S_GROUPS))
    return y.reshape(b, n, S_INNER).astype(z.dtype), s_f, s_b


def token_mixer(h, p, layer_idx, ctx):
    b, n, _ = h.shape
    q, k, v, z, xbc, dt_raw, g = jnp.split(h @ p['w_in'], IN_SPLIT, axis=-1)
    q = q.reshape(b, n, A_HEADS, 2, A_HEAD_DIM)
    k = k.reshape(b, n, A_HEADS, 2, A_HEAD_DIM)
    v = v.reshape(b, n, A_HEADS, A_V_DIM)
    lam_init = 0.8 - 0.6 * math.exp(-0.3 * layer_idx)
    lam = (jnp.exp(jnp.sum(p['lam_q1'] * p['lam_k1']).astype(jnp.float32))
           - jnp.exp(jnp.sum(p['lam_q2'] * p['lam_k2']).astype(jnp.float32)) + lam_init)
    if ctx is None:
        k_all, v_all = k, v
        h0_f = jnp.zeros((b, S_GROUPS, S_HPG, S_HEADDIM, S_STATE), jnp.float32)
        h0_b = h0_f
    else:
        k_ctx, v_ctx, st_ctx = ctx
        q, k = axial_rope(q), axial_rope(k)
        k_all = jnp.concatenate([k_ctx.reshape(b, -1, A_HEADS, 2, A_HEAD_DIM).astype(k.dtype), k], axis=1)
        v_all = jnp.concatenate([v_ctx.astype(v.dtype), v], axis=1)
        st = st_ctx.astype(jnp.float32).reshape(b, 2, S_GROUPS, S_HPG, S_HEADDIM, S_STATE)
        h0_f, h0_b = st[:, 0], st[:, 1]
    o_attn = diff_attention(q, k_all, v_all, lam, lam_init, p['w_subln'])
    o_ssd, s_f, s_b = ssd_branch(z, xbc, dt_raw, p, h0_f, h0_b)
    g_attn, g_ssd = jnp.split(jax.nn.sigmoid(g), 2, axis=-1)
    merged = g_attn * (o_attn @ p['w_o_attn']) + g_ssd * (o_ssd @ p['w_o_ssd'])
    y = merged @ p['w_out']
    if ctx is None:
        new_ctx = (k.reshape(b, n, A_HEADS, 2 * A_HEAD_DIM), v,
                   jnp.stack([s_f, s_b], axis=1).reshape(b, 2, S_HEADS, S_HEADDIM, S_STATE))
    else:
        new_ctx = None
    return y, new_ctx


def expert_choice_ffn(h, w_router, w_gate, w_up, w_down):
    b, n, d = h.shape
    ntok = b * n
    cap = EC_FACTOR * ntok // N_EXPERTS
    xf = h.reshape(ntok, d)
    aff = jax.nn.softmax((xf @ w_router).astype(jnp.float32), axis=-1)
    gates, idx = lax.top_k(aff.T, cap)
    xe = xf[idx]
    he = jax.nn.silu(jnp.einsum('ecd,edf->ecf', xe, w_gate)) * jnp.einsum('ecd,edf->ecf', xe, w_up)
    ye = jnp.einsum('ecf,efd->ecd', he, w_down) * gates[..., None].astype(xe.dtype)
    out = jnp.zeros_like(xf).at[idx.reshape(-1)].add(ye.reshape(-1, d).astype(xf.dtype))
    return out.reshape(b, n, d)


def trunk_layer(x, cond, p, layer_idx, ctx):
    mod = jax.nn.silu(cond) @ p['w_ada'] + p['b_ada']
    sh1, sc1, g1, sh2, sc2, g2 = [m[:, None, :] for m in jnp.split(mod, N_MOD, axis=-1)]
    h = rmsnorm(x, p['g_pre_mix']) * (1 + sc1) + sh1
    y, new_ctx = token_mixer(h, p, layer_idx, ctx)
    x = x + g1 * rmsnorm(y, p['g_post_mix'])
    h = rmsnorm(x, p['g_pre_ffn']) * (1 + sc2) + sh2
    y = expert_choice_ffn(h, p['w_router'], p['w_gate'], p['w_up'], p['w_down'])
    x = x + g2 * rmsnorm(y, p['g_post_ffn'])
    return x, new_ctx


def setup_inputs(seed: int = 0) -> dict:
    key = jax.random.key(seed)
    ks = jax.random.split(key, 40)
    f32 = jnp.float32
    L = DEPTH

    def nrm(k, shape, s):
        return jax.random.normal(k, shape, f32) * s

    def gain(k, shape):
        return 1.0 + 0.02 * jax.random.normal(k, shape, f32)

    dt0 = jnp.exp(jax.random.uniform(ks[20], (L, 2, S_HEADS), f32, math.log(1e-3), math.log(1e-1)))
    return {
        'x_prompt': nrm(ks[0], (BATCH, SEQ, D_MODEL), 1.0),
        'x_sample': nrm(ks[1], (DEC_BATCH, DEC_SEQ, D_MODEL), 1.0),
        'cache_k': nrm(ks[2], (DEC_BATCH, L, PAST_LEN, A_HEADS, 2 * A_HEAD_DIM), 1.0),
        'cache_v': nrm(ks[3], (DEC_BATCH, L, PAST_LEN, A_HEADS, A_V_DIM), 1.0),
        'state_ssd': nrm(ks[4], (DEC_BATCH, L, 2, S_HEADS, S_HEADDIM, S_STATE), 0.5),
        'c': nrm(ks[5], (DEC_BATCH, D_MODEL), 1.0),
        'c_ctx': nrm(ks[6], (D_MODEL,), 1.0),
        'w_ada': nrm(ks[7], (L, D_MODEL, N_MOD * D_MODEL), 0.5 * D_MODEL ** -0.5),
        'b_ada': nrm(ks[8], (L, N_MOD * D_MODEL), 0.02),
        'g_pre_mix': gain(ks[9], (L, D_MODEL)),
        'g_post_mix': gain(ks[10], (L, D_MODEL)),
        'g_pre_ffn': gain(ks[11], (L, D_MODEL)),
        'g_post_ffn': gain(ks[12], (L, D_MODEL)),
        'w_in': nrm(ks[13], (L, D_MODEL, IN_WIDTH), D_MODEL ** -0.5),
        'lam_q1': nrm(ks[14], (L, A_HEAD_DIM), 0.1),
        'lam_k1': nrm(ks[15], (L, A_HEAD_DIM), 0.1),
        'lam_q2': nrm(ks[16], (L, A_HEAD_DIM), 0.1),
        'lam_k2': nrm(ks[17], (L, A_HEAD_DIM), 0.1),
        'w_subln': gain(ks[18], (L, A_V_DIM)),
        'conv_w': nrm(ks[19], (L, S_CONV, S_CONV_CH), S_CONV ** -0.5),
        'conv_b': nrm(ks[21], (L, S_CONV_CH), 0.02),
        'dt_bias': dt0 + jnp.log(-jnp.expm1(-dt0)),
        'a_log': jnp.log(jax.random.uniform(ks[22], (L, 2, S_HEADS), f32, 1.0, 16.0)),
        'd_skip': gain(ks[23], (L, S_HEADS)),
        'w_ssd_norm': gain(ks[24], (L, S_INNER)),
        'w_o_attn': nrm(ks[25], (L, A_V_W, D_MODEL), A_V_W ** -0.5),
        'w_o_ssd': nrm(ks[26], (L, S_INNER, D_MODEL), S_INNER ** -0.5),
        'w_out': nrm(ks[27], (L, D_MODEL, D_MODEL), D_MODEL ** -0.5),
        'w_router': nrm(ks[28], (L, D_MODEL, N_EXPERTS), D_MODEL ** -0.5),
        'w_gate': nrm(ks[29], (L, N_EXPERTS, D_MODEL, D_FF), D_MODEL ** -0.5),
        'w_up': nrm(ks[30], (L, N_EXPERTS, D_MODEL, D_FF), D_MODEL ** -0.5),
        'w_down': nrm(ks[31], (L, N_EXPERTS, D_FF, D_MODEL), D_FF ** -0.5),
    }


def reference(x_prompt, x_sample, cache_k, cache_v, state_ssd, c, c_ctx, w_ada, b_ada,
              g_pre_mix, g_post_mix, g_pre_ffn, g_post_ffn, w_in, lam_q1, lam_k1, lam_q2, lam_k2,
              w_subln, conv_w, conv_b, dt_bias, a_log, d_skip, w_ssd_norm, w_o_attn, w_o_ssd,
              w_out, w_router, w_gate, w_up, w_down):
    cond_ctx = jnp.broadcast_to(c_ctx, (x_prompt.shape[0], D_MODEL))
    xp, xs = x_prompt, x_sample
    ks_new, vs_new, sts_new = [], [], []
    for l in range(DEPTH):
        p = dict(w_ada=w_ada[l], b_ada=b_ada[l], g_pre_mix=g_pre_mix[l], g_post_mix=g_post_mix[l],
                 g_pre_ffn=g_pre_ffn[l], g_post_ffn=g_post_ffn[l], w_in=w_in[l],
                 lam_q1=lam_q1[l], lam_k1=lam_k1[l], lam_q2=lam_q2[l], lam_k2=lam_k2[l],
                 w_subln=w_subln[l], conv_w=conv_w[l], conv_b=conv_b[l], dt_bias=dt_bias[l],
                 a_log=a_log[l], d_skip=d_skip[l], w_ssd_norm=w_ssd_norm[l], w_o_attn=w_o_attn[l],
                 w_o_ssd=w_o_ssd[l], w_out=w_out[l], w_router=w_router[l], w_gate=w_gate[l],
                 w_up=w_up[l], w_down=w_down[l])
        xp, (k_new, v_new, st_new) = trunk_layer(xp, cond_ctx, p, l, None)
        ks_new.append(k_new)
        vs_new.append(v_new)
        sts_new.append(st_new)
        xs, _ = trunk_layer(xs, c, p, l, (cache_k[:, l], cache_v[:, l], state_ssd[:, l]))
    new_cache_k = jnp.stack(ks_new, axis=1)
    new_cache_v = jnp.stack(vs_new, axis=1)
    new_state_ssd = jnp.stack(sts_new, axis=1)
    return (xp, xs, new_cache_k, new_cache_v, new_state_ssd)
```

```python
import functools
import math

import jax
import jax.numpy as jnp
from jax import lax
from jax.experimental import pallas as pl
from jax.experimental.pallas import tpu as pltpu

F32 = jnp.float32
BF16 = jnp.bfloat16
I32 = jnp.int32

D_MODEL = 1024
GRID_W = 64
EPS = 1e-6
N_MOD = 6
A_HEADS = 8
A_HEAD_DIM = 64
A_V_DIM = 128
ROPE_BASE = 10000.0
S_INNER = 2048
S_HEADDIM = 64
S_HEADS = 32
S_GROUPS = 4
S_GROUP_W = S_INNER // S_GROUPS
S_STATE = 128
S_CHUNK = 128
S_CONV_CH = S_INNER + 2 * S_GROUPS * S_STATE
N_EXPERTS = 16
EC_FACTOR = 2
D_FF = 2048
LANES = 128
TAIL_W = 2560
DT_COL = 2048
VMEM_LIMIT = 56 * 1024 * 1024


def _cparams(sem, vmem=None):
    return pltpu.CompilerParams(dimension_semantics=sem, vmem_limit_bytes=vmem)


def _dot(a, b):
    return jnp.dot(a, b, preferred_element_type=F32)


def _dot_nt(a, b):
    return lax.dot_general(a, b, (((1,), (1,)), ((), ())), preferred_element_type=F32)


def _split3(a):
    a1 = a.astype(BF16)
    r1 = a - a1.astype(F32)
    a2 = r1.astype(BF16)
    a3 = (r1 - a2.astype(F32)).astype(BF16)
    return a1, a2, a3


def _dot_f32_lhs(a, m):
    a1, a2, a3 = _split3(a)
    return _dot(a1, m) + (_dot(a2, m) + _dot(a3, m))


def _dot_f32_rhs(m, b):
    b1, b2, b3 = _split3(b)
    return _dot(m, b1) + (_dot(m, b2) + _dot(m, b3))


def _dot_hi(a, b):
    a1, a2, _ = _split3(a)
    b1, b2, _ = _split3(b)
    return _dot(a1, b1) + (_dot(a1, b2) + _dot(a2, b1))


def _silu(x):
    return x * jax.nn.sigmoid(x)


def _rms(x, w):
    return x * lax.rsqrt(jnp.mean(x * x, axis=-1, keepdims=True) + EPS) * w


def _ada_kernel(c_ref, w_ref, b_ref, o_ref):
    o_ref[...] = _dot_hi(_silu(c_ref[...]), w_ref[...]) + b_ref[...]


def _ada(cond8, w_ada, b_ada):
    n = w_ada.shape[1]
    tn = 1536
    return pl.pallas_call(
        _ada_kernel,
        grid=(n // tn,),
        in_specs=[pl.BlockSpec((8, D_MODEL), lambda j: (0, 0)),
                  pl.BlockSpec((D_MODEL, tn), lambda j: (0, j)),
                  pl.BlockSpec((1, tn), lambda j: (0, j))],
        out_specs=pl.BlockSpec((8, tn), lambda j: (0, j)),
        out_shape=jax.ShapeDtypeStruct((8, n), F32),
        compiler_params=_cparams(("arbitrary",), VMEM_LIMIT),
        name="ada",
    )(cond8, w_ada, b_ada.reshape(1, n))


def _mod_spec(col, tm, row0, rows_per_cond):
    return pl.BlockSpec((1, 1, D_MODEL), lambda i: (row0 + (i * tm) // rows_per_cond, 0, col))


def _prenorm_kernel(x_ref, sh_ref, sc_ref, g_ref, o_ref):
    xn = _rms(x_ref[...], g_ref[...])
    o_ref[...] = (xn * (1.0 + sc_ref[0]) + sh_ref[0]).astype(o_ref.dtype)


def _prenorm(x, mod3, gain, row0, rows_per_cond):
    t = x.shape[0]
    tm = 512
    return pl.pallas_call(
        _prenorm_kernel,
        grid=(t // tm,),
        in_specs=[pl.BlockSpec((tm, D_MODEL), lambda i: (i, 0)),
                  _mod_spec(0, tm, row0, rows_per_cond),
                  _mod_spec(1, tm, row0, rows_per_cond),
                  pl.BlockSpec((1, D_MODEL), lambda i: (0, 0))],
        out_specs=pl.BlockSpec((tm, D_MODEL), lambda i: (i, 0)),
        out_shape=jax.ShapeDtypeStruct((t, D_MODEL), BF16),
        compiler_params=_cparams(("parallel",)),
        name="prenorm",
    )(x, mod3, mod3, gain.reshape(1, D_MODEL))


def _mm_kernel(a_ref, w_ref, o_ref):
    o_ref[...] = _dot(a_ref[...], w_ref[...].astype(BF16)).astype(o_ref.dtype)


def _matmul(a, w, col0, ncols, out_dtype=F32):
    t, k = a.shape
    tm = min(t, 1024)
    tn = 512
    cb = col0 // tn
    return pl.pallas_call(
        _mm_kernel,
        grid=(t // tm, ncols // tn),
        in_specs=[pl.BlockSpec((tm, k), lambda i, j: (i, 0)),
                  pl.BlockSpec((k, tn), lambda i, j: (0, j + cb))],
        out_specs=pl.BlockSpec((tm, tn), lambda i, j: (i, j)),
        out_shape=jax.ShapeDtypeStruct((t, ncols), out_dtype),
        compiler_params=_cparams(("parallel", "arbitrary"), VMEM_LIMIT),
        name="inproj",
    )(a, w)


def _lam(lp_ref, lam_init):
    lp = lp_ref[...]
    s1 = jnp.sum(lp[0:1] * lp[1:2], axis=-1, keepdims=True)
    s2 = jnp.sum(lp[2:3] * lp[3:4], axis=-1, keepdims=True)
    return jnp.exp(s1) - jnp.exp(s2) + lam_init


def _diff_attn(q, k, v, lam, wsub, lam_init):
    nq = q.shape[0]
    lane = lax.broadcasted_iota(I32, q.shape, 1)
    q = q * (A_HEAD_DIM ** -0.5)
    q1 = jnp.where(lane < A_HEAD_DIM, q, 0.0).astype(BF16)
    q2 = jnp.where(lane >= A_HEAD_DIM, q, 0.0).astype(BF16)
    s = _dot_nt(jnp.concatenate([q1, q2], axis=0), k.astype(BF16))
    p = jnp.exp(s - jnp.max(s, axis=-1, keepdims=True))
    p = p / jnp.sum(p, axis=-1, keepdims=True)
    a = p[:nq] - lam * p[nq:]
    o = _dot(a.astype(BF16), v.astype(BF16))
    return _rms(o, wsub) * (1.0 - lam_init)


def _attn_ctx_kernel(lp_ref, q_ref, k_ref, v_ref, ws_ref, o_ref, *, lam_init):
    lam = _lam(lp_ref, lam_init)
    o_ref[...] = _diff_attn(q_ref[...], k_ref[...], v_ref[...], lam, ws_ref[...], lam_init)


def _attn_ctx(lam_p, q, k, v, wsub, nbatch, n, lam_init):
    blk = pl.BlockSpec((n, LANES), lambda b, h: (b, h))
    return pl.pallas_call(
        functools.partial(_attn_ctx_kernel, lam_init=lam_init),
        grid=(nbatch, A_HEADS),
        in_specs=[pl.BlockSpec((4, A_HEAD_DIM), lambda b, h: (0, 0)), blk, blk, blk,
                  pl.BlockSpec((1, A_V_DIM), lambda b, h: (0, 0))],
        out_specs=blk,
        out_shape=jax.ShapeDtypeStruct(q.shape, F32),
        compiler_params=_cparams(("parallel", "parallel")),
        name="attn_ctx",
    )(lam_p, q, k, v, wsub)


def _rope(x, cos, sin_signed):
    lane = lax.broadcasted_iota(I32, x.shape, 1)
    quarter = A_HEAD_DIM // 4
    partner = jnp.where((lane % (2 * quarter)) < quarter,
                        pltpu.roll(x, LANES - quarter, axis=1),
                        pltpu.roll(x, quarter, axis=1))
    return x * cos + partner * sin_signed


def _attn_lat_kernel(lp_ref, q_ref, k_ref, v_ref, kc_ref, vc_ref, cq_ref, sq_ref, ck_ref, sk_ref, ws_ref, o_ref,
                     *, lam_init):
    lam = _lam(lp_ref, lam_init)
    q = _rope(q_ref[...], cq_ref[...], sq_ref[...])
    k = _rope(k_ref[...], ck_ref[...], sk_ref[...])
    k_all = jnp.concatenate([kc_ref[...], k], axis=0)
    v_all = jnp.concatenate([vc_ref[...], v_ref[...]], axis=0)
    o_ref[...] = _diff_attn(q, k_all, v_all, lam, ws_ref[...], lam_init)


def _attn_lat(lam_p, q, k, v, kc, vc, cos, sin, wsub, nbatch, n, npast, lam_init):
    tq = 256
    nqb = n // tq
    qblk = pl.BlockSpec((tq, LANES), lambda b, h, i: (b * nqb + i, h))
    kblk = pl.BlockSpec((n, LANES), lambda b, h, i: (b, h))
    cblk = pl.BlockSpec((npast, LANES), lambda b, h, i: (b, h))
    return pl.pallas_call(
        functools.partial(_attn_lat_kernel, lam_init=lam_init),
        grid=(nbatch, A_HEADS, nqb),
        in_specs=[pl.BlockSpec((4, A_HEAD_DIM), lambda b, h, i: (0, 0)), qblk, kblk, kblk, cblk, cblk,
                  pl.BlockSpec((tq, LANES), lambda b, h, i: (i, 0)),
                  pl.BlockSpec((tq, LANES), lambda b, h, i: (i, 0)),
                  pl.BlockSpec((n, LANES), lambda b, h, i: (0, 0)),
                  pl.BlockSpec((n, LANES), lambda b, h, i: (0, 0)),
                  pl.BlockSpec((1, A_V_DIM), lambda b, h, i: (0, 0))],
        out_specs=qblk,
        out_shape=jax.ShapeDtypeStruct(q.shape, F32),
        compiler_params=_cparams(("parallel", "parallel", "parallel")),
        name="attn_lat",
    )(lam_p, q, k, v, kc, vc, cos, sin, cos, sin, wsub)


def _rope_tables(n):
    pos = jnp.arange(n)
    rowcol = jnp.stack([pos // GRID_W, pos % GRID_W], axis=-1).astype(F32)
    nf = A_HEAD_DIM // 4
    inv_freq = jnp.power(ROPE_BASE, -jnp.arange(nf, dtype=F32) / nf)
    lane = jnp.arange(LANES)
    axis = (lane % A_HEAD_DIM) // (2 * nf)
    ang = rowcol[:, axis] * inv_freq[lane % nf][None, :]
    sign = jnp.where((lane % (2 * nf)) < nf, -1.0, 1.0).astype(F32)
    return jnp.cos(ang), jnp.sin(ang) * sign[None, :]


def _conv_kernel(x_ref, w_ref, b_ref, o_ref):
    x = x_ref[...]
    n = x.shape[0]
    row = lax.broadcasted_iota(I32, x.shape, 0)
    prev = jnp.where(row == 0, 0.0, pltpu.roll(x, 1, axis=0))
    nxt = jnp.where(row == n - 1, 0.0, pltpu.roll(x, n - 1, axis=0))
    w = w_ref[...]
    o_ref[...] = _silu(prev * w[0:1] + x * w[1:2] + nxt * w[2:3] + b_ref[...])


def _conv(xbc, conv_w, conv_b, nbatch, n):
    tc = 512
    c = xbc.shape[1]
    return pl.pallas_call(
        _conv_kernel,
        grid=(nbatch, c // tc),
        in_specs=[pl.BlockSpec((n, tc), lambda b, j: (b, j)),
                  pl.BlockSpec((3, tc), lambda b, j: (0, j)),
                  pl.BlockSpec((1, tc), lambda b, j: (0, j))],
        out_specs=pl.BlockSpec((n, tc), lambda b, j: (b, j)),
        out_shape=jax.ShapeDtypeStruct(xbc.shape, F32),
        compiler_params=_cparams(("parallel", "parallel")),
        name="conv",
    )(xbc, conv_w, conv_b.reshape(1, c))


def _pair_expand(v, c0):
    lane = lax.broadcasted_iota(I32, (v.shape[0], LANES), 1)
    return jnp.where(lane < S_HEADDIM, v[:, c0:c0 + 1], v[:, c0 + 1:c0 + 2])


def _ssd_direction(xa, dt_raw, dt_raw_t, dtb_row, alog_row, dtb_col, alog_col, ht_ref, y_ref, backward):
    col0 = S_HEADS if backward else 0
    row = lax.broadcasted_iota(I32, (S_CHUNK, S_CHUNK), 0)
    col = lax.broadcasted_iota(I32, (S_CHUNK, S_CHUNK), 1)
    upper = jnp.where(row <= col, 1.0, 0.0).astype(BF16)
    lower = jnp.where(row >= col, 1.0, 0.0).astype(BF16)
    dt = jax.nn.softplus(dt_raw + dtb_row)
    dt_t = jax.nn.softplus(dt_raw_t + dtb_col)
    da = dt * (-jnp.exp(alog_row))
    da_t = dt_t * (-jnp.exp(alog_col))
    if backward:
        cum = _dot_f32_rhs(upper, da)
        cum_t = _dot_f32_lhs(da_t, lower)
        total = cum[0:1, :]
        keep = row <= col
    else:
        cum = _dot_f32_rhs(lower, da)
        cum_t = _dot_f32_lhs(da_t, upper)
        total = cum[S_CHUNK - 1:S_CHUNK, :]
        keep = row >= col
    w_state = dt * jnp.exp(total - cum)
    e_in = jnp.exp(cum)
    e_tot = jnp.exp(total)
    lane = lax.broadcasted_iota(I32, (S_CHUNK, LANES), 1)
    for g in range(S_GROUPS):
        b_g = xa[:, S_INNER + g * S_STATE:S_INNER + (g + 1) * S_STATE]
        c_g = xa[:, S_INNER + (S_GROUPS + g) * S_STATE:S_INNER + (S_GROUPS + g + 1) * S_STATE]
        b_bf = b_g.astype(BF16)
        c_bf = c_g.astype(BF16)
        cb = _dot_nt(c_bf, b_bf)
        bt_bf = b_g.T.astype(BF16)
        ht = ht_ref[g]
        y_off = _dot(c_bf, ht.astype(BF16))
        xw_parts = []
        tot_parts = []
        for pr in range(S_GROUP_W // LANES):
            hp = g * (S_GROUP_W // LANES) + pr
            c0 = col0 + 2 * hp
            x_pair = xa[:, hp * LANES:(hp + 1) * LANES]
            xdt = x_pair * _pair_expand(dt, c0)
            ms = []
            for hh in range(2):
                diff = cum[:, c0 + hh:c0 + hh + 1] - cum_t[c0 + hh:c0 + hh + 1, :]
                ms.append((cb * jnp.exp(jnp.where(keep, diff, -1e30))).astype(BF16))
            lhs = jnp.concatenate(ms, axis=1)
            rhs = jnp.concatenate([jnp.where(lane < S_HEADDIM, xdt, 0.0),
                                   jnp.where(lane >= S_HEADDIM, xdt, 0.0)], axis=0).astype(BF16)
            y_diag = _dot(lhs, rhs)
            y_pair = y_diag + y_off[:, pr * LANES:(pr + 1) * LANES] * _pair_expand(e_in, c0)
            y_ref[:, hp * LANES:(hp + 1) * LANES] = y_pair
            xw_parts.append((x_pair * _pair_expand(w_state, c0)).astype(BF16))
            tot_parts.append(_pair_expand(e_tot, c0))
        xw = jnp.concatenate(xw_parts, axis=1)
        ht_ref[g] = ht * jnp.concatenate(tot_parts, axis=1) + _dot(bt_bf, xw)


def _ssd_kernel(*refs, has_h0, emit_final):
    xf_ref, xb_ref, dtf_ref, dtb_ref, dtft_ref, dtbt_ref, br_ref, ar_ref, bc_ref, ac_ref = refs[:10]
    refs = refs[10:]
    if has_h0:
        h0_ref, refs = refs[0], refs[1:]
    yf_ref, yb_ref = refs[:2]
    refs = refs[2:]
    if emit_final:
        fin_ref, refs = refs[0], refs[1:]
    htf_ref, htb_ref = refs
    c = pl.program_id(1)
    nc = pl.num_programs(1)

    @pl.when(c == 0)
    def _():
        for d, ht_ref in enumerate((htf_ref, htb_ref)):
            for g in range(S_GROUPS):
                if has_h0:
                    ht_ref[g] = h0_ref[0, d, g * S_GROUP_W:(g + 1) * S_GROUP_W, :].T
                else:
                    ht_ref[g] = jnp.zeros((S_STATE, S_GROUP_W), F32)

    _ssd_direction(xf_ref[...], dtf_ref[...], dtft_ref[...], br_ref[...], ar_ref[...], bc_ref[...], ac_ref[...],
                   htf_ref, yf_ref, backward=False)
    _ssd_direction(xb_ref[...], dtb_ref[...], dtbt_ref[...], br_ref[...], ar_ref[...], bc_ref[...], ac_ref[...],
                   htb_ref, yb_ref, backward=True)

    if emit_final:
        @pl.when(c == nc - 1)
        def _():
            for d, ht_ref in enumerate((htf_ref, htb_ref)):
                for g in range(S_GROUPS):
                    fin_ref[0, d, g * S_GROUP_W:(g + 1) * S_GROUP_W, :] = ht_ref[g].T


def _ssd(xa, tail, tail_t, dtb_row, alog_row, dtb_col, alog_col, h0, nbatch, n, emit_final):
    nc = n // S_CHUNK
    t = xa.shape[0]
    dtc = DT_COL // LANES
    fwd = lambda b, c: (b * nc + c, 0)
    bwd = lambda b, c: (b * nc + nc - 1 - c, 0)
    const = lambda b, c: (0, 0)
    in_specs = [pl.BlockSpec((S_CHUNK, S_CONV_CH), fwd),
                pl.BlockSpec((S_CHUNK, S_CONV_CH), bwd),
                pl.BlockSpec((S_CHUNK, LANES), lambda b, c: (b * nc + c, dtc)),
                pl.BlockSpec((S_CHUNK, LANES), lambda b, c: (b * nc + nc - 1 - c, dtc)),
                pl.BlockSpec((LANES, S_CHUNK), lambda b, c: (0, b * nc + c)),
                pl.BlockSpec((LANES, S_CHUNK), lambda b, c: (0, b * nc + nc - 1 - c)),
                pl.BlockSpec((1, LANES), const), pl.BlockSpec((1, LANES), const),
                pl.BlockSpec((LANES, 1), const), pl.BlockSpec((LANES, 1), const)]
    args = [xa, xa, tail, tail, tail_t, tail_t, dtb_row, alog_row, dtb_col, alog_col]
    if h0 is not None:
        in_specs.append(pl.BlockSpec((1, 2, S_INNER, S_STATE), lambda b, c: (b, 0, 0, 0)))
        args.append(h0)
    out_specs = [pl.BlockSpec((S_CHUNK, S_INNER), fwd), pl.BlockSpec((S_CHUNK, S_INNER), bwd)]
    out_shape = [jax.ShapeDtypeStruct((t, S_INNER), F32), jax.ShapeDtypeStruct((t, S_INNER), F32)]
    if emit_final:
        out_specs.append(pl.BlockSpec((1, 2, S_INNER, S_STATE), lambda b, c: (b, 0, 0, 0)))
        out_shape.append(jax.ShapeDtypeStruct((nbatch, 2, S_INNER, S_STATE), F32))
    return pl.pallas_call(
        functools.partial(_ssd_kernel, has_h0=h0 is not None, emit_final=emit_final),
        grid=(nbatch, nc),
        in_specs=in_specs,
        out_specs=out_specs,
        out_shape=out_shape,
        scratch_shapes=[pltpu.VMEM((S_GROUPS, S_STATE, S_GROUP_W), F32),
                        pltpu.VMEM((S_GROUPS, S_STATE, S_GROUP_W), F32)],
        compiler_params=_cparams(("parallel", "arbitrary"), VMEM_LIMIT),
        name="ssd",
    )(*args)


def _mixout_kernel(yf_ref, yb_ref, xs_ref, z_ref, oa_ref, ga_ref, gs_ref, x_ref, g1_ref, sh2_ref, sc2_ref,
                   dskip_ref, wn_ref, woa_ref, wos_ref, wout_ref, gpost_ref, gpre_ref, wr_ref,
                   x1_ref, h2_ref, aff_ref):
    y = yf_ref[...] + yb_ref[...] + xs_ref[...] * dskip_ref[...]
    y = y * _silu(z_ref[...])
    wn = wn_ref[...]
    parts = []
    for g in range(S_GROUPS):
        sl = slice(g * S_GROUP_W, (g + 1) * S_GROUP_W)
        parts.append(_rms(y[:, sl], wn[:, sl]).astype(BF16))
    o_ssd = jnp.concatenate(parts, axis=1)
    a = _dot(oa_ref[...].astype(BF16), woa_ref[...])
    s = _dot(o_ssd, wos_ref[...])
    merged = jax.nn.sigmoid(ga_ref[...]) * a + jax.nn.sigmoid(gs_ref[...]) * s
    ym = _dot(merged.astype(BF16), wout_ref[...])
    x1 = x_ref[...] + g1_ref[0] * _rms(ym, gpost_ref[...])
    x1_ref[...] = x1
    h2 = _rms(x1, gpre_ref[...]) * (1.0 + sc2_ref[0]) + sh2_ref[0]
    h2_ref[...] = h2.astype(BF16)
    logits = _dot_hi(h2, wr_ref[...])
    lane = lax.broadcasted_iota(I32, logits.shape, 1)
    logits = jnp.where(lane < N_EXPERTS, logits, -1e30)
    p = jnp.exp(logits - jnp.max(logits, axis=-1, keepdims=True))
    aff_ref[...] = p / jnp.sum(p, axis=-1, keepdims=True)


def _mixout(yf, yb, xa, z, o_attn, tail, x, mod3, dskip, wn, woa, wos, wout, gpost, gpre, wr, row0, rows_per_cond):
    t = x.shape[0]
    tm = 256
    tok = lambda w: pl.BlockSpec((tm, w), lambda i: (i, 0))
    const = lambda a: pl.BlockSpec(a.shape, lambda i: (0,) * a.ndim)
    return pl.pallas_call(
        _mixout_kernel,
        grid=(t // tm,),
        in_specs=[tok(S_INNER), tok(S_INNER), tok(S_INNER), tok(S_INNER), tok(D_MODEL),
                  pl.BlockSpec((tm, D_MODEL), lambda i: (i, 0)),
                  pl.BlockSpec((tm, D_MODEL), lambda i: (i, 1)),
                  tok(D_MODEL),
                  _mod_spec(2, tm, row0, rows_per_cond),
                  _mod_spec(3, tm, row0, rows_per_cond),
                  _mod_spec(4, tm, row0, rows_per_cond),
                  const(dskip), const(wn), const(woa), const(wos), const(wout), const(gpost), const(gpre),
                  const(wr)],
        out_specs=[tok(D_MODEL), tok(D_MODEL), tok(LANES)],
        out_shape=[jax.ShapeDtypeStruct((t, D_MODEL), F32), jax.ShapeDtypeStruct((t, D_MODEL), BF16),
                   jax.ShapeDtypeStruct((t, LANES), F32)],
        compiler_params=_cparams(("parallel",), VMEM_LIMIT),
        name="mixout",
    )(yf, yb, xa, z, o_attn, tail, tail, x, mod3, mod3, mod3, dskip, wn, woa, wos, wout, gpost, gpre, wr)


def _router_kernel(aff_ref, pos_ref, *, cap):
    aff = aff_ref[...]
    e, t = aff.shape
    idx = lax.broadcasted_iota(I32, (e, t), 1)

    def count(m):
        return jnp.sum(jnp.where(m, 1.0, 0.0), axis=-1, keepdims=True)

    def value_bit(i, thr):
        cand = thr | jnp.left_shift(jnp.int32(1), 30 - i)
        return jnp.where(count(aff >= pltpu.bitcast(cand, F32)) >= cap, cand, thr)

    thr = pltpu.bitcast(lax.fori_loop(0, 31, value_bit, jnp.zeros((e, 1), I32)), F32)
    gt = aff > thr
    eq = aff == thr
    need = cap - count(gt)
    nbits = (t - 1).bit_length()

    def index_bit(i, v):
        cand = v | jnp.left_shift(jnp.int32(1), nbits - 1 - i)
        return jnp.where(count(jnp.logical_and(eq, idx < cand)) < need, cand, v)

    last = lax.fori_loop(0, nbits, index_bit, jnp.zeros((e, 1), I32))
    sel = jnp.logical_or(gt, jnp.logical_and(eq, idx <= last))
    sel_bf = jnp.where(sel, 1.0, 0.0).astype(BF16)
    blk = 512
    r = lax.broadcasted_iota(I32, (blk, blk), 0)
    c = lax.broadcasted_iota(I32, (blk, blk), 1)
    upper = jnp.where(r <= c, 1.0, 0.0).astype(BF16)
    carry = jnp.zeros((e, 1), F32)
    for b in range(t // blk):
        cum = _dot(sel_bf[:, b * blk:(b + 1) * blk], upper) + carry
        carry = cum[:, blk - 1:blk]
        slot = jnp.where(sel[:, b * blk:(b + 1) * blk], cum - 1.0, -1.0)
        pos_ref[:, b * blk:(b + 1) * blk] = slot.astype(I32)


def _router(aff_t, cap):
    e, t = aff_t.shape
    return pl.pallas_call(
        functools.partial(_router_kernel, cap=cap),
        out_shape=jax.ShapeDtypeStruct((e, t), I32),
        name="router",
    )(aff_t)


def _dispatch_kernel(pos_ref, h_ref, o_ref, acc_ref, *, cap):
    t = h_ref.shape[0]
    tk = 512
    slot = lax.broadcasted_iota(I32, (cap, tk), 0)
    for kb in range(t // tk):
        onehot = jnp.where(pos_ref[0, :, kb * tk:(kb + 1) * tk] == slot, 1.0, 0.0).astype(BF16)
        part = _dot(onehot, h_ref[kb * tk:(kb + 1) * tk, :])
        if kb == 0:
            acc_ref[...] = part
        else:
            acc_ref[...] += part
    o_ref[0] = acc_ref[...].astype(BF16)


def _dispatch(pos, h2, cap):
    e, t = pos.shape
    return pl.pallas_call(
        functools.partial(_dispatch_kernel, cap=cap),
        grid=(e,),
        in_specs=[pl.BlockSpec((1, 1, t), lambda i: (i, 0, 0)),
                  pl.BlockSpec((t, D_MODEL), lambda i: (0, 0))],
        out_specs=pl.BlockSpec((1, cap, D_MODEL), lambda i: (i, 0, 0)),
        out_shape=jax.ShapeDtypeStruct((e, cap, D_MODEL), BF16),
        scratch_shapes=[pltpu.VMEM((cap, D_MODEL), F32)],
        compiler_params=_cparams(("parallel",), VMEM_LIMIT),
        name="dispatch",
    )(pos.reshape(e, 1, t), h2)


def _ffn_kernel(xc_ref, xl_ref, wg_ref, wu_ref, wd_ref, oc_ref, ol_ref, acc_ref):
    f = pl.program_id(1)
    cc = xc_ref.shape[1]
    xe = jnp.concatenate([xc_ref[0], xl_ref[0]], axis=0)
    hg = _dot(xe, wg_ref[0].astype(BF16))
    hu = _dot(xe, wu_ref[0].astype(BF16))
    part = _dot((_silu(hg) * hu).astype(BF16), wd_ref[0].astype(BF16))

    @pl.when(f == 0)
    def _():
        acc_ref[...] = part

    @pl.when(f > 0)
    def _():
        acc_ref[...] += part

    @pl.when(f == pl.num_programs(1) - 1)
    def _():
        oc_ref[0] = acc_ref[:cc, :].astype(BF16)
        ol_ref[0] = acc_ref[cc:, :].astype(BF16)


def _ffn(xe_c, xe_l, w_gate, w_up, w_down):
    e, cc, d = xe_c.shape
    cl = xe_l.shape[1]
    tf = 512
    return pl.pallas_call(
        _ffn_kernel,
        grid=(e, D_FF // tf),
        in_specs=[pl.BlockSpec((1, cc, d), lambda i, f: (i, 0, 0)),
                  pl.BlockSpec((1, cl, d), lambda i, f: (i, 0, 0)),
                  pl.BlockSpec((1, d, tf), lambda i, f: (i, 0, f)),
                  pl.BlockSpec((1, d, tf), lambda i, f: (i, 0, f)),
                  pl.BlockSpec((1, tf, d), lambda i, f: (i, f, 0))],
        out_specs=[pl.BlockSpec((1, cc, d), lambda i, f: (i, 0, 0)),
                   pl.BlockSpec((1, cl, d), lambda i, f: (i, 0, 0))],
        out_shape=[jax.ShapeDtypeStruct((e, cc, d), BF16), jax.ShapeDtypeStruct((e, cl, d), BF16)],
        scratch_shapes=[pltpu.VMEM((cc + cl, d), F32)],
        compiler_params=_cparams(("parallel", "arbitrary"), VMEM_LIMIT),
        name="ffn",
    )(xe_c, xe_l, w_gate, w_up, w_down)


def _combine_kernel(pos_ref, aff_ref, ye_ref, x1_ref, g2_ref, gp_ref, o_ref):
    tm = pos_ref.shape[0]
    cap = ye_ref.shape[1]
    pos = pos_ref[...]
    aff = aff_ref[...]
    slot = lax.broadcasted_iota(I32, (tm, cap), 1)
    y = jnp.zeros((tm, D_MODEL), F32)
    for e in range(N_EXPERTS):
        onehot = jnp.where(pos[:, e:e + 1] == slot, 1.0, 0.0).astype(BF16)
        y = y + aff[:, e:e + 1] * _dot(onehot, ye_ref[e])
    o_ref[...] = x1_ref[...] + g2_ref[0] * _rms(y, gp_ref[...])


def _combine(pos_t, aff, ye, x1, mod3, gpost, row0, rows_per_cond):
    t = x1.shape[0]
    e, cap, d = ye.shape
    tm = 256
    return pl.pallas_call(
        _combine_kernel,
        grid=(t // tm,),
        in_specs=[pl.BlockSpec((tm, e), lambda i: (i, 0)),
                  pl.BlockSpec((tm, LANES), lambda i: (i, 0)),
                  pl.BlockSpec((e, cap, d), lambda i: (0, 0, 0)),
                  pl.BlockSpec((tm, d), lambda i: (i, 0)),
                  _mod_spec(5, tm, row0, rows_per_cond),
                  pl.BlockSpec((1, d), lambda i: (0, 0))],
        out_specs=pl.BlockSpec((tm, d), lambda i: (i, 0)),
        out_shape=jax.ShapeDtypeStruct((t, d), F32),
        compiler_params=_cparams(("parallel",), VMEM_LIMIT),
        name="combine",
    )(pos_t, aff, ye, x1, mod3, gpost.reshape(1, d))


def _head_rows(v):
    flat = jnp.concatenate([v.reshape(2 * S_HEADS), jnp.zeros((LANES - 2 * S_HEADS,), F32)])
    return flat.reshape(1, LANES), flat.reshape(LANES, 1)


def kernel(x_prompt, x_sample, cache_k, cache_v, state_ssd, c, c_ctx, w_ada, b_ada, g_pre_mix, g_post_mix, g_pre_ffn, g_post_ffn, w_in, lam_q1, lam_k1, lam_q2, lam_k2, w_subln, conv_w, conv_b, dt_bias, a_log, d_skip, w_ssd_norm, w_o_attn, w_o_ssd, w_out, w_router, w_gate, w_up, w_down):
    nb_c, n_c, d = x_prompt.shape
    nb_l, n_l, _ = x_sample.shape
    npast = cache_k.shape[2]
    depth = w_in.shape[0]
    assert depth == 1 and d == D_MODEL
    t_c, t_l = nb_c * n_c, nb_l * n_l
    lyr = 0
    lam_init = 0.8 - 0.6 * math.exp(-0.3 * lyr)

    cond8 = jnp.concatenate([c_ctx[None, :], c, jnp.zeros((8 - 1 - nb_l, d), F32)], axis=0)
    mod3 = _ada(cond8, w_ada[lyr], b_ada[lyr]).reshape(8, 1, N_MOD * d)

    q_off, k_off, v_off, z_off, xbc_off = 0, 1024, 2048, 3072, 5120
    dt_off = xbc_off + S_CONV_CH
    g_off = dt_off + 2 * S_HEADS
    w_in_l = w_in[lyr]
    w_tail = jnp.concatenate([w_in_l[:, g_off:], w_in_l[:, dt_off:g_off],
                              jnp.zeros((d, TAIL_W - 2 * d - 2 * S_HEADS), F32)], axis=1)
    lam_p = jnp.stack([lam_q1[lyr], lam_k1[lyr], lam_q2[lyr], lam_k2[lyr]], axis=0)
    wsub = w_subln[lyr].reshape(1, A_V_DIM)
    dtb_row, dtb_col = _head_rows(dt_bias[lyr])
    alog_row, alog_col = _head_rows(a_log[lyr])
    dskip = jnp.repeat(d_skip[lyr], S_HEADDIM).reshape(1, S_INNER)
    wn = w_ssd_norm[lyr].reshape(1, S_INNER)
    woa = w_o_attn[lyr].astype(BF16)
    wos = w_o_ssd[lyr].astype(BF16)
    wout = w_out[lyr].astype(BF16)
    wr = jnp.concatenate([w_router[lyr], jnp.zeros((d, LANES - N_EXPERTS), F32)], axis=1)
    gpost = g_post_mix[lyr].reshape(1, d)
    gpre = g_pre_ffn[lyr].reshape(1, d)
    cos, sin = _rope_tables(n_l)

    def mixer(x, row0, rows_per_cond, nbatch, n, ctx):
        h = _prenorm(x, mod3, g_pre_mix[lyr], row0, rows_per_cond)
        q = _matmul(h, w_in_l, q_off, 1024)
        k = _matmul(h, w_in_l, k_off, 1024)
        v = _matmul(h, w_in_l, v_off, 1024)
        z = _matmul(h, w_in_l, z_off, S_INNER)
        xbc = _matmul(h, w_in_l, xbc_off, S_CONV_CH)
        tail = _matmul(h, w_tail, 0, TAIL_W)
        if ctx is None:
            o_attn = _attn_ctx(lam_p, q, k, v, wsub, nbatch, n, lam_init)
            h0 = None
        else:
            kc, vc, h0 = ctx
            o_attn = _attn_lat(lam_p, q, k, v, kc, vc, cos, sin, wsub, nbatch, n, npast, lam_init)
        xa = _conv(xbc, conv_w[lyr], conv_b[lyr], nbatch, n)
        tail_t = tail[:, DT_COL:DT_COL + LANES].T
        res = _ssd(xa, tail, tail_t, dtb_row, alog_row, dtb_col, alog_col, h0, nbatch, n, emit_final=ctx is None)
        yf, yb = res[0], res[1]
        x1, h2, aff = _mixout(yf, yb, xa, z, o_attn, tail, x, mod3, dskip, wn, woa, wos, wout, gpost, gpre, wr,
                              row0, rows_per_cond)
        fin = res[2] if ctx is None else None
        return x1, h2, aff, k, v, fin

    xc = x_prompt.reshape(t_c, d)
    xl = x_sample.reshape(t_l, d)
    x1_c, h2_c, aff_c, k_c, v_c, fin_c = mixer(xc, 0, t_c, nb_c, n_c, None)
    kc = cache_k[:, lyr].reshape(nb_l * npast, A_HEADS * 2 * A_HEAD_DIM)
    vc = cache_v[:, lyr].reshape(nb_l * npast, A_HEADS * A_V_DIM)
    h0 = state_ssd[:, lyr].reshape(nb_l, 2, S_INNER, S_STATE)
    x1_l, h2_l, aff_l, _, _, _ = mixer(xl, 1, n_l, nb_l, n_l, (kc, vc, h0))

    cap_c = EC_FACTOR * t_c // N_EXPERTS
    cap_l = EC_FACTOR * t_l // N_EXPERTS
    pos_c = _router(aff_c[:, :N_EXPERTS].T, cap_c)
    pos_l = _router(aff_l[:, :N_EXPERTS].T, cap_l)
    xe_c = _dispatch(pos_c, h2_c, cap_c)
    xe_l = _dispatch(pos_l, h2_l, cap_l)
    ye_c, ye_l = _ffn(xe_c, xe_l, w_gate[lyr], w_up[lyr], w_down[lyr])
    out_c = _combine(pos_c.T, aff_c, ye_c, x1_c, mod3, g_post_ffn[lyr], 0, t_c)
    out_l = _combine(pos_l.T, aff_l, ye_l, x1_l, mod3, g_post_ffn[lyr], 1, n_l)

    y_prompt = out_c.reshape(nb_c, n_c, d)
    y_sample = out_l.reshape(nb_l, n_l, d)
    new_k = k_c.reshape(nb_c, 1, n_c, A_HEADS, 2 * A_HEAD_DIM)
    new_v = v_c.reshape(nb_c, 1, n_c, A_HEADS, A_V_DIM)
    new_state = fin_c.reshape(nb_c, 1, 2, S_HEADS, S_HEADDIM, S_STATE)
    return (y_prompt, y_sample, new_k, new_v, new_state)
```

```python
import functools
import math

import jax
import jax.numpy as jnp
from jax import lax
from jax.experimental import pallas as pl
from jax.experimental.pallas import tpu as pltpu

F32 = jnp.float32
BF16 = jnp.bfloat16
I32 = jnp.int32

D_MODEL = 1024
GRID_W = 64
EPS = 1e-6
N_MOD = 6
A_HEADS = 8
A_HEAD_DIM = 64
A_V_DIM = 128
ROPE_BASE = 10000.0
S_INNER = 2048
S_HEADDIM = 64
S_HEADS = 32
S_GROUPS = 4
S_GROUP_W = S_INNER // S_GROUPS
S_STATE = 128
S_CHUNK = 128
S_CONV_CH = S_INNER + 2 * S_GROUPS * S_STATE
N_EXPERTS = 16
EC_FACTOR = 2
D_FF = 2048
LANES = 128
TAIL_W = 2560
DT_COL = 2048
VMEM_LIMIT = 56 * 1024 * 1024
LOG2E = 1.4426950408889634


def _cparams(sem, vmem=None):
    return pltpu.CompilerParams(dimension_semantics=sem, vmem_limit_bytes=vmem)


def _dot(a, b):
    return jnp.dot(a, b, preferred_element_type=F32)


def _dot_nt(a, b):
    return lax.dot_general(a, b, (((1,), (1,)), ((), ())), preferred_element_type=F32)


def _split3(a):
    a1 = a.astype(BF16)
    r1 = a - a1.astype(F32)
    a2 = r1.astype(BF16)
    a3 = (r1 - a2.astype(F32)).astype(BF16)
    return a1, a2, a3


def _dot_f32_lhs(a, m):
    a1, a2, a3 = _split3(a)
    return _dot(a1, m) + (_dot(a2, m) + _dot(a3, m))


def _dot_f32_rhs(m, b):
    b1, b2, b3 = _split3(b)
    return _dot(m, b1) + (_dot(m, b2) + _dot(m, b3))


def _dot_hi(a, b):
    a1, a2, _ = _split3(a)
    b1, b2, _ = _split3(b)
    return _dot(a1, b1) + (_dot(a1, b2) + _dot(a2, b1))


def _silu(x):
    return x * jax.nn.sigmoid(x)


def _rms(x, w):
    return x * lax.rsqrt(jnp.mean(x * x, axis=-1, keepdims=True) + EPS) * w


def _ada_kernel(c_ref, w_ref, b_ref, o_ref):
    o_ref[...] = _dot_hi(_silu(c_ref[...]), w_ref[...]) + b_ref[...]


def _ada(cond8, w_ada, b_ada):
    n = w_ada.shape[1]
    tn = 1536
    return pl.pallas_call(
        _ada_kernel,
        grid=(n // tn,),
        in_specs=[pl.BlockSpec((8, D_MODEL), lambda j: (0, 0)),
                  pl.BlockSpec((D_MODEL, tn), lambda j: (0, j)),
                  pl.BlockSpec((1, tn), lambda j: (0, j))],
        out_specs=pl.BlockSpec((8, tn), lambda j: (0, j)),
        out_shape=jax.ShapeDtypeStruct((8, n), F32),
        compiler_params=_cparams(("arbitrary",), VMEM_LIMIT),
        name="ada",
    )(cond8, w_ada, b_ada.reshape(1, n))


def _mod_spec(col, tm, row0, rows_per_cond):
    return pl.BlockSpec((1, 1, D_MODEL), lambda i: (row0 + (i * tm) // rows_per_cond, 0, col))


def _prenorm_kernel(x_ref, sh_ref, sc_ref, g_ref, o_ref):
    xn = _rms(x_ref[...], g_ref[...])
    o_ref[...] = (xn * (1.0 + sc_ref[0]) + sh_ref[0]).astype(o_ref.dtype)


def _prenorm(x, mod3, gain, row0, rows_per_cond):
    t = x.shape[0]
    tm = 512
    return pl.pallas_call(
        _prenorm_kernel,
        grid=(t // tm,),
        in_specs=[pl.BlockSpec((tm, D_MODEL), lambda i: (i, 0)),
                  _mod_spec(0, tm, row0, rows_per_cond),
                  _mod_spec(1, tm, row0, rows_per_cond),
                  pl.BlockSpec((1, D_MODEL), lambda i: (0, 0))],
        out_specs=pl.BlockSpec((tm, D_MODEL), lambda i: (i, 0)),
        out_shape=jax.ShapeDtypeStruct((t, D_MODEL), BF16),
        compiler_params=_cparams(("parallel",)),
        name="prenorm",
    )(x, mod3, mod3, gain.reshape(1, D_MODEL))


def _mm_kernel(a_ref, w_ref, o_ref):
    o_ref[...] = _dot(a_ref[...], w_ref[...].astype(BF16)).astype(o_ref.dtype)


def _matmul(a, w, col0, ncols, out_dtype, tn=1024):
    t, k = a.shape
    tm = 2048
    cb = col0 // tn
    assert col0 % tn == 0 and ncols % tn == 0 and t % tm == 0
    return pl.pallas_call(
        _mm_kernel,
        grid=(ncols // tn, t // tm),
        in_specs=[pl.BlockSpec((tm, k), lambda j, i: (i, 0)),
                  pl.BlockSpec((k, tn), lambda j, i: (0, j + cb))],
        out_specs=pl.BlockSpec((tm, tn), lambda j, i: (i, j)),
        out_shape=jax.ShapeDtypeStruct((t, ncols), out_dtype),
        compiler_params=_cparams(("parallel", "arbitrary"), VMEM_LIMIT),
        name="inproj",
    )(a, w)


def _dtproj_kernel(h_ref, w_ref, wt_ref, o_ref, ot_ref):
    h = h_ref[...]
    o_ref[...] = _dot(h, w_ref[...])
    ot_ref[...] = _dot_nt(wt_ref[...], h)


def _dtproj(h, w_dt, w_dt_t):
    t, k = h.shape
    tm = 1024
    return pl.pallas_call(
        _dtproj_kernel,
        grid=(t // tm,),
        in_specs=[pl.BlockSpec((tm, k), lambda i: (i, 0)),
                  pl.BlockSpec((k, LANES), lambda i: (0, 0)),
                  pl.BlockSpec((LANES, k), lambda i: (0, 0))],
        out_specs=[pl.BlockSpec((tm, LANES), lambda i: (i, 0)),
                   pl.BlockSpec((LANES, tm), lambda i: (0, i))],
        out_shape=[jax.ShapeDtypeStruct((t, LANES), F32), jax.ShapeDtypeStruct((LANES, t), F32)],
        compiler_params=_cparams(("parallel",)),
        name="dtproj",
    )(h, w_dt, w_dt_t)


def _lam(lp_ref, lam_init):
    lp = lp_ref[...]
    s1 = jnp.sum(lp[0:1] * lp[1:2], axis=-1, keepdims=True)
    s2 = jnp.sum(lp[2:3] * lp[3:4], axis=-1, keepdims=True)
    return jnp.exp(s1) - jnp.exp(s2) + lam_init


def _diff_attn(q, k, v, lam, wsub, lam_init):
    nq = q.shape[0]
    lane = lax.broadcasted_iota(I32, q.shape, 1)
    q = q * (A_HEAD_DIM ** -0.5 * LOG2E)
    q1 = jnp.where(lane < A_HEAD_DIM, q, 0.0).astype(BF16)
    q2 = jnp.where(lane >= A_HEAD_DIM, q, 0.0).astype(BF16)
    s = _dot_nt(jnp.concatenate([q1, q2], axis=0), k)
    p = jnp.exp2(s - jnp.max(s, axis=-1, keepdims=True))
    r = 1.0 / jnp.sum(p, axis=-1, keepdims=True)
    pv = _dot(p.astype(BF16), v) * r
    o = pv[:nq] - lam * pv[nq:]
    return _rms(o, wsub) * (1.0 - lam_init)


def _attn_ctx_kernel(lp_ref, q_ref, k_ref, v_ref, ws_ref, o_ref, *, lam_init):
    lam = _lam(lp_ref, lam_init)
    for h in range(A_HEADS):
        sl = slice(h * LANES, (h + 1) * LANES)
        o = _diff_attn(q_ref[:, sl].astype(F32), k_ref[:, sl].astype(BF16), v_ref[:, sl].astype(BF16), lam,
                       ws_ref[...], lam_init)
        o_ref[:, sl] = o.astype(o_ref.dtype)


def _attn_ctx(lam_p, q, k, v, wsub, nbatch, n, lam_init):
    blk = pl.BlockSpec((n, A_HEADS * LANES), lambda b: (b, 0))
    return pl.pallas_call(
        functools.partial(_attn_ctx_kernel, lam_init=lam_init),
        grid=(nbatch,),
        in_specs=[pl.BlockSpec((4, A_HEAD_DIM), lambda b: (0, 0)), blk, blk, blk,
                  pl.BlockSpec((1, A_V_DIM), lambda b: (0, 0))],
        out_specs=blk,
        out_shape=jax.ShapeDtypeStruct(q.shape, BF16),
        compiler_params=_cparams(("parallel",), VMEM_LIMIT),
        name="attn_ctx",
    )(lam_p, q, k, v, wsub)


def _rope(x, cos, sin_signed):
    lane = lax.broadcasted_iota(I32, x.shape, 1)
    quarter = A_HEAD_DIM // 4
    partner = jnp.where((lane % (2 * quarter)) < quarter,
                        pltpu.roll(x, LANES - quarter, axis=1),
                        pltpu.roll(x, quarter, axis=1))
    return x * cos + partner * sin_signed


def _attn_lat_kernel(lp_ref, q_ref, k_ref, v_ref, kc_ref, vc_ref, cos_ref, sin_ref, ws_ref, o_ref, *, lam_init, tq):
    lam = _lam(lp_ref, lam_init)
    k = _rope(k_ref[...], cos_ref[...], sin_ref[...])
    k_all = jnp.concatenate([kc_ref[...].astype(BF16), k.astype(BF16)], axis=0)
    v_all = jnp.concatenate([vc_ref[...].astype(BF16), v_ref[...].astype(BF16)], axis=0)
    for i in range(q_ref.shape[0] // tq):
        rows = slice(i * tq, (i + 1) * tq)
        q = _rope(q_ref[rows, :].astype(F32), cos_ref[rows, :], sin_ref[rows, :])
        o_ref[rows, :] = _diff_attn(q, k_all, v_all, lam, ws_ref[...], lam_init).astype(o_ref.dtype)


def _attn_lat(lam_p, q, k, v, kc, vc, cos, sin, wsub, nbatch, n, npast, lam_init):
    blk = pl.BlockSpec((n, LANES), lambda b, h: (b, h))
    cblk = pl.BlockSpec((npast, LANES), lambda b, h: (b, h))
    tab = pl.BlockSpec((n, LANES), lambda b, h: (0, 0))
    return pl.pallas_call(
        functools.partial(_attn_lat_kernel, lam_init=lam_init, tq=256),
        grid=(nbatch, A_HEADS),
        in_specs=[pl.BlockSpec((4, A_HEAD_DIM), lambda b, h: (0, 0)), blk, blk, blk, cblk, cblk, tab, tab,
                  pl.BlockSpec((1, A_V_DIM), lambda b, h: (0, 0))],
        out_specs=blk,
        out_shape=jax.ShapeDtypeStruct(q.shape, BF16),
        compiler_params=_cparams(("parallel", "parallel"), VMEM_LIMIT),
        name="attn_lat",
    )(lam_p, q, k, v, kc, vc, cos, sin, wsub)


def _rope_tables(n):
    pos = jnp.arange(n)
    rowcol = jnp.stack([pos // GRID_W, pos % GRID_W], axis=-1).astype(F32)
    nf = A_HEAD_DIM // 4
    inv_freq = jnp.power(ROPE_BASE, -jnp.arange(nf, dtype=F32) / nf)
    lane = jnp.arange(LANES)
    axis = (lane % A_HEAD_DIM) // (2 * nf)
    ang = rowcol[:, axis] * inv_freq[lane % nf][None, :]
    sign = jnp.where((lane % (2 * nf)) < nf, -1.0, 1.0).astype(F32)
    return jnp.cos(ang), jnp.sin(ang) * sign[None, :]


def _conv_kernel(x_ref, w_ref, b_ref, o_ref):
    x = x_ref[...].astype(F32)
    n = x.shape[0]
    row = lax.broadcasted_iota(I32, x.shape, 0)
    prev = jnp.where(row == 0, 0.0, pltpu.roll(x, 1, axis=0))
    nxt = jnp.where(row == n - 1, 0.0, pltpu.roll(x, n - 1, axis=0))
    w = w_ref[...]
    o_ref[...] = _silu(prev * w[0:1] + x * w[1:2] + nxt * w[2:3] + b_ref[...]).astype(o_ref.dtype)


def _conv(xbc, conv_w, conv_b, nbatch, n):
    tc = 1024
    c = xbc.shape[1]
    return pl.pallas_call(
        _conv_kernel,
        grid=(nbatch, c // tc),
        in_specs=[pl.BlockSpec((n, tc), lambda b, j: (b, j)),
                  pl.BlockSpec((3, tc), lambda b, j: (0, j)),
                  pl.BlockSpec((1, tc), lambda b, j: (0, j))],
        out_specs=pl.BlockSpec((n, tc), lambda b, j: (b, j)),
        out_shape=jax.ShapeDtypeStruct(xbc.shape, BF16),
        compiler_params=_cparams(("parallel", "parallel"), VMEM_LIMIT),
        name="conv",
    )(xbc, conv_w, conv_b.reshape(1, c))


S_PAIRS = S_INNER // LANES


def _ssd_direction(xa_ref, dt_raw, dt_raw_t, dtb_row, alog_row, dtb_col, alog_col, ht_ref, y_ref, backward):
    col0 = S_HEADS if backward else 0
    row = lax.broadcasted_iota(I32, (S_CHUNK, S_CHUNK), 0)
    col = lax.broadcasted_iota(I32, (S_CHUNK, S_CHUNK), 1)
    upper = jnp.where(row <= col, 1.0, 0.0).astype(BF16)
    lower = jnp.where(row >= col, 1.0, 0.0).astype(BF16)
    dt = jax.nn.softplus(dt_raw + dtb_row)
    dt_t = jax.nn.softplus(dt_raw_t + dtb_col)
    da = dt * (-jnp.exp(alog_row) * LOG2E)
    da_t = dt_t * (-jnp.exp(alog_col) * LOG2E)
    if backward:
        cum = _dot_f32_rhs(upper, da)
        cum_t = _dot_f32_lhs(da_t, lower)
        last = 0
        keep = row <= col
    else:
        cum = _dot_f32_rhs(lower, da)
        cum_t = _dot_f32_lhs(da_t, upper)
        last = S_CHUNK - 1
        keep = row >= col
    w_t = dt_t * jnp.exp2(cum_t[:, last:last + 1] - cum_t)
    e_tot = jnp.exp2(cum[last:last + 1, :])
    lo_half = col < S_HEADDIM
    lo_bf = jnp.where(lo_half, 1.0, 0.0).astype(BF16)
    hi_bf = jnp.where(lo_half, 0.0, 1.0).astype(BF16)
    for g in range(S_GROUPS):
        b_bf = xa_ref[:, S_INNER + g * S_STATE:S_INNER + (g + 1) * S_STATE]
        c_bf = xa_ref[:, S_INNER + (S_GROUPS + g) * S_STATE:S_INNER + (S_GROUPS + g + 1) * S_STATE]
        cb = _dot_nt(c_bf, b_bf)
        bt = b_bf.astype(F32).T
        c_f = c_bf.astype(F32)
        for pr in range(S_PAIRS // S_GROUPS):
            hp = g * (S_PAIRS // S_GROUPS) + pr
            c0 = col0 + 2 * hp
            x_pair = xa_ref[:, hp * LANES:(hp + 1) * LANES]
            bd_x = jnp.concatenate([x_pair * lo_bf, x_pair * hi_bf], axis=0)
            ht = ht_ref[hp]
            bd_h = jnp.concatenate([jnp.where(lo_half, ht, 0.0), jnp.where(lo_half, 0.0, ht)], axis=0).astype(BF16)
            ms, ces, bws = [], [], []
            for cc in (c0, c0 + 1):
                bc = jnp.broadcast_to(cum[:, cc:cc + 1], (S_CHUNK, S_CHUNK))
                decay = jnp.exp2(jnp.where(keep, bc - cum_t[cc:cc + 1, :], -1e30))
                ms.append((cb * decay * dt_t[cc:cc + 1, :]).astype(BF16))
                ces.append((c_f * jnp.exp2(bc)).astype(BF16))
                bws.append((bt * w_t[cc:cc + 1, :]).astype(BF16))
            y_ref[:, hp * LANES:(hp + 1) * LANES] = _dot(jnp.concatenate(ms + ces, axis=1),
                                                         jnp.concatenate([bd_x, bd_h], axis=0))
            tot = jnp.where(lo_half[0:1], e_tot[:, c0:c0 + 1], e_tot[:, c0 + 1:c0 + 2])
            ht_ref[hp] = ht * tot + _dot(jnp.concatenate(bws, axis=1), bd_x)


def _ssd_kernel(*refs, has_h0, emit_final):
    xf_ref, xb_ref, dtf_ref, dtb_ref, dtft_ref, dtbt_ref, br_ref, ar_ref, bc_ref, ac_ref = refs[:10]
    refs = refs[10:]
    if has_h0:
        h0_ref, refs = refs[0], refs[1:]
    yf_ref, yb_ref = refs[:2]
    refs = refs[2:]
    if emit_final:
        fin_ref, refs = refs[0], refs[1:]
    htf_ref, htb_ref = refs
    c = pl.program_id(1)
    nc = pl.num_programs(1)

    @pl.when(c == 0)
    def _():
        for d, ht_ref in enumerate((htf_ref, htb_ref)):
            for hp in range(S_PAIRS):
                if has_h0:
                    ht_ref[hp] = h0_ref[0, d, hp * LANES:(hp + 1) * LANES, :].T
                else:
                    ht_ref[hp] = jnp.zeros((S_STATE, LANES), F32)

    _ssd_direction(xf_ref, dtf_ref[...], dtft_ref[...], br_ref[...], ar_ref[...], bc_ref[...], ac_ref[...],
                   htf_ref, yf_ref, backward=False)
    _ssd_direction(xb_ref, dtb_ref[...], dtbt_ref[...], br_ref[...], ar_ref[...], bc_ref[...], ac_ref[...],
                   htb_ref, yb_ref, backward=True)

    if emit_final:
        @pl.when(c == nc - 1)
        def _():
            for d, ht_ref in enumerate((htf_ref, htb_ref)):
                for hp in range(S_PAIRS):
                    fin_ref[0, d, hp * LANES:(hp + 1) * LANES, :] = ht_ref[hp].T


def _ssd(xa, dt, dt_t, dtb_row, alog_row, dtb_col, alog_col, h0, nbatch, n, emit_final):
    nc = n // S_CHUNK
    t = xa.shape[0]
    fwd = lambda b, c: (b * nc + c, 0)
    bwd = lambda b, c: (b * nc + nc - 1 - c, 0)
    const = lambda b, c: (0, 0)
    in_specs = [pl.BlockSpec((S_CHUNK, S_CONV_CH), fwd),
                pl.BlockSpec((S_CHUNK, S_CONV_CH), bwd),
                pl.BlockSpec((S_CHUNK, LANES), fwd),
                pl.BlockSpec((S_CHUNK, LANES), bwd),
                pl.BlockSpec((LANES, S_CHUNK), lambda b, c: (0, b * nc + c)),
                pl.BlockSpec((LANES, S_CHUNK), lambda b, c: (0, b * nc + nc - 1 - c)),
                pl.BlockSpec((1, LANES), const), pl.BlockSpec((1, LANES), const),
                pl.BlockSpec((LANES, 1), const), pl.BlockSpec((LANES, 1), const)]
    args = [xa, xa, dt, dt, dt_t, dt_t, dtb_row, alog_row, dtb_col, alog_col]
    if h0 is not None:
        in_specs.append(pl.BlockSpec((1, 2, S_INNER, S_STATE), lambda b, c: (b, 0, 0, 0)))
        args.append(h0)
    out_specs = [pl.BlockSpec((S_CHUNK, S_INNER), fwd), pl.BlockSpec((S_CHUNK, S_INNER), bwd)]
    out_shape = [jax.ShapeDtypeStruct((t, S_INNER), F32), jax.ShapeDtypeStruct((t, S_INNER), F32)]
    if emit_final:
        out_specs.append(pl.BlockSpec((1, 2, S_INNER, S_STATE), lambda b, c: (b, 0, 0, 0)))
        out_shape.append(jax.ShapeDtypeStruct((nbatch, 2, S_INNER, S_STATE), F32))
    return pl.pallas_call(
        functools.partial(_ssd_kernel, has_h0=h0 is not None, emit_final=emit_final),
        grid=(nbatch, nc),
        in_specs=in_specs,
        out_specs=out_specs,
        out_shape=out_shape,
        scratch_shapes=[pltpu.VMEM((S_PAIRS, S_STATE, LANES), F32),
                        pltpu.VMEM((S_PAIRS, S_STATE, LANES), F32)],
        compiler_params=_cparams(("parallel", "arbitrary"), VMEM_LIMIT),
        name="ssd",
    )(*args)


def _mixout_kernel(yf_ref, yb_ref, xs_ref, z_ref, oa_ref, ga_ref, gs_ref, x_ref, g1_ref, sh2_ref, sc2_ref,
                   dskip_ref, wn_ref, woa_ref, wos_ref, wout_ref, gpost_ref, gpre_ref, wr_ref,
                   x1_ref, h2_ref, aff_ref):
    y = yf_ref[...] + yb_ref[...] + xs_ref[...].astype(F32) * dskip_ref[...]
    y = y * _silu(z_ref[...].astype(F32))
    wn = wn_ref[...]
    parts = []
    for g in range(S_GROUPS):
        sl = slice(g * S_GROUP_W, (g + 1) * S_GROUP_W)
        parts.append(_rms(y[:, sl], wn[:, sl]).astype(BF16))
    o_ssd = jnp.concatenate(parts, axis=1)
    a = _dot(oa_ref[...], woa_ref[...])
    s = _dot(o_ssd, wos_ref[...])
    merged = jax.nn.sigmoid(ga_ref[...].astype(F32)) * a + jax.nn.sigmoid(gs_ref[...].astype(F32)) * s
    ym = _dot(merged.astype(BF16), wout_ref[...])
    x1 = x_ref[...] + g1_ref[0] * _rms(ym, gpost_ref[...])
    x1_ref[...] = x1
    h2 = _rms(x1, gpre_ref[...]) * (1.0 + sc2_ref[0]) + sh2_ref[0]
    h2_ref[...] = h2.astype(BF16)
    logits = _dot_hi(h2, wr_ref[...])
    lane = lax.broadcasted_iota(I32, logits.shape, 1)
    logits = jnp.where(lane < N_EXPERTS, logits, -1e30)
    p = jnp.exp(logits - jnp.max(logits, axis=-1, keepdims=True))
    aff_ref[...] = p / jnp.sum(p, axis=-1, keepdims=True)


def _mixout(yf, yb, xa, z, o_attn, gates, x, mod3, dskip, wn, woa, wos, wout, gpost, gpre, wr, row0, rows_per_cond):
    t = x.shape[0]
    tm = 256
    tok = lambda w: pl.BlockSpec((tm, w), lambda i: (i, 0))
    const = lambda a: pl.BlockSpec(a.shape, lambda i: (0,) * a.ndim)
    return pl.pallas_call(
        _mixout_kernel,
        grid=(t // tm,),
        in_specs=[tok(S_INNER), tok(S_INNER), tok(S_INNER), tok(S_INNER), tok(D_MODEL),
                  pl.BlockSpec((tm, D_MODEL), lambda i: (i, 0)),
                  pl.BlockSpec((tm, D_MODEL), lambda i: (i, 1)),
                  tok(D_MODEL),
                  _mod_spec(2, tm, row0, rows_per_cond),
                  _mod_spec(3, tm, row0, rows_per_cond),
                  _mod_spec(4, tm, row0, rows_per_cond),
                  const(dskip), const(wn), const(woa), const(wos), const(wout), const(gpost), const(gpre),
                  const(wr)],
        out_specs=[tok(D_MODEL), tok(D_MODEL), tok(LANES)],
        out_shape=[jax.ShapeDtypeStruct((t, D_MODEL), F32), jax.ShapeDtypeStruct((t, D_MODEL), BF16),
                   jax.ShapeDtypeStruct((t, LANES), F32)],
        compiler_params=_cparams(("parallel",), VMEM_LIMIT),
        name="mixout",
    )(yf, yb, xa, z, o_attn, gates, gates, x, mod3, mod3, mod3, dskip, wn, woa, wos, wout, gpost, gpre, wr)


def _router_kernel(aff_ref, pos_ref, *, cap):
    aff = aff_ref[...]
    e, t = aff.shape
    idx = lax.broadcasted_iota(I32, (e, t), 1)

    def count(m):
        return jnp.sum(jnp.where(m, 1.0, 0.0), axis=-1, keepdims=True)

    def value_bit(i, thr):
        cand = thr | jnp.left_shift(jnp.int32(1), 30 - i)
        return jnp.where(count(aff >= pltpu.bitcast(cand, F32)) >= cap, cand, thr)

    thr = pltpu.bitcast(lax.fori_loop(0, 31, value_bit, jnp.zeros((e, 1), I32)), F32)
    gt = aff > thr
    eq = aff == thr
    need = cap - count(gt)
    nbits = (t - 1).bit_length()

    def index_bit(i, v):
        cand = v | jnp.left_shift(jnp.int32(1), nbits - 1 - i)
        return jnp.where(count(jnp.logical_and(eq, idx < cand)) < need, cand, v)

    last = lax.fori_loop(0, nbits, index_bit, jnp.zeros((e, 1), I32))
    sel = jnp.logical_or(gt, jnp.logical_and(eq, idx <= last))
    sel_bf = jnp.where(sel, 1.0, 0.0).astype(BF16)
    blk = 512
    r = lax.broadcasted_iota(I32, (blk, blk), 0)
    c = lax.broadcasted_iota(I32, (blk, blk), 1)
    upper = jnp.where(r <= c, 1.0, 0.0).astype(BF16)
    carry = jnp.zeros((e, 1), F32)
    for b in range(t // blk):
        cum = _dot(sel_bf[:, b * blk:(b + 1) * blk], upper) + carry
        carry = cum[:, blk - 1:blk]
        slot = jnp.where(sel[:, b * blk:(b + 1) * blk], cum - 1.0, -1.0)
        pos_ref[:, b * blk:(b + 1) * blk] = slot.astype(I32)


def _router(aff_t, cap):
    e, t = aff_t.shape
    return pl.pallas_call(
        functools.partial(_router_kernel, cap=cap),
        out_shape=jax.ShapeDtypeStruct((e, t), I32),
        name="router",
    )(aff_t)


def _dispatch_kernel(pos_ref, h_ref, o_ref, acc_ref, *, cap):
    t = h_ref.shape[0]
    tk = 512
    slot = lax.broadcasted_iota(I32, (cap, tk), 0)
    for kb in range(t // tk):
        onehot = jnp.where(pos_ref[0, :, kb * tk:(kb + 1) * tk] == slot, 1.0, 0.0).astype(BF16)
        part = _dot(onehot, h_ref[kb * tk:(kb + 1) * tk, :])
        if kb == 0:
            acc_ref[...] = part
        else:
            acc_ref[...] += part
    o_ref[0] = acc_ref[...].astype(BF16)


def _dispatch(pos, h2, cap):
    e, t = pos.shape
    return pl.pallas_call(
        functools.partial(_dispatch_kernel, cap=cap),
        grid=(e,),
        in_specs=[pl.BlockSpec((1, 1, t), lambda i: (i, 0, 0)),
                  pl.BlockSpec((t, D_MODEL), lambda i: (0, 0))],
        out_specs=pl.BlockSpec((1, cap, D_MODEL), lambda i: (i, 0, 0)),
        out_shape=jax.ShapeDtypeStruct((e, cap, D_MODEL), BF16),
        scratch_shapes=[pltpu.VMEM((cap, D_MODEL), F32)],
        compiler_params=_cparams(("parallel",), VMEM_LIMIT),
        name="dispatch",
    )(pos.reshape(e, 1, t), h2)


def _ffn_kernel(xc_ref, xl_ref, wg_ref, wu_ref, wd_ref, oc_ref, ol_ref, acc_ref):
    f = pl.program_id(1)
    cc = xc_ref.shape[1]
    xe = jnp.concatenate([xc_ref[0], xl_ref[0]], axis=0)
    hg = _dot(xe, wg_ref[0].astype(BF16))
    hu = _dot(xe, wu_ref[0].astype(BF16))
    part = _dot((_silu(hg) * hu).astype(BF16), wd_ref[0].astype(BF16))

    @pl.when(f == 0)
    def _():
        acc_ref[...] = part

    @pl.when(f > 0)
    def _():
        acc_ref[...] += part

    @pl.when(f == pl.num_programs(1) - 1)
    def _():
        oc_ref[0] = acc_ref[:cc, :].astype(BF16)
        ol_ref[0] = acc_ref[cc:, :].astype(BF16)


def _ffn(xe_c, xe_l, w_gate, w_up, w_down):
    e, cc, d = xe_c.shape
    cl = xe_l.shape[1]
    tf = 512
    return pl.pallas_call(
        _ffn_kernel,
        grid=(e, D_FF // tf),
        in_specs=[pl.BlockSpec((1, cc, d), lambda i, f: (i, 0, 0)),
                  pl.BlockSpec((1, cl, d), lambda i, f: (i, 0, 0)),
                  pl.BlockSpec((1, d, tf), lambda i, f: (i, 0, f)),
                  pl.BlockSpec((1, d, tf), lambda i, f: (i, 0, f)),
                  pl.BlockSpec((1, tf, d), lambda i, f: (i, f, 0))],
        out_specs=[pl.BlockSpec((1, cc, d), lambda i, f: (i, 0, 0)),
                   pl.BlockSpec((1, cl, d), lambda i, f: (i, 0, 0))],
        out_shape=[jax.ShapeDtypeStruct((e, cc, d), BF16), jax.ShapeDtypeStruct((e, cl, d), BF16)],
        scratch_shapes=[pltpu.VMEM((cc + cl, d), F32)],
        compiler_params=_cparams(("parallel", "arbitrary"), VMEM_LIMIT),
        name="ffn",
    )(xe_c, xe_l, w_gate, w_up, w_down)


def _combine_kernel(pos_ref, aff_ref, ye_ref, x1_ref, g2_ref, gp_ref, o_ref):
    tm = pos_ref.shape[0]
    cap = ye_ref.shape[1]
    pos = pos_ref[...]
    aff = aff_ref[...]
    slot = lax.broadcasted_iota(I32, (tm, cap), 1)
    y = jnp.zeros((tm, D_MODEL), F32)
    for e in range(N_EXPERTS):
        onehot = jnp.where(pos[:, e:e + 1] == slot, 1.0, 0.0).astype(BF16)
        y = y + aff[:, e:e + 1] * _dot(onehot, ye_ref[e])
    o_ref[...] = x1_ref[...] + g2_ref[0] * _rms(y, gp_ref[...])


def _combine(pos_t, aff, ye, x1, mod3, gpost, row0, rows_per_cond):
    t = x1.shape[0]
    e, cap, d = ye.shape
    tm = 256
    return pl.pallas_call(
        _combine_kernel,
        grid=(t // tm,),
        in_specs=[pl.BlockSpec((tm, e), lambda i: (i, 0)),
                  pl.BlockSpec((tm, LANES), lambda i: (i, 0)),
                  pl.BlockSpec((e, cap, d), lambda i: (0, 0, 0)),
                  pl.BlockSpec((tm, d), lambda i: (i, 0)),
                  _mod_spec(5, tm, row0, rows_per_cond),
                  pl.BlockSpec((1, d), lambda i: (0, 0))],
        out_specs=pl.BlockSpec((tm, d), lambda i: (i, 0)),
        out_shape=jax.ShapeDtypeStruct((t, d), F32),
        compiler_params=_cparams(("parallel",), VMEM_LIMIT),
        name="combine",
    )(pos_t, aff, ye, x1, mod3, gpost.reshape(1, d))


def _head_rows(v):
    flat = jnp.concatenate([v.reshape(2 * S_HEADS), jnp.zeros((LANES - 2 * S_HEADS,), F32)])
    return flat.reshape(1, LANES), flat.reshape(LANES, 1)


def kernel(x_prompt, x_sample, cache_k, cache_v, state_ssd, c, c_ctx, w_ada, b_ada, g_pre_mix, g_post_mix, g_pre_ffn, g_post_ffn, w_in, lam_q1, lam_k1, lam_q2, lam_k2, w_subln, conv_w, conv_b, dt_bias, a_log, d_skip, w_ssd_norm, w_o_attn, w_o_ssd, w_out, w_router, w_gate, w_up, w_down):
    nb_c, n_c, d = x_prompt.shape
    nb_l, n_l, _ = x_sample.shape
    npast = cache_k.shape[2]
    depth = w_in.shape[0]
    assert depth == 1 and d == D_MODEL
    t_c, t_l = nb_c * n_c, nb_l * n_l
    lyr = 0
    lam_init = 0.8 - 0.6 * math.exp(-0.3 * lyr)

    cond8 = jnp.concatenate([c_ctx[None, :], c, jnp.zeros((8 - 1 - nb_l, d), F32)], axis=0)
    mod3 = _ada(cond8, w_ada[lyr], b_ada[lyr]).reshape(8, 1, N_MOD * d)

    q_off, k_off, v_off, z_off, xbc_off = 0, 1024, 2048, 3072, 5120
    dt_off = xbc_off + S_CONV_CH
    g_off = dt_off + 2 * S_HEADS
    w_in_l = w_in[lyr]
    w_gates = w_in_l[:, g_off:]
    w_dt = jnp.concatenate([w_in_l[:, dt_off:g_off], jnp.zeros((d, LANES - 2 * S_HEADS), F32)], axis=1).astype(BF16)
    w_dt_t = w_dt.T
    lam_p = jnp.stack([lam_q1[lyr], lam_k1[lyr], lam_q2[lyr], lam_k2[lyr]], axis=0)
    wsub = w_subln[lyr].reshape(1, A_V_DIM)
    dtb_row, dtb_col = _head_rows(dt_bias[lyr])
    alog_row, alog_col = _head_rows(a_log[lyr])
    dskip = jnp.repeat(d_skip[lyr], S_HEADDIM).reshape(1, S_INNER)
    wn = w_ssd_norm[lyr].reshape(1, S_INNER)
    woa = w_o_attn[lyr].astype(BF16)
    wos = w_o_ssd[lyr].astype(BF16)
    wout = w_out[lyr].astype(BF16)
    wr = jnp.concatenate([w_router[lyr], jnp.zeros((d, LANES - N_EXPERTS), F32)], axis=1)
    gpost = g_post_mix[lyr].reshape(1, d)
    gpre = g_pre_ffn[lyr].reshape(1, d)
    cos, sin = _rope_tables(n_l)

    def mixer(x, row0, rows_per_cond, nbatch, n, ctx):
        h = _prenorm(x, mod3, g_pre_mix[lyr], row0, rows_per_cond)
        q = _matmul(h, w_in_l, q_off, 1024, BF16)
        k = _matmul(h, w_in_l, k_off, 1024, F32)
        v = _matmul(h, w_in_l, v_off, 1024, F32)
        z = _matmul(h, w_in_l, z_off, S_INNER, BF16)
        xbc = _matmul(h, w_in_l, xbc_off, S_CONV_CH, BF16)
        gates = _matmul(h, w_gates, 0, 2 * d, BF16)
        dt, dt_t = _dtproj(h, w_dt, w_dt_t)
        if ctx is None:
            o_attn = _attn_ctx(lam_p, q, k, v, wsub, nbatch, n, lam_init)
            h0 = None
        else:
            kc, vc, h0 = ctx
            o_attn = _attn_lat(lam_p, q, k, v, kc, vc, cos, sin, wsub, nbatch, n, npast, lam_init)
        xa = _conv(xbc, conv_w[lyr], conv_b[lyr], nbatch, n)
        res = _ssd(xa, dt, dt_t, dtb_row, alog_row, dtb_col, alog_col, h0, nbatch, n, emit_final=ctx is None)
        yf, yb = res[0], res[1]
        x1, h2, aff = _mixout(yf, yb, xa, z, o_attn, gates, x, mod3, dskip, wn, woa, wos, wout, gpost, gpre, wr,
                              row0, rows_per_cond)
        fin = res[2] if ctx is None else None
        return x1, h2, aff, k, v, fin

    xc = x_prompt.reshape(t_c, d)
    xl = x_sample.reshape(t_l, d)
    x1_c, h2_c, aff_c, k_c, v_c, fin_c = mixer(xc, 0, t_c, nb_c, n_c, None)
    kc = cache_k[:, lyr].reshape(nb_l * npast, A_HEADS * 2 * A_HEAD_DIM)
    vc = cache_v[:, lyr].reshape(nb_l * npast, A_HEADS * A_V_DIM)
    h0 = state_ssd[:, lyr].reshape(nb_l, 2, S_INNER, S_STATE)
    x1_l, h2_l, aff_l, _, _, _ = mixer(xl, 1, n_l, nb_l, n_l, (kc, vc, h0))

    cap_c = EC_FACTOR * t_c // N_EXPERTS
    cap_l = EC_FACTOR * t_l // N_EXPERTS
    pos_c = _router(aff_c[:, :N_EXPERTS].T, cap_c)
    pos_l = _router(aff_l[:, :N_EXPERTS].T, cap_l)
    xe_c = _dispatch(pos_c, h2_c, cap_c)
    xe_l = _dispatch(pos_l, h2_l, cap_l)
    ye_c, ye_l = _ffn(xe_c, xe_l, w_gate[lyr], w_up[lyr], w_down[lyr])
    out_c = _combine(pos_c.T, aff_c, ye_c, x1_c, mod3, g_post_ffn[lyr], 0, t_c)
    out_l = _combine(pos_l.T, aff_l, ye_l, x1_l, mod3, g_post_ffn[lyr], 1, n_l)

    y_prompt = out_c.reshape(nb_c, n_c, d)
    y_sample = out_l.reshape(nb_l, n_l, d)
    new_k = k_c.reshape(nb_c, 1, n_c, A_HEADS, 2 * A_HEAD_DIM)
    new_v = v_c.reshape(nb_c, 1, n_c, A_HEADS, A_V_DIM)
    new_state = fin_c.reshape(nb_c, 1, 2, S_HEADS, S_HEADDIM, S_STATE)
    return (y_prompt, y_sample, new_k, new_v, new_state)
```

```python
import functools
import math

import jax
import jax.numpy as jnp
import numpy as np
from jax import lax
from jax.experimental import pallas as pl
from jax.experimental.pallas import tpu as pltpu

F32 = jnp.float32
BF16 = jnp.bfloat16
I32 = jnp.int32

D_MODEL = 1024
GRID_W = 64
EPS = 1e-6
N_MOD = 6
A_HEADS = 8
A_HEAD_DIM = 64
A_V_DIM = 128
ROPE_BASE = 10000.0
S_INNER = 2048
S_HEADDIM = 64
S_HEADS = 32
S_GROUPS = 4
S_GROUP_W = S_INNER // S_GROUPS
S_STATE = 128
S_CHUNK = 128
S_CONV_CH = S_INNER + 2 * S_GROUPS * S_STATE
N_EXPERTS = 16
EC_FACTOR = 2
D_FF = 2048
LANES = 128
TAIL_W = 2560
DT_COL = 2048
VMEM_LIMIT = 56 * 1024 * 1024
LOG2E = 1.4426950408889634


def _cparams(sem, vmem=None):
    return pltpu.CompilerParams(dimension_semantics=sem, vmem_limit_bytes=vmem)


def _dot(a, b):
    return jnp.dot(a, b, preferred_element_type=F32)


def _dot_nt(a, b):
    return lax.dot_general(a, b, (((1,), (1,)), ((), ())), preferred_element_type=F32)


def _split3(a):
    a1 = a.astype(BF16)
    r1 = a - a1.astype(F32)
    a2 = r1.astype(BF16)
    a3 = (r1 - a2.astype(F32)).astype(BF16)
    return a1, a2, a3


def _dot_f32_lhs(a, m):
    a1, a2, a3 = _split3(a)
    return _dot(a1, m) + (_dot(a2, m) + _dot(a3, m))


def _dot_f32_rhs(m, b):
    b1, b2, b3 = _split3(b)
    return _dot(m, b1) + (_dot(m, b2) + _dot(m, b3))


def _dot_hi(a, b):
    a1, a2, _ = _split3(a)
    b1, b2, _ = _split3(b)
    return _dot(a1, b1) + (_dot(a1, b2) + _dot(a2, b1))


def _silu(x):
    return x * jax.nn.sigmoid(x)


def _rms(x, w):
    return x * lax.rsqrt(jnp.mean(x * x, axis=-1, keepdims=True) + EPS) * w


def _ada_kernel(c_ref, w_ref, b_ref, o_ref):
    o_ref[...] = _dot_hi(_silu(c_ref[...]), w_ref[...]) + b_ref[...]


def _ada(cond8, w_ada, b_ada):
    n = w_ada.shape[1]
    tn = 1536
    return pl.pallas_call(
        _ada_kernel,
        grid=(n // tn,),
        in_specs=[pl.BlockSpec((8, D_MODEL), lambda j: (0, 0)),
                  pl.BlockSpec((D_MODEL, tn), lambda j: (0, j)),
                  pl.BlockSpec((1, tn), lambda j: (0, j))],
        out_specs=pl.BlockSpec((8, tn), lambda j: (0, j)),
        out_shape=jax.ShapeDtypeStruct((8, n), F32),
        compiler_params=_cparams(("arbitrary",), VMEM_LIMIT),
        name="ada",
    )(cond8, w_ada, b_ada.reshape(1, n))


def _mod_spec(col, tm, row0, rows_per_cond):
    return pl.BlockSpec((1, 1, D_MODEL), lambda i, *_: (row0 + (i * tm) // rows_per_cond, 0, col))


def _prenorm_kernel(x_ref, sh_ref, sc_ref, g_ref, o_ref):
    xn = _rms(x_ref[...], g_ref[...])
    o_ref[...] = (xn * (1.0 + sc_ref[0]) + sh_ref[0]).astype(o_ref.dtype)


def _prenorm(x, mod3, gain, row0, rows_per_cond):
    t = x.shape[0]
    tm = 512
    return pl.pallas_call(
        _prenorm_kernel,
        grid=(t // tm,),
        in_specs=[pl.BlockSpec((tm, D_MODEL), lambda i: (i, 0)),
                  _mod_spec(0, tm, row0, rows_per_cond),
                  _mod_spec(1, tm, row0, rows_per_cond),
                  pl.BlockSpec((1, D_MODEL), lambda i: (0, 0))],
        out_specs=pl.BlockSpec((tm, D_MODEL), lambda i: (i, 0)),
        out_shape=jax.ShapeDtypeStruct((t, D_MODEL), BF16),
        compiler_params=_cparams(("parallel",)),
        name="prenorm",
    )(x, mod3, mod3, gain.reshape(1, D_MODEL))


def _mm_kernel(a_ref, w_ref, o_ref):
    o_ref[...] = _dot_nt(a_ref[...], w_ref[...].astype(BF16)).astype(o_ref.dtype)


def _matmul(a, w_t, col0, ncols, out_dtype, tn=1024):
    t, k = a.shape
    tm = 2048
    cb = col0 // tn
    assert col0 % tn == 0 and ncols % tn == 0 and t % tm == 0
    return pl.pallas_call(
        _mm_kernel,
        grid=(ncols // tn, t // tm),
        in_specs=[pl.BlockSpec((tm, k), lambda j, i: (i, 0)),
                  pl.BlockSpec((tn, k), lambda j, i: (j + cb, 0))],
        out_specs=pl.BlockSpec((tm, tn), lambda j, i: (i, j)),
        out_shape=jax.ShapeDtypeStruct((t, ncols), out_dtype),
        compiler_params=_cparams(("parallel", "arbitrary"), VMEM_LIMIT),
        name="inproj",
    )(a, w_t)


def _dtproj_kernel(h_ref, w_ref, wt_ref, o_ref, ot_ref):
    h = h_ref[...]
    o_ref[...] = _dot(h, w_ref[...])
    ot_ref[...] = _dot_nt(wt_ref[...], h)


def _dtproj(h, w_dt, w_dt_t):
    t, k = h.shape
    tm = 1024
    return pl.pallas_call(
        _dtproj_kernel,
        grid=(t // tm,),
        in_specs=[pl.BlockSpec((tm, k), lambda i: (i, 0)),
                  pl.BlockSpec((k, LANES), lambda i: (0, 0)),
                  pl.BlockSpec((LANES, k), lambda i: (0, 0))],
        out_specs=[pl.BlockSpec((tm, LANES), lambda i: (i, 0)),
                   pl.BlockSpec((LANES, tm), lambda i: (0, i))],
        out_shape=[jax.ShapeDtypeStruct((t, LANES), F32), jax.ShapeDtypeStruct((LANES, t), F32)],
        compiler_params=_cparams(("parallel",)),
        name="dtproj",
    )(h, w_dt, w_dt_t)


def _lam(lp_ref, lam_init):
    lp = lp_ref[...]
    s1 = jnp.sum(lp[0:1] * lp[1:2], axis=-1, keepdims=True)
    s2 = jnp.sum(lp[2:3] * lp[3:4], axis=-1, keepdims=True)
    return jnp.exp(s1) - jnp.exp(s2) + lam_init


def _diff_attn(q, k, v, lam, wsub, lam_init):
    nq = q.shape[0]
    lane = lax.broadcasted_iota(I32, q.shape, 1)
    q = q * (A_HEAD_DIM ** -0.5 * LOG2E)
    q1 = jnp.where(lane < A_HEAD_DIM, q, 0.0).astype(BF16)
    q2 = jnp.where(lane >= A_HEAD_DIM, q, 0.0).astype(BF16)
    s = _dot_nt(jnp.concatenate([q1, q2], axis=0), k)
    p = jnp.exp2(s - jnp.max(s, axis=-1, keepdims=True))
    r = 1.0 / jnp.sum(p, axis=-1, keepdims=True)
    pv = _dot(p.astype(BF16), v) * r
    o = pv[:nq] - lam * pv[nq:]
    return _rms(o, wsub) * (1.0 - lam_init)


def _attn_ctx_kernel(lp_ref, q_ref, k_ref, v_ref, ws_ref, o_ref, *, lam_init):
    lam = _lam(lp_ref, lam_init)
    for h in range(A_HEADS):
        sl = slice(h * LANES, (h + 1) * LANES)
        o = _diff_attn(q_ref[:, sl].astype(F32), k_ref[:, sl].astype(BF16), v_ref[:, sl].astype(BF16), lam,
                       ws_ref[...], lam_init)
        o_ref[:, sl] = o.astype(o_ref.dtype)


def _attn_ctx(lam_p, q, k, v, wsub, nbatch, n, lam_init):
    blk = pl.BlockSpec((n, A_HEADS * LANES), lambda b: (b, 0))
    return pl.pallas_call(
        functools.partial(_attn_ctx_kernel, lam_init=lam_init),
        grid=(nbatch,),
        in_specs=[pl.BlockSpec((4, A_HEAD_DIM), lambda b: (0, 0)), blk, blk, blk,
                  pl.BlockSpec((1, A_V_DIM), lambda b: (0, 0))],
        out_specs=blk,
        out_shape=jax.ShapeDtypeStruct(q.shape, BF16),
        compiler_params=_cparams(("parallel",), VMEM_LIMIT),
        name="attn_ctx",
    )(lam_p, q, k, v, wsub)


def _rope(x, cos, sin_signed):
    lane = lax.broadcasted_iota(I32, x.shape, 1)
    quarter = A_HEAD_DIM // 4
    partner = jnp.where((lane % (2 * quarter)) < quarter,
                        pltpu.roll(x, LANES - quarter, axis=1),
                        pltpu.roll(x, quarter, axis=1))
    return x * cos + partner * sin_signed


def _attn_lat_kernel(lp_ref, q_ref, k_ref, v_ref, kc_ref, vc_ref, cos_ref, sin_ref, ws_ref, o_ref, *, lam_init, tq):
    lam = _lam(lp_ref, lam_init)
    k = _rope(k_ref[...], cos_ref[...], sin_ref[...])
    k_all = jnp.concatenate([kc_ref[...].astype(BF16), k.astype(BF16)], axis=0)
    v_all = jnp.concatenate([vc_ref[...].astype(BF16), v_ref[...].astype(BF16)], axis=0)
    for i in range(q_ref.shape[0] // tq):
        rows = slice(i * tq, (i + 1) * tq)
        q = _rope(q_ref[rows, :].astype(F32), cos_ref[rows, :], sin_ref[rows, :])
        o_ref[rows, :] = _diff_attn(q, k_all, v_all, lam, ws_ref[...], lam_init).astype(o_ref.dtype)


def _attn_lat(lam_p, q, k, v, kc, vc, cos, sin, wsub, nbatch, n, npast, lam_init):
    blk = pl.BlockSpec((n, LANES), lambda b, h: (b, h))
    cblk = pl.BlockSpec((npast, LANES), lambda b, h: (b, h))
    tab = pl.BlockSpec((n, LANES), lambda b, h: (0, 0))
    return pl.pallas_call(
        functools.partial(_attn_lat_kernel, lam_init=lam_init, tq=256),
        grid=(nbatch, A_HEADS),
        in_specs=[pl.BlockSpec((4, A_HEAD_DIM), lambda b, h: (0, 0)), blk, blk, blk, cblk, cblk, tab, tab,
                  pl.BlockSpec((1, A_V_DIM), lambda b, h: (0, 0))],
        out_specs=blk,
        out_shape=jax.ShapeDtypeStruct(q.shape, BF16),
        compiler_params=_cparams(("parallel", "parallel"), VMEM_LIMIT),
        name="attn_lat",
    )(lam_p, q, k, v, kc, vc, cos, sin, wsub)


def _rope_tables(n):
    pos = np.arange(n)
    rowcol = np.stack([pos // GRID_W, pos % GRID_W], axis=-1).astype(np.float64)
    nf = A_HEAD_DIM // 4
    inv_freq = np.power(ROPE_BASE, -np.arange(nf, dtype=np.float64) / nf)
    lane = np.arange(LANES)
    axis = (lane % A_HEAD_DIM) // (2 * nf)
    ang = rowcol[:, axis] * inv_freq[lane % nf][None, :]
    sign = np.where((lane % (2 * nf)) < nf, -1.0, 1.0)
    return jnp.asarray(np.cos(ang), F32), jnp.asarray(np.sin(ang) * sign[None, :], F32)


def _conv_kernel(x_ref, w_ref, b_ref, o_ref):
    x = x_ref[...].astype(F32)
    n = x.shape[0]
    row = lax.broadcasted_iota(I32, x.shape, 0)
    prev = jnp.where(row == 0, 0.0, pltpu.roll(x, 1, axis=0))
    nxt = jnp.where(row == n - 1, 0.0, pltpu.roll(x, n - 1, axis=0))
    w = w_ref[...]
    o_ref[...] = _silu(prev * w[0:1] + x * w[1:2] + nxt * w[2:3] + b_ref[...]).astype(o_ref.dtype)


def _conv(xbc, conv_w, conv_b, nbatch, n):
    tc = 1024
    c = xbc.shape[1]
    return pl.pallas_call(
        _conv_kernel,
        grid=(nbatch, c // tc),
        in_specs=[pl.BlockSpec((n, tc), lambda b, j: (b, j)),
                  pl.BlockSpec((3, tc), lambda b, j: (0, j)),
                  pl.BlockSpec((1, tc), lambda b, j: (0, j))],
        out_specs=pl.BlockSpec((n, tc), lambda b, j: (b, j)),
        out_shape=jax.ShapeDtypeStruct(xbc.shape, BF16),
        compiler_params=_cparams(("parallel", "parallel"), VMEM_LIMIT),
        name="conv",
    )(xbc, conv_w, conv_b.reshape(1, c))


S_PAIRS = S_INNER // LANES


def _ssd_direction(xa_ref, dt_raw, dt_raw_t, dtb_row, alog_row, dtb_col, alog_col, ht_ref, y_ref, backward):
    col0 = S_HEADS if backward else 0
    row = lax.broadcasted_iota(I32, (S_CHUNK, S_CHUNK), 0)
    col = lax.broadcasted_iota(I32, (S_CHUNK, S_CHUNK), 1)
    upper = jnp.where(row <= col, 1.0, 0.0).astype(BF16)
    lower = jnp.where(row >= col, 1.0, 0.0).astype(BF16)
    dt = jax.nn.softplus(dt_raw + dtb_row)
    dt_t = jax.nn.softplus(dt_raw_t + dtb_col)
    da = dt * (-jnp.exp(alog_row) * LOG2E)
    da_t = dt_t * (-jnp.exp(alog_col) * LOG2E)
    if backward:
        cum = _dot_f32_rhs(upper, da)
        cum_t = _dot_f32_lhs(da_t, lower)
        last = 0
        keep = row <= col
    else:
        cum = _dot_f32_rhs(lower, da)
        cum_t = _dot_f32_lhs(da_t, upper)
        last = S_CHUNK - 1
        keep = row >= col
    w_t = dt_t * jnp.exp2(cum_t[:, last:last + 1] - cum_t)
    e_tot = jnp.exp2(cum[last:last + 1, :])
    lo_half = col < S_HEADDIM
    lo_bf = jnp.where(lo_half, 1.0, 0.0).astype(BF16)
    hi_bf = jnp.where(lo_half, 0.0, 1.0).astype(BF16)
    for g in range(S_GROUPS):
        b_bf = xa_ref[:, S_INNER + g * S_STATE:S_INNER + (g + 1) * S_STATE]
        c_bf = xa_ref[:, S_INNER + (S_GROUPS + g) * S_STATE:S_INNER + (S_GROUPS + g + 1) * S_STATE]
        cb = _dot_nt(c_bf, b_bf)
        bt = b_bf.astype(F32).T
        c_f = c_bf.astype(F32)
        for pr in range(S_PAIRS // S_GROUPS):
            hp = g * (S_PAIRS // S_GROUPS) + pr
            c0 = col0 + 2 * hp
            x_pair = xa_ref[:, hp * LANES:(hp + 1) * LANES]
            bd_x = jnp.concatenate([x_pair * lo_bf, x_pair * hi_bf], axis=0)
            ht = ht_ref[hp]
            bd_h = jnp.concatenate([jnp.where(lo_half, ht, 0.0), jnp.where(lo_half, 0.0, ht)], axis=0).astype(BF16)
            ms, ces, bws = [], [], []
            for cc in (c0, c0 + 1):
                bc = jnp.broadcast_to(cum[:, cc:cc + 1], (S_CHUNK, S_CHUNK))
                decay = jnp.exp2(jnp.where(keep, bc - cum_t[cc:cc + 1, :], -1e30))
                ms.append((cb * decay * dt_t[cc:cc + 1, :]).astype(BF16))
                ces.append((c_f * jnp.exp2(bc)).astype(BF16))
                bws.append((bt * w_t[cc:cc + 1, :]).astype(BF16))
            y_ref[:, hp * LANES:(hp + 1) * LANES] = _dot(jnp.concatenate(ms + ces, axis=1),
                                                         jnp.concatenate([bd_x, bd_h], axis=0))
            tot = jnp.where(lo_half[0:1], e_tot[:, c0:c0 + 1], e_tot[:, c0 + 1:c0 + 2])
            ht_ref[hp] = ht * tot + _dot(jnp.concatenate(bws, axis=1), bd_x)


def _ssd_kernel(*refs, has_h0, emit_final):
    xf_ref, xb_ref, dtf_ref, dtb_ref, dtft_ref, dtbt_ref, br_ref, ar_ref, bc_ref, ac_ref = refs[:10]
    refs = refs[10:]
    if has_h0:
        h0_ref, refs = refs[0], refs[1:]
    yf_ref, yb_ref = refs[:2]
    refs = refs[2:]
    if emit_final:
        fin_ref, refs = refs[0], refs[1:]
    htf_ref, htb_ref = refs
    c = pl.program_id(1)
    nc = pl.num_programs(1)

    @pl.when(c == 0)
    def _():
        for d, ht_ref in enumerate((htf_ref, htb_ref)):
            for hp in range(S_PAIRS):
                if has_h0:
                    ht_ref[hp] = h0_ref[0, d, hp * LANES:(hp + 1) * LANES, :].T
                else:
                    ht_ref[hp] = jnp.zeros((S_STATE, LANES), F32)

    _ssd_direction(xf_ref, dtf_ref[...], dtft_ref[...], br_ref[...], ar_ref[...], bc_ref[...], ac_ref[...],
                   htf_ref, yf_ref, backward=False)
    _ssd_direction(xb_ref, dtb_ref[...], dtbt_ref[...], br_ref[...], ar_ref[...], bc_ref[...], ac_ref[...],
                   htb_ref, yb_ref, backward=True)

    if emit_final:
        @pl.when(c == nc - 1)
        def _():
            for d, ht_ref in enumerate((htf_ref, htb_ref)):
                for hp in range(S_PAIRS):
                    fin_ref[0, d, hp * LANES:(hp + 1) * LANES, :] = ht_ref[hp].T


def _ssd(xa, dt, dt_t, dtb_row, alog_row, dtb_col, alog_col, h0, nbatch, n, emit_final):
    nc = n // S_CHUNK
    t = xa.shape[0]
    fwd = lambda b, c: (b * nc + c, 0)
    bwd = lambda b, c: (b * nc + nc - 1 - c, 0)
    const = lambda b, c: (0, 0)
    in_specs = [pl.BlockSpec((S_CHUNK, S_CONV_CH), fwd),
                pl.BlockSpec((S_CHUNK, S_CONV_CH), bwd),
                pl.BlockSpec((S_CHUNK, LANES), fwd),
                pl.BlockSpec((S_CHUNK, LANES), bwd),
                pl.BlockSpec((LANES, S_CHUNK), lambda b, c: (0, b * nc + c)),
                pl.BlockSpec((LANES, S_CHUNK), lambda b, c: (0, b * nc + nc - 1 - c)),
                pl.BlockSpec((1, LANES), const), pl.BlockSpec((1, LANES), const),
                pl.BlockSpec((LANES, 1), const), pl.BlockSpec((LANES, 1), const)]
    args = [xa, xa, dt, dt, dt_t, dt_t, dtb_row, alog_row, dtb_col, alog_col]
    if h0 is not None:
        in_specs.append(pl.BlockSpec((1, 2, S_INNER, S_STATE), lambda b, c: (b, 0, 0, 0)))
        args.append(h0)
    out_specs = [pl.BlockSpec((S_CHUNK, S_INNER), fwd), pl.BlockSpec((S_CHUNK, S_INNER), bwd)]
    out_shape = [jax.ShapeDtypeStruct((t, S_INNER), F32), jax.ShapeDtypeStruct((t, S_INNER), F32)]
    if emit_final:
        out_specs.append(pl.BlockSpec((1, 2, S_INNER, S_STATE), lambda b, c: (b, 0, 0, 0)))
        out_shape.append(jax.ShapeDtypeStruct((nbatch, 2, S_INNER, S_STATE), F32))
    return pl.pallas_call(
        functools.partial(_ssd_kernel, has_h0=h0 is not None, emit_final=emit_final),
        grid=(nbatch, nc),
        in_specs=in_specs,
        out_specs=out_specs,
        out_shape=out_shape,
        scratch_shapes=[pltpu.VMEM((S_PAIRS, S_STATE, LANES), F32),
                        pltpu.VMEM((S_PAIRS, S_STATE, LANES), F32)],
        compiler_params=_cparams(("parallel", "arbitrary"), VMEM_LIMIT),
        name="ssd",
    )(*args)


def _mixout_kernel(yf_ref, yb_ref, xs_ref, z_ref, oa_ref, ga_ref, gs_ref, x_ref, g1_ref, sh2_ref, sc2_ref,
                   dskip_ref, wn_ref, woa_ref, wos_ref, wout_ref, gpost_ref, gpre_ref, wr_ref,
                   x1_ref, h2_ref, aff_ref):
    y = yf_ref[...] + yb_ref[...] + xs_ref[...].astype(F32) * dskip_ref[...]
    y = y * _silu(z_ref[...].astype(F32))
    wn = wn_ref[...]
    parts = []
    for g in range(S_GROUPS):
        sl = slice(g * S_GROUP_W, (g + 1) * S_GROUP_W)
        parts.append(_rms(y[:, sl], wn[:, sl]).astype(BF16))
    o_ssd = jnp.concatenate(parts, axis=1)
    a = _dot(oa_ref[...], woa_ref[...])
    s = _dot(o_ssd, wos_ref[...])
    merged = jax.nn.sigmoid(ga_ref[...].astype(F32)) * a + jax.nn.sigmoid(gs_ref[...].astype(F32)) * s
    ym = _dot(merged.astype(BF16), wout_ref[...])
    x1 = x_ref[...] + g1_ref[0] * _rms(ym, gpost_ref[...])
    x1_ref[...] = x1
    h2 = _rms(x1, gpre_ref[...]) * (1.0 + sc2_ref[0]) + sh2_ref[0]
    h2_ref[...] = h2.astype(BF16)
    logits = _dot_hi(h2, wr_ref[...])
    lane = lax.broadcasted_iota(I32, logits.shape, 1)
    logits = jnp.where(lane < N_EXPERTS, logits, -1e30)
    p = jnp.exp(logits - jnp.max(logits, axis=-1, keepdims=True))
    aff_ref[...] = p / jnp.sum(p, axis=-1, keepdims=True)


def _mixout(yf, yb, xa, z, o_attn, gates, x, mod3, dskip, wn, woa, wos, wout, gpost, gpre, wr, row0, rows_per_cond):
    t = x.shape[0]
    tm = 256
    tok = lambda w: pl.BlockSpec((tm, w), lambda i: (i, 0))
    const = lambda a: pl.BlockSpec(a.shape, lambda i: (0,) * a.ndim)
    return pl.pallas_call(
        _mixout_kernel,
        grid=(t // tm,),
        in_specs=[tok(S_INNER), tok(S_INNER), tok(S_INNER), tok(S_INNER), tok(D_MODEL),
                  pl.BlockSpec((tm, D_MODEL), lambda i: (i, 0)),
                  pl.BlockSpec((tm, D_MODEL), lambda i: (i, 1)),
                  tok(D_MODEL),
                  _mod_spec(2, tm, row0, rows_per_cond),
                  _mod_spec(3, tm, row0, rows_per_cond),
                  _mod_spec(4, tm, row0, rows_per_cond),
                  const(dskip), const(wn), const(woa), const(wos), const(wout), const(gpost), const(gpre),
                  const(wr)],
        out_specs=[tok(D_MODEL), tok(D_MODEL), tok(LANES)],
        out_shape=[jax.ShapeDtypeStruct((t, D_MODEL), F32), jax.ShapeDtypeStruct((t, D_MODEL), BF16),
                   jax.ShapeDtypeStruct((t, LANES), F32)],
        compiler_params=_cparams(("parallel",), VMEM_LIMIT),
        name="mixout",
    )(yf, yb, xa, z, o_attn, gates, gates, x, mod3, mod3, mod3, dskip, wn, woa, wos, wout, gpost, gpre, wr)


SLOT_BLK = 128
TOK_CHUNK = 256


COMB_TILE = 256
COMB_WIN = 64


def _router_kernel(aff_ref, pos_ref, rng_ref, cnt_ref, *, cap):
    aff = aff_ref[...]
    e, t = aff.shape
    idx = lax.broadcasted_iota(I32, (e, t), 1)

    def count(m):
        return jnp.sum(jnp.where(m, 1.0, 0.0), axis=-1, keepdims=True)

    def value_bit(i, thr):
        cand = thr | jnp.left_shift(jnp.int32(1), 30 - i)
        return jnp.where(count(aff >= pltpu.bitcast(cand, F32)) >= cap, cand, thr)

    thr = pltpu.bitcast(lax.fori_loop(0, 31, value_bit, jnp.zeros((e, 1), I32)), F32)
    gt = aff > thr
    eq = aff == thr
    need = cap - count(gt)
    nbits = (t - 1).bit_length()

    def index_bit(i, v):
        cand = v | jnp.left_shift(jnp.int32(1), nbits - 1 - i)
        return jnp.where(count(jnp.logical_and(eq, idx < cand)) < need, cand, v)

    last = lax.fori_loop(0, nbits, index_bit, jnp.zeros((e, 1), I32))
    sel = jnp.logical_or(gt, jnp.logical_and(eq, idx <= last))
    sel_bf = jnp.where(sel, 1.0, 0.0).astype(BF16)
    blk = 512
    r = lax.broadcasted_iota(I32, (blk, blk), 0)
    c = lax.broadcasted_iota(I32, (blk, blk), 1)
    upper = jnp.where(r <= c, 1.0, 0.0).astype(BF16)
    carry = jnp.zeros((e, 1), F32)
    nsb = cap // SLOT_BLK
    first_tok = [jnp.zeros((e, 1), F32)] * nsb
    last_tok = [jnp.zeros((e, 1), F32)] * nsb
    tile_cnt = []
    for b in range(t // blk):
        cum = _dot(sel_bf[:, b * blk:(b + 1) * blk], upper) + carry
        carry = cum[:, blk - 1:blk]
        slot = jnp.where(sel[:, b * blk:(b + 1) * blk], cum - 1.0, -1.0)
        pos_ref[:, b * blk:(b + 1) * blk] = slot.astype(I32)
        for sb in range(nsb):
            first_tok[sb] = first_tok[sb] + count(cum <= sb * SLOT_BLK)
            last_tok[sb] = last_tok[sb] + count(cum <= sb * SLOT_BLK + SLOT_BLK - 1)
        for j in range(blk // COMB_TILE):
            tile_cnt.append(cum[:, (j + 1) * COMB_TILE - 1:(j + 1) * COMB_TILE])
    lane = lax.broadcasted_iota(I32, (e, LANES), 1)
    cnt = jnp.zeros((e, LANES), F32)
    for i, v in enumerate(tile_cnt):
        cnt = jnp.where(lane == i, v, cnt)
    cnt_ref[...] = cnt.astype(I32)
    rng = jnp.zeros((e, LANES), F32)
    for sb in range(nsb):
        rng = jnp.where(lane == sb, jnp.floor(first_tok[sb] * (1.0 / TOK_CHUNK)), rng)
        rng = jnp.where(lane == LANES // 2 + sb,
                        jnp.floor(jnp.minimum(last_tok[sb], t - 1.0) * (1.0 / TOK_CHUNK)), rng)
    rng_ref[...] = rng.astype(I32)


def _router(aff_t, cap):
    e, t = aff_t.shape
    nsb = cap // SLOT_BLK
    ntile = t // COMB_TILE
    pos, rng, cnt = pl.pallas_call(
        functools.partial(_router_kernel, cap=cap),
        out_shape=[jax.ShapeDtypeStruct((e, t), I32), jax.ShapeDtypeStruct((e, LANES), I32),
                   jax.ShapeDtypeStruct((e, LANES), I32)],
        name="router",
    )(aff_t)
    first = rng[:, :nsb].reshape(e * nsb)
    last = rng[:, LANES // 2:LANES // 2 + nsb].reshape(e * nsb)
    end = cnt[:, :ntile]
    start = jnp.concatenate([jnp.zeros((e, 1), I32), end[:, :-1]], axis=1)
    win = jnp.minimum(start // 16 * 16, cap - COMB_WIN)
    fits = jnp.all(end - win <= COMB_WIN, axis=0).astype(I32)
    return pos, first, last, win.T.reshape(ntile * e), fits


def _dispatch_kernel(first_ref, last_ref, pos_ref, h_ref, o_ref, acc_ref, *, cap):
    e = pl.program_id(0)
    nsb = cap // SLOT_BLK
    for sb in range(nsb):
        slot = sb * SLOT_BLK + lax.broadcasted_iota(I32, (SLOT_BLK, TOK_CHUNK), 0)
        acc_ref[...] = jnp.zeros(acc_ref.shape, F32)

        def chunk(kc, carry):
            t0 = pl.multiple_of(kc * TOK_CHUNK, TOK_CHUNK)
            onehot = jnp.where(pos_ref[0, :, pl.ds(t0, TOK_CHUNK)] == slot, 1.0, 0.0).astype(BF16)
            acc_ref[...] += _dot(onehot, h_ref[pl.ds(t0, TOK_CHUNK), :])
            return carry

        lax.fori_loop(first_ref[e * nsb + sb], last_ref[e * nsb + sb] + 1, chunk, 0)
        o_ref[0, sb * SLOT_BLK:(sb + 1) * SLOT_BLK, :] = acc_ref[...].astype(BF16)


def _dispatch(pos, first, last, h2, cap):
    e, t = pos.shape
    return pl.pallas_call(
        functools.partial(_dispatch_kernel, cap=cap),
        grid_spec=pltpu.PrefetchScalarGridSpec(
            num_scalar_prefetch=2,
            grid=(e,),
            in_specs=[pl.BlockSpec((1, 1, t), lambda i, first, last: (i, 0, 0)),
                      pl.BlockSpec((t, D_MODEL), lambda i, first, last: (0, 0))],
            out_specs=pl.BlockSpec((1, cap, D_MODEL), lambda i, first, last: (i, 0, 0)),
            scratch_shapes=[pltpu.VMEM((SLOT_BLK, D_MODEL), F32)]),
        out_shape=jax.ShapeDtypeStruct((e, cap, D_MODEL), BF16),
        compiler_params=_cparams(("parallel",), VMEM_LIMIT),
        name="dispatch",
    )(first, last, pos.reshape(e, 1, t), h2)


def _ffn_kernel(xc_ref, xl_ref, wg_ref, wu_ref, wd_ref, oc_ref, ol_ref, acc_ref):
    f = pl.program_id(1)
    cc = xc_ref.shape[1]
    xe = jnp.concatenate([xc_ref[0], xl_ref[0]], axis=0)
    hg = _dot(xe, wg_ref[0].astype(BF16))
    hu = _dot(xe, wu_ref[0].astype(BF16))
    part = _dot((_silu(hg) * hu).astype(BF16), wd_ref[0].astype(BF16))

    @pl.when(f == 0)
    def _():
        acc_ref[...] = part

    @pl.when(f > 0)
    def _():
        acc_ref[...] += part

    @pl.when(f == pl.num_programs(1) - 1)
    def _():
        oc_ref[0] = acc_ref[:cc, :].astype(BF16)
        ol_ref[0] = acc_ref[cc:, :].astype(BF16)


def _ffn(xe_c, xe_l, w_gate, w_up, w_down):
    e, cc, d = xe_c.shape
    cl = xe_l.shape[1]
    tf = 512
    return pl.pallas_call(
        _ffn_kernel,
        grid=(e, D_FF // tf),
        in_specs=[pl.BlockSpec((1, cc, d), lambda i, f: (i, 0, 0)),
                  pl.BlockSpec((1, cl, d), lambda i, f: (i, 0, 0)),
                  pl.BlockSpec((1, d, tf), lambda i, f: (i, 0, f)),
                  pl.BlockSpec((1, d, tf), lambda i, f: (i, 0, f)),
                  pl.BlockSpec((1, tf, d), lambda i, f: (i, f, 0))],
        out_specs=[pl.BlockSpec((1, cc, d), lambda i, f: (i, 0, 0)),
                   pl.BlockSpec((1, cl, d), lambda i, f: (i, 0, 0))],
        out_shape=[jax.ShapeDtypeStruct((e, cc, d), BF16), jax.ShapeDtypeStruct((e, cl, d), BF16)],
        scratch_shapes=[pltpu.VMEM((cc + cl, d), F32)],
        compiler_params=_cparams(("parallel", "arbitrary"), VMEM_LIMIT),
        name="ffn",
    )(xe_c, xe_l, w_gate, w_up, w_down)


def _combine_kernel(win_ref, fits_ref, pos_ref, aff_ref, ye_ref, x1_ref, g2_ref, gp_ref, o_ref, rhs_ref, y_ref):
    i = pl.program_id(0)
    tm = pos_ref.shape[0]
    cap = ye_ref.shape[1]
    pos = pos_ref[...]
    aff = aff_ref[...]

    @pl.when(fits_ref[i] == 1)
    def _():
        per_col = LANES // COMB_WIN
        lane = lax.broadcasted_iota(I32, (tm, LANES), 1)
        cols = []
        for c in range(N_EXPERTS // per_col):
            col = jnp.zeros((tm, LANES), F32)
            for j in range(per_col):
                e = c * per_col + j
                w0 = pl.multiple_of(win_ref[i * N_EXPERTS + e], 16)
                rhs_ref[e * COMB_WIN:(e + 1) * COMB_WIN, :] = ye_ref[e, pl.ds(w0, COMB_WIN), :]
                p = pos[:, e:e + 1]
                tgt = jnp.where(p >= 0, p - w0 + j * COMB_WIN, -1)
                col = jnp.where(lane == tgt, aff[:, e:e + 1], col)
            cols.append(col)
        sel = jnp.concatenate(cols, axis=1)
        hi = sel.astype(BF16)
        lo = (sel - hi.astype(F32)).astype(BF16)
        rhs = rhs_ref[...]
        y_ref[...] = _dot(hi, rhs) + _dot(lo, rhs)

    @pl.when(fits_ref[i] == 0)
    def _():
        slot = lax.broadcasted_iota(I32, (tm, cap), 1)
        y = jnp.zeros((tm, D_MODEL), F32)
        for e in range(N_EXPERTS):
            onehot = jnp.where(pos[:, e:e + 1] == slot, 1.0, 0.0).astype(BF16)
            y = y + aff[:, e:e + 1] * _dot(onehot, ye_ref[e])
        y_ref[...] = y

    o_ref[...] = x1_ref[...] + g2_ref[0] * _rms(y_ref[...], gp_ref[...])


def _combine(pos_t, win, fits, aff, ye, x1, mod3, gpost, row0, rows_per_cond):
    t = x1.shape[0]
    e, cap, d = ye.shape
    tm = COMB_TILE
    return pl.pallas_call(
        _combine_kernel,
        grid_spec=pltpu.PrefetchScalarGridSpec(
            num_scalar_prefetch=2,
            grid=(t // tm,),
            in_specs=[pl.BlockSpec((tm, e), lambda i, *_: (i, 0)),
                      pl.BlockSpec((tm, LANES), lambda i, *_: (i, 0)),
                      pl.BlockSpec((e, cap, d), lambda i, *_: (0, 0, 0)),
                      pl.BlockSpec((tm, d), lambda i, *_: (i, 0)),
                      _mod_spec(5, tm, row0, rows_per_cond),
                      pl.BlockSpec((1, d), lambda i, *_: (0, 0))],
            out_specs=pl.BlockSpec((tm, d), lambda i, *_: (i, 0)),
            scratch_shapes=[pltpu.VMEM((e * COMB_WIN, d), BF16), pltpu.VMEM((tm, d), F32)]),
        out_shape=jax.ShapeDtypeStruct((t, d), F32),
        compiler_params=_cparams(("parallel",), VMEM_LIMIT),
        name="combine",
    )(win, fits, pos_t, aff, ye, x1, mod3, gpost.reshape(1, d))


def _head_rows(v):
    flat = jnp.concatenate([v.reshape(2 * S_HEADS), jnp.zeros((LANES - 2 * S_HEADS,), F32)])
    return flat.reshape(1, LANES), flat.reshape(LANES, 1)


def kernel(x_prompt, x_sample, cache_k, cache_v, state_ssd, c, c_ctx, w_ada, b_ada, g_pre_mix, g_post_mix, g_pre_ffn, g_post_ffn, w_in, lam_q1, lam_k1, lam_q2, lam_k2, w_subln, conv_w, conv_b, dt_bias, a_log, d_skip, w_ssd_norm, w_o_attn, w_o_ssd, w_out, w_router, w_gate, w_up, w_down):
    nb_c, n_c, d = x_prompt.shape
    nb_l, n_l, _ = x_sample.shape
    npast = cache_k.shape[2]
    depth = w_in.shape[0]
    assert depth == 1 and d == D_MODEL
    t_c, t_l = nb_c * n_c, nb_l * n_l
    lyr = 0
    lam_init = 0.8 - 0.6 * math.exp(-0.3 * lyr)

    cond8 = jnp.concatenate([c_ctx[None, :], c, jnp.zeros((8 - 1 - nb_l, d), F32)], axis=0)
    mod3 = _ada(cond8, w_ada[lyr], b_ada[lyr]).reshape(8, 1, N_MOD * d)

    q_off, k_off, v_off, z_off, xbc_off = 0, 1024, 2048, 3072, 5120
    dt_off = xbc_off + S_CONV_CH
    g_off = dt_off + 2 * S_HEADS
    w_in_t = w_in[lyr].T
    w_gates_t = w_in_t[g_off:]
    w_dt_t = jnp.concatenate([w_in_t[dt_off:g_off], jnp.zeros((LANES - 2 * S_HEADS, d), F32)], axis=0).astype(BF16)
    w_dt = w_dt_t.T
    lam_p = jnp.stack([lam_q1[lyr], lam_k1[lyr], lam_q2[lyr], lam_k2[lyr]], axis=0)
    wsub = w_subln[lyr].reshape(1, A_V_DIM)
    dtb_row, dtb_col = _head_rows(dt_bias[lyr])
    alog_row, alog_col = _head_rows(a_log[lyr])
    dskip = jnp.repeat(d_skip[lyr], S_HEADDIM).reshape(1, S_INNER)
    wn = w_ssd_norm[lyr].reshape(1, S_INNER)
    woa = w_o_attn[lyr].astype(BF16)
    wos = w_o_ssd[lyr].astype(BF16)
    wout = w_out[lyr].astype(BF16)
    wr = jnp.concatenate([w_router[lyr], jnp.zeros((d, LANES - N_EXPERTS), F32)], axis=1)
    gpost = g_post_mix[lyr].reshape(1, d)
    gpre = g_pre_ffn[lyr].reshape(1, d)
    cos, sin = _rope_tables(n_l)

    def mixer(x, row0, rows_per_cond, nbatch, n, ctx):
        h = _prenorm(x, mod3, g_pre_mix[lyr], row0, rows_per_cond)
        q = _matmul(h, w_in_t, q_off, 1024, BF16)
        k = _matmul(h, w_in_t, k_off, 1024, F32)
        v = _matmul(h, w_in_t, v_off, 1024, F32)
        z = _matmul(h, w_in_t, z_off, S_INNER, BF16)
        xbc = _matmul(h, w_in_t, xbc_off, S_CONV_CH, BF16)
        gates = _matmul(h, w_gates_t, 0, 2 * d, BF16)
        dt, dt_t = _dtproj(h, w_dt, w_dt_t)
        if ctx is None:
            o_attn = _attn_ctx(lam_p, q, k, v, wsub, nbatch, n, lam_init)
            h0 = None
        else:
            kc, vc, h0 = ctx
            o_attn = _attn_lat(lam_p, q, k, v, kc, vc, cos, sin, wsub, nbatch, n, npast, lam_init)
        xa = _conv(xbc, conv_w[lyr], conv_b[lyr], nbatch, n)
        res = _ssd(xa, dt, dt_t, dtb_row, alog_row, dtb_col, alog_col, h0, nbatch, n, emit_final=ctx is None)
        yf, yb = res[0], res[1]
        x1, h2, aff = _mixout(yf, yb, xa, z, o_attn, gates, x, mod3, dskip, wn, woa, wos, wout, gpost, gpre, wr,
                              row0, rows_per_cond)
        fin = res[2] if ctx is None else None
        return x1, h2, aff, k, v, fin

    xc = x_prompt.reshape(t_c, d)
    xl = x_sample.reshape(t_l, d)
    x1_c, h2_c, aff_c, k_c, v_c, fin_c = mixer(xc, 0, t_c, nb_c, n_c, None)
    kc = cache_k[:, lyr].reshape(nb_l * npast, A_HEADS * 2 * A_HEAD_DIM)
    vc = cache_v[:, lyr].reshape(nb_l * npast, A_HEADS * A_V_DIM)
    h0 = state_ssd[:, lyr].reshape(nb_l, 2, S_INNER, S_STATE)
    x1_l, h2_l, aff_l, _, _, _ = mixer(xl, 1, n_l, nb_l, n_l, (kc, vc, h0))

    cap_c = EC_FACTOR * t_c // N_EXPERTS
    cap_l = EC_FACTOR * t_l // N_EXPERTS
    pos_c, first_c, last_c, win_c, fits_c = _router(aff_c[:, :N_EXPERTS].T, cap_c)
    pos_l, first_l, last_l, win_l, fits_l = _router(aff_l[:, :N_EXPERTS].T, cap_l)
    xe_c = _dispatch(pos_c, first_c, last_c, h2_c, cap_c)
    xe_l = _dispatch(pos_l, first_l, last_l, h2_l, cap_l)
    ye_c, ye_l = _ffn(xe_c, xe_l, w_gate[lyr], w_up[lyr], w_down[lyr])
    out_c = _combine(pos_c.T, win_c, fits_c, aff_c, ye_c, x1_c, mod3, g_post_ffn[lyr], 0, t_c)
    out_l = _combine(pos_l.T, win_l, fits_l, aff_l, ye_l, x1_l, mod3, g_post_ffn[lyr], 1, n_l)

    y_prompt = out_c.reshape(nb_c, n_c, d)
    y_sample = out_l.reshape(nb_l, n_l, d)
    new_k = k_c.reshape(nb_c, 1, n_c, A_HEADS, 2 * A_HEAD_DIM)
    new_v = v_c.reshape(nb_c, 1, n_c, A_HEADS, A_V_DIM)
    new_state = fin_c.reshape(nb_c, 1, 2, S_HEADS, S_HEADDIM, S_STATE)
    return (y_prompt, y_sample, new_k, new_v, new_state)
```

```python
import functools
import math
from typing import NamedTuple

import jax
import jax.numpy as jnp
import numpy as np
from jax import lax
from jax.experimental import pallas as pl
from jax.experimental.pallas import tpu as pltpu

F32 = jnp.float32
BF16 = jnp.bfloat16
I32 = jnp.int32

D_MODEL = 1024
GRID_W = 64
EPS = 1e-6
N_MOD = 6
A_HEADS = 8
A_HEAD_DIM = 64
A_V_DIM = 128
ROPE_BASE = 10000.0
S_INNER = 2048
S_HEADDIM = 64
S_HEADS = 32
S_GROUPS = 4
S_GROUP_W = S_INNER // S_GROUPS
S_STATE = 128
S_CHUNK = 128
S_CONV_CH = S_INNER + 2 * S_GROUPS * S_STATE
N_EXPERTS = 16
EC_FACTOR = 2
D_FF = 2048
LANES = 128
IN_Q, IN_K, IN_V, IN_Z, IN_XBC = 0, 1024, 2048, 3072, 5120
IN_DT = IN_XBC + S_CONV_CH
IN_GATES = IN_DT + 2 * S_HEADS
PROJ_Z, PROJ_XBC, PROJ_GATES, PROJ_Q = 0, 2048, 5120, 7168
VMEM_LIMIT = 56 * 1024 * 1024
LOG2E = 1.4426950408889634


def _cparams(sem, vmem=None):
    return pltpu.CompilerParams(dimension_semantics=sem, vmem_limit_bytes=vmem)


def _dot(a, b):
    return jnp.dot(a, b, preferred_element_type=F32)


def _dot_nt(a, b):
    return lax.dot_general(a, b, (((1,), (1,)), ((), ())), preferred_element_type=F32)


def _split3(a):
    a1 = a.astype(BF16)
    r1 = a - a1.astype(F32)
    a2 = r1.astype(BF16)
    a3 = (r1 - a2.astype(F32)).astype(BF16)
    return a1, a2, a3


def _dot_f32_lhs(a, m):
    a1, a2, a3 = _split3(a)
    return _dot(a1, m) + (_dot(a2, m) + _dot(a3, m))


def _dot_f32_rhs(m, b):
    b1, b2, b3 = _split3(b)
    return _dot(m, b1) + (_dot(m, b2) + _dot(m, b3))


def _dot_hi(a, b):
    a1, a2, _ = _split3(a)
    b1, b2, _ = _split3(b)
    return _dot(a1, b1) + (_dot(a1, b2) + _dot(a2, b1))


def _silu(x):
    return x * jax.nn.sigmoid(x)


def _rms(x, w):
    return x * lax.rsqrt(jnp.mean(x * x, axis=-1, keepdims=True) + EPS) * w


def _ada_kernel(c_ref, w_ref, b_ref, o_ref):
    o_ref[...] = _dot_hi(_silu(c_ref[...]), w_ref[...]) + b_ref[...]


def _ada(cond8, w_ada, b_ada):
    n = w_ada.shape[1]
    tn = 1536
    return pl.pallas_call(
        _ada_kernel,
        grid=(n // tn,),
        in_specs=[pl.BlockSpec((8, D_MODEL), lambda j: (0, 0)),
                  pl.BlockSpec((D_MODEL, tn), lambda j: (0, j)),
                  pl.BlockSpec((1, tn), lambda j: (0, j))],
        out_specs=pl.BlockSpec((8, tn), lambda j: (0, j)),
        out_shape=jax.ShapeDtypeStruct((8, n), F32),
        compiler_params=_cparams(("arbitrary",), VMEM_LIMIT),
        name="ada",
    )(cond8, w_ada, b_ada.reshape(1, n))


def _mod_spec(col, tm, row0, rows_per_cond):
    return pl.BlockSpec((1, 1, D_MODEL), lambda i, *_: (row0 + (i * tm) // rows_per_cond, 0, col))


def _prenorm_kernel(x_ref, sh_ref, sc_ref, g_ref, o_ref):
    xn = _rms(x_ref[...], g_ref[...])
    o_ref[...] = (xn * (1.0 + sc_ref[0]) + sh_ref[0]).astype(o_ref.dtype)


def _prenorm(x, mod3, gain, row0, rows_per_cond):
    t = x.shape[0]
    tm = 512
    return pl.pallas_call(
        _prenorm_kernel,
        grid=(t // tm,),
        in_specs=[pl.BlockSpec((tm, D_MODEL), lambda i: (i, 0)),
                  _mod_spec(0, tm, row0, rows_per_cond),
                  _mod_spec(1, tm, row0, rows_per_cond),
                  pl.BlockSpec((1, D_MODEL), lambda i: (0, 0))],
        out_specs=pl.BlockSpec((tm, D_MODEL), lambda i: (i, 0)),
        out_shape=jax.ShapeDtypeStruct((t, D_MODEL), BF16),
        compiler_params=_cparams(("parallel",)),
        name="prenorm",
    )(x, mod3, mod3, gain.reshape(1, D_MODEL))


INPROJ_TM = 2048
INPROJ_TN = 512


def _inproj_kernel(blk_ref, hc_ref, hl_ref, w_ref, walt_ref, oc_ref, ol_ref, wbf_ref, *, alt_lo, alt_hi, n_ctx_tiles):
    del blk_ref
    j = pl.program_id(0)
    i = pl.program_id(1)
    use_alt = jnp.logical_and(j >= alt_lo, j < alt_hi)

    @pl.when(jnp.logical_and(i == 0, use_alt))
    def _():
        wbf_ref[...] = walt_ref[...].astype(BF16)

    @pl.when(jnp.logical_and(i == 0, jnp.logical_not(use_alt)))
    def _():
        wbf_ref[...] = w_ref[...].astype(BF16)

    @pl.when(i < n_ctx_tiles)
    def _():
        oc_ref[...] = _dot_nt(hc_ref[...], wbf_ref[...]).astype(oc_ref.dtype)

    @pl.when(i >= n_ctx_tiles)
    def _():
        ol_ref[...] = _dot_nt(hl_ref[...], wbf_ref[...]).astype(ol_ref.dtype)


def _inproj(h_c, h_l, w_t, w_blocks, out_dtype, w_alt_t=None, alt_range=(0, 0)):
    tm, tn = INPROJ_TM, INPROJ_TN
    t_c, k = h_c.shape
    t_l = h_l.shape[0]
    assert t_c % tm == 0 and t_l == tm
    nct = t_c // tm
    ncol = len(w_blocks)
    alt_lo, alt_hi = alt_range
    if w_alt_t is None:
        w_alt_t = w_t
    blocks = jnp.asarray(w_blocks, I32)
    n_alt_blocks = max(alt_hi - alt_lo, 1)

    def w_idx(j, i, blk):
        return (blk[j], 0)

    def alt_idx(j, i, blk):
        return (jnp.clip(j - alt_lo, 0, n_alt_blocks - 1), 0)

    return pl.pallas_call(
        functools.partial(_inproj_kernel, alt_lo=alt_lo, alt_hi=alt_hi, n_ctx_tiles=nct),
        grid_spec=pltpu.PrefetchScalarGridSpec(
            num_scalar_prefetch=1,
            grid=(ncol, nct + 1),
            in_specs=[pl.BlockSpec((tm, k), lambda j, i, blk: (jnp.minimum(i, nct - 1), 0)),
                      pl.BlockSpec((tm, k), lambda j, i, blk: (0, 0)),
                      pl.BlockSpec((tn, k), w_idx),
                      pl.BlockSpec((tn, k), alt_idx)],
            out_specs=[pl.BlockSpec((tm, tn), lambda j, i, blk: (jnp.minimum(i, nct - 1), j)),
                       pl.BlockSpec((tm, tn), lambda j, i, blk: (0, j))],
            scratch_shapes=[pltpu.VMEM((tn, k), BF16)]),
        out_shape=[jax.ShapeDtypeStruct((t_c, ncol * tn), out_dtype),
                   jax.ShapeDtypeStruct((t_l, ncol * tn), out_dtype)],
        compiler_params=_cparams(("arbitrary", "arbitrary"), VMEM_LIMIT),
        name="inproj",
    )(blocks, h_c, h_l, w_t, w_alt_t)


def _dtproj_kernel(h_ref, w_ref, wt_ref, o_ref, ot_ref):
    h = h_ref[...]
    o_ref[...] = _dot(h, w_ref[...])
    ot_ref[...] = _dot_nt(wt_ref[...], h)


def _dtproj(h, w_dt, w_dt_t):
    t, k = h.shape
    tm = 1024
    return pl.pallas_call(
        _dtproj_kernel,
        grid=(t // tm,),
        in_specs=[pl.BlockSpec((tm, k), lambda i: (i, 0)),
                  pl.BlockSpec((k, LANES), lambda i: (0, 0)),
                  pl.BlockSpec((LANES, k), lambda i: (0, 0))],
        out_specs=[pl.BlockSpec((tm, LANES), lambda i: (i, 0)),
                   pl.BlockSpec((LANES, tm), lambda i: (0, i))],
        out_shape=[jax.ShapeDtypeStruct((t, LANES), F32), jax.ShapeDtypeStruct((LANES, t), F32)],
        compiler_params=_cparams(("parallel",)),
        name="dtproj",
    )(h, w_dt, w_dt_t)


def _lam(lp_ref, lam_init):
    lp = lp_ref[...]
    s1 = jnp.sum(lp[0:1] * lp[1:2], axis=-1, keepdims=True)
    s2 = jnp.sum(lp[2:3] * lp[3:4], axis=-1, keepdims=True)
    return jnp.exp(s1) - jnp.exp(s2) + lam_init


def _diff_attn(q, k, v, lam, wsub, lam_init):
    nq = q.shape[0]
    lane = lax.broadcasted_iota(I32, q.shape, 1)
    q = q * (A_HEAD_DIM ** -0.5 * LOG2E)
    q1 = jnp.where(lane < A_HEAD_DIM, q, 0.0).astype(BF16)
    q2 = jnp.where(lane >= A_HEAD_DIM, q, 0.0).astype(BF16)
    s = _dot_nt(jnp.concatenate([q1, q2], axis=0), k)
    p = jnp.exp2(s - jnp.max(s, axis=-1, keepdims=True))
    r = 1.0 / jnp.sum(p, axis=-1, keepdims=True)
    pv = _dot(p.astype(BF16), v) * r
    o = pv[:nq] - lam * pv[nq:]
    return _rms(o, wsub) * (1.0 - lam_init)


def _attn_ctx_kernel(lp_ref, q_ref, k_ref, v_ref, ws_ref, o_ref, *, lam_init):
    lam = _lam(lp_ref, lam_init)
    for h in range(A_HEADS):
        sl = slice(h * LANES, (h + 1) * LANES)
        o = _diff_attn(q_ref[:, sl].astype(F32), k_ref[:, sl].astype(BF16), v_ref[:, sl].astype(BF16), lam,
                       ws_ref[...], lam_init)
        o_ref[:, sl] = o.astype(o_ref.dtype)


def _attn_ctx(lam_p, proj, k, v, wsub, nbatch, n, lam_init):
    w = A_HEADS * LANES
    blk = pl.BlockSpec((n, w), lambda b: (b, 0))
    return pl.pallas_call(
        functools.partial(_attn_ctx_kernel, lam_init=lam_init),
        grid=(nbatch,),
        in_specs=[pl.BlockSpec((4, A_HEAD_DIM), lambda b: (0, 0)),
                  pl.BlockSpec((n, w), lambda b: (b, PROJ_Q // w)), blk, blk,
                  pl.BlockSpec((1, A_V_DIM), lambda b: (0, 0))],
        out_specs=blk,
        out_shape=jax.ShapeDtypeStruct(k.shape, BF16),
        compiler_params=_cparams(("parallel",), VMEM_LIMIT),
        name="attn_ctx",
    )(lam_p, proj, k, v, wsub)


def _rope(x, cos, sin_signed):
    lane = lax.broadcasted_iota(I32, x.shape, 1)
    quarter = A_HEAD_DIM // 4
    partner = jnp.where((lane % (2 * quarter)) < quarter,
                        pltpu.roll(x, LANES - quarter, axis=1),
                        pltpu.roll(x, quarter, axis=1))
    return x * cos + partner * sin_signed


def _attn_lat_kernel(lp_ref, q_ref, k_ref, v_ref, kc_ref, vc_ref, cos_ref, sin_ref, ws_ref, o_ref, *, lam_init, tq):
    lam = _lam(lp_ref, lam_init)
    k = _rope(k_ref[...], cos_ref[...], sin_ref[...])
    k_all = jnp.concatenate([kc_ref[...].astype(BF16), k.astype(BF16)], axis=0)
    v_all = jnp.concatenate([vc_ref[...].astype(BF16), v_ref[...].astype(BF16)], axis=0)
    for i in range(q_ref.shape[0] // tq):
        rows = slice(i * tq, (i + 1) * tq)
        q = _rope(q_ref[rows, :].astype(F32), cos_ref[rows, :], sin_ref[rows, :])
        o_ref[rows, :] = _diff_attn(q, k_all, v_all, lam, ws_ref[...], lam_init).astype(o_ref.dtype)


def _attn_lat(lam_p, proj, k, v, kc, vc, cos, sin, wsub, nbatch, n, npast, lam_init):
    blk = pl.BlockSpec((n, LANES), lambda b, h: (b, h))
    cblk = pl.BlockSpec((npast, LANES), lambda b, h: (b, h))
    tab = pl.BlockSpec((n, LANES), lambda b, h: (0, 0))
    return pl.pallas_call(
        functools.partial(_attn_lat_kernel, lam_init=lam_init, tq=256),
        grid=(nbatch, A_HEADS),
        in_specs=[pl.BlockSpec((4, A_HEAD_DIM), lambda b, h: (0, 0)),
                  pl.BlockSpec((n, LANES), lambda b, h: (b, PROJ_Q // LANES + h)), blk, blk, cblk, cblk, tab, tab,
                  pl.BlockSpec((1, A_V_DIM), lambda b, h: (0, 0))],
        out_specs=blk,
        out_shape=jax.ShapeDtypeStruct(k.shape, BF16),
        compiler_params=_cparams(("parallel", "parallel"), VMEM_LIMIT),
        name="attn_lat",
    )(lam_p, proj, k, v, kc, vc, cos, sin, wsub)


def _rope_tables(n):
    pos = np.arange(n)
    rowcol = np.stack([pos // GRID_W, pos % GRID_W], axis=-1).astype(np.float64)
    nf = A_HEAD_DIM // 4
    inv_freq = np.power(ROPE_BASE, -np.arange(nf, dtype=np.float64) / nf)
    lane = np.arange(LANES)
    axis = (lane % A_HEAD_DIM) // (2 * nf)
    ang = rowcol[:, axis] * inv_freq[lane % nf][None, :]
    sign = np.where((lane % (2 * nf)) < nf, -1.0, 1.0)
    return jnp.asarray(np.cos(ang), F32), jnp.asarray(np.sin(ang) * sign[None, :], F32)


def _conv_kernel(x_ref, w_ref, b_ref, o_ref):
    x = x_ref[...].astype(F32)
    n = x.shape[0]
    row = lax.broadcasted_iota(I32, x.shape, 0)
    prev = jnp.where(row == 0, 0.0, pltpu.roll(x, 1, axis=0))
    nxt = jnp.where(row == n - 1, 0.0, pltpu.roll(x, n - 1, axis=0))
    w = w_ref[...]
    o_ref[...] = _silu(prev * w[0:1] + x * w[1:2] + nxt * w[2:3] + b_ref[...]).astype(o_ref.dtype)


def _conv(proj, conv_w, conv_b, nbatch, n):
    tc = 1024
    c = S_CONV_CH
    return pl.pallas_call(
        _conv_kernel,
        grid=(nbatch, c // tc),
        in_specs=[pl.BlockSpec((n, tc), lambda b, j: (b, PROJ_XBC // tc + j)),
                  pl.BlockSpec((3, tc), lambda b, j: (0, j)),
                  pl.BlockSpec((1, tc), lambda b, j: (0, j))],
        out_specs=pl.BlockSpec((n, tc), lambda b, j: (b, j)),
        out_shape=jax.ShapeDtypeStruct((proj.shape[0], c), BF16),
        compiler_params=_cparams(("parallel", "parallel"), VMEM_LIMIT),
        name="conv",
    )(proj, conv_w, conv_b.reshape(1, c))


S_PAIRS = S_INNER // LANES


def _ssd_direction(xa_ref, dt_raw, dt_raw_t, dtb_row, alog_row, dtb_col, alog_col, ht_ref, y_ref, backward):
    col0 = S_HEADS if backward else 0
    row = lax.broadcasted_iota(I32, (S_CHUNK, S_CHUNK), 0)
    col = lax.broadcasted_iota(I32, (S_CHUNK, S_CHUNK), 1)
    upper = jnp.where(row <= col, 1.0, 0.0).astype(BF16)
    lower = jnp.where(row >= col, 1.0, 0.0).astype(BF16)
    dt = jax.nn.softplus(dt_raw + dtb_row)
    dt_t = jax.nn.softplus(dt_raw_t + dtb_col)
    da = dt * (-jnp.exp(alog_row) * LOG2E)
    da_t = dt_t * (-jnp.exp(alog_col) * LOG2E)
    if backward:
        cum = _dot_f32_rhs(upper, da)
        cum_t = _dot_f32_lhs(da_t, lower)
        last = 0
        keep = row <= col
    else:
        cum = _dot_f32_rhs(lower, da)
        cum_t = _dot_f32_lhs(da_t, upper)
        last = S_CHUNK - 1
        keep = row >= col
    w_t = dt_t * jnp.exp2(cum_t[:, last:last + 1] - cum_t)
    e_tot = jnp.exp2(cum[last:last + 1, :])
    lo_half = col < S_HEADDIM
    lo_bf = jnp.where(lo_half, 1.0, 0.0).astype(BF16)
    hi_bf = jnp.where(lo_half, 0.0, 1.0).astype(BF16)
    for g in range(S_GROUPS):
        b_bf = xa_ref[:, S_INNER + g * S_STATE:S_INNER + (g + 1) * S_STATE]
        c_bf = xa_ref[:, S_INNER + (S_GROUPS + g) * S_STATE:S_INNER + (S_GROUPS + g + 1) * S_STATE]
        cb = _dot_nt(c_bf, b_bf)
        bt = b_bf.astype(F32).T
        c_f = c_bf.astype(F32)
        for pr in range(S_PAIRS // S_GROUPS):
            hp = g * (S_PAIRS // S_GROUPS) + pr
            c0 = col0 + 2 * hp
            x_pair = xa_ref[:, hp * LANES:(hp + 1) * LANES]
            bd_x = jnp.concatenate([x_pair * lo_bf, x_pair * hi_bf], axis=0)
            ht = ht_ref[hp]
            bd_h = jnp.concatenate([jnp.where(lo_half, ht, 0.0), jnp.where(lo_half, 0.0, ht)], axis=0).astype(BF16)
            ms, ces, bws = [], [], []
            for cc in (c0, c0 + 1):
                bc = jnp.broadcast_to(cum[:, cc:cc + 1], (S_CHUNK, S_CHUNK))
                decay = jnp.exp2(jnp.where(keep, bc - cum_t[cc:cc + 1, :], -1e30))
                ms.append((cb * decay * dt_t[cc:cc + 1, :]).astype(BF16))
                ces.append((c_f * jnp.exp2(bc)).astype(BF16))
                bws.append((bt * w_t[cc:cc + 1, :]).astype(BF16))
            y_ref[:, hp * LANES:(hp + 1) * LANES] = _dot(jnp.concatenate(ms + ces, axis=1),
                                                         jnp.concatenate([bd_x, bd_h], axis=0))
            tot = jnp.where(lo_half[0:1], e_tot[:, c0:c0 + 1], e_tot[:, c0 + 1:c0 + 2])
            ht_ref[hp] = ht * tot + _dot(jnp.concatenate(bws, axis=1), bd_x)


def _ssd_kernel(*refs, has_h0, emit_final):
    xf_ref, xb_ref, dtf_ref, dtb_ref, dtft_ref, dtbt_ref, br_ref, ar_ref, bc_ref, ac_ref = refs[:10]
    refs = refs[10:]
    if has_h0:
        h0_ref, refs = refs[0], refs[1:]
    yf_ref, yb_ref = refs[:2]
    refs = refs[2:]
    if emit_final:
        fin_ref, refs = refs[0], refs[1:]
    htf_ref, htb_ref = refs
    c = pl.program_id(1)
    nc = pl.num_programs(1)

    @pl.when(c == 0)
    def _():
        for d, ht_ref in enumerate((htf_ref, htb_ref)):
            for hp in range(S_PAIRS):
                if has_h0:
                    ht_ref[hp] = h0_ref[0, d, hp * LANES:(hp + 1) * LANES, :].T
                else:
                    ht_ref[hp] = jnp.zeros((S_STATE, LANES), F32)

    _ssd_direction(xf_ref, dtf_ref[...], dtft_ref[...], br_ref[...], ar_ref[...], bc_ref[...], ac_ref[...],
                   htf_ref, yf_ref, backward=False)
    _ssd_direction(xb_ref, dtb_ref[...], dtbt_ref[...], br_ref[...], ar_ref[...], bc_ref[...], ac_ref[...],
                   htb_ref, yb_ref, backward=True)

    if emit_final:
        @pl.when(c == nc - 1)
        def _():
            for d, ht_ref in enumerate((htf_ref, htb_ref)):
                for hp in range(S_PAIRS):
                    fin_ref[0, d, hp * LANES:(hp + 1) * LANES, :] = ht_ref[hp].T


def _ssd(xa, dt, dt_t, dtb_row, alog_row, dtb_col, alog_col, h0, nbatch, n, emit_final):
    nc = n // S_CHUNK
    t = xa.shape[0]
    fwd = lambda b, c: (b * nc + c, 0)
    bwd = lambda b, c: (b * nc + nc - 1 - c, 0)
    const = lambda b, c: (0, 0)
    in_specs = [pl.BlockSpec((S_CHUNK, S_CONV_CH), fwd),
                pl.BlockSpec((S_CHUNK, S_CONV_CH), bwd),
                pl.BlockSpec((S_CHUNK, LANES), fwd),
                pl.BlockSpec((S_CHUNK, LANES), bwd),
                pl.BlockSpec((LANES, S_CHUNK), lambda b, c: (0, b * nc + c)),
                pl.BlockSpec((LANES, S_CHUNK), lambda b, c: (0, b * nc + nc - 1 - c)),
                pl.BlockSpec((1, LANES), const), pl.BlockSpec((1, LANES), const),
                pl.BlockSpec((LANES, 1), const), pl.BlockSpec((LANES, 1), const)]
    args = [xa, xa, dt, dt, dt_t, dt_t, dtb_row, alog_row, dtb_col, alog_col]
    if h0 is not None:
        in_specs.append(pl.BlockSpec((1, 2, S_INNER, S_STATE), lambda b, c: (b, 0, 0, 0)))
        args.append(h0)
    out_specs = [pl.BlockSpec((S_CHUNK, S_INNER), fwd), pl.BlockSpec((S_CHUNK, S_INNER), bwd)]
    out_shape = [jax.ShapeDtypeStruct((t, S_INNER), F32), jax.ShapeDtypeStruct((t, S_INNER), F32)]
    if emit_final:
        out_specs.append(pl.BlockSpec((1, 2, S_INNER, S_STATE), lambda b, c: (b, 0, 0, 0)))
        out_shape.append(jax.ShapeDtypeStruct((nbatch, 2, S_INNER, S_STATE), F32))
    return pl.pallas_call(
        functools.partial(_ssd_kernel, has_h0=h0 is not None, emit_final=emit_final),
        grid=(nbatch, nc),
        in_specs=in_specs,
        out_specs=out_specs,
        out_shape=out_shape,
        scratch_shapes=[pltpu.VMEM((S_PAIRS, S_STATE, LANES), F32),
                        pltpu.VMEM((S_PAIRS, S_STATE, LANES), F32)],
        compiler_params=_cparams(("parallel", "arbitrary"), VMEM_LIMIT),
        name="ssd",
    )(*args)


def _mixout_kernel(yf_ref, yb_ref, xs_ref, z_ref, oa_ref, ga_ref, gs_ref, x_ref, g1_ref, sh2_ref, sc2_ref,
                   dskip_ref, wn_ref, woa_ref, wos_ref, wout_ref, gpost_ref, gpre_ref, wr_ref,
                   x1_ref, h2_ref, aff_ref):
    y = yf_ref[...] + yb_ref[...] + xs_ref[...].astype(F32) * dskip_ref[...]
    y = y * _silu(z_ref[...].astype(F32))
    wn = wn_ref[...]
    parts = []
    for g in range(S_GROUPS):
        sl = slice(g * S_GROUP_W, (g + 1) * S_GROUP_W)
        parts.append(_rms(y[:, sl], wn[:, sl]).astype(BF16))
    o_ssd = jnp.concatenate(parts, axis=1)
    a = _dot(oa_ref[...], woa_ref[...])
    s = _dot(o_ssd, wos_ref[...])
    merged = jax.nn.sigmoid(ga_ref[...].astype(F32)) * a + jax.nn.sigmoid(gs_ref[...].astype(F32)) * s
    ym = _dot(merged.astype(BF16), wout_ref[...])
    x1 = x_ref[...] + g1_ref[0] * _rms(ym, gpost_ref[...])
    x1_ref[...] = x1
    h2 = _rms(x1, gpre_ref[...]) * (1.0 + sc2_ref[0]) + sh2_ref[0]
    h2_ref[...] = h2.astype(BF16)
    logits = _dot_hi(h2, wr_ref[...])
    lane = lax.broadcasted_iota(I32, logits.shape, 1)
    logits = jnp.where(lane < N_EXPERTS, logits, -1e30)
    p = jnp.exp(logits - jnp.max(logits, axis=-1, keepdims=True))
    aff_ref[...] = p / jnp.sum(p, axis=-1, keepdims=True)


def _mixout(yf, yb, xa, proj, o_attn, x, mod3, dskip, wn, woa, wos, wout, gpost, gpre, wr, row0, rows_per_cond):
    t = x.shape[0]
    tm = 256
    tok = lambda w: pl.BlockSpec((tm, w), lambda i: (i, 0))
    const = lambda a: pl.BlockSpec(a.shape, lambda i: (0,) * a.ndim)
    return pl.pallas_call(
        _mixout_kernel,
        grid=(t // tm,),
        in_specs=[tok(S_INNER), tok(S_INNER), tok(S_INNER),
                  pl.BlockSpec((tm, S_INNER), lambda i: (i, PROJ_Z // S_INNER)), tok(D_MODEL),
                  pl.BlockSpec((tm, D_MODEL), lambda i: (i, PROJ_GATES // D_MODEL)),
                  pl.BlockSpec((tm, D_MODEL), lambda i: (i, PROJ_GATES // D_MODEL + 1)),
                  tok(D_MODEL),
                  _mod_spec(2, tm, row0, rows_per_cond),
                  _mod_spec(3, tm, row0, rows_per_cond),
                  _mod_spec(4, tm, row0, rows_per_cond),
                  const(dskip), const(wn), const(woa), const(wos), const(wout), const(gpost), const(gpre),
                  const(wr)],
        out_specs=[tok(D_MODEL), tok(D_MODEL), tok(LANES)],
        out_shape=[jax.ShapeDtypeStruct((t, D_MODEL), F32), jax.ShapeDtypeStruct((t, D_MODEL), BF16),
                   jax.ShapeDtypeStruct((t, LANES), F32)],
        compiler_params=_cparams(("parallel",), VMEM_LIMIT),
        name="mixout",
    )(yf, yb, xa, proj, o_attn, proj, proj, x, mod3, mod3, mod3, dskip, wn, woa, wos, wout, gpost, gpre, wr)


COMB_TILE = 256
COMB_WIN = 64
BF16_ROWS = 16


def _router_kernel(aff_ref, pos_ref, cnt_ref, *, cap):
    aff = aff_ref[...]
    e, t = aff.shape
    idx = lax.broadcasted_iota(I32, (e, t), 1)

    def count(m):
        return jnp.sum(jnp.where(m, 1.0, 0.0), axis=-1, keepdims=True)

    def value_bit(i, thr):
        cand = thr | jnp.left_shift(jnp.int32(1), 30 - i)
        return jnp.where(count(aff >= pltpu.bitcast(cand, F32)) >= cap, cand, thr)

    thr = pltpu.bitcast(lax.fori_loop(0, 31, value_bit, jnp.zeros((e, 1), I32)), F32)
    gt = aff > thr
    eq = aff == thr
    need = cap - count(gt)
    nbits = (t - 1).bit_length()

    def index_bit(i, v):
        cand = v | jnp.left_shift(jnp.int32(1), nbits - 1 - i)
        return jnp.where(count(jnp.logical_and(eq, idx < cand)) < need, cand, v)

    last = lax.fori_loop(0, nbits, index_bit, jnp.zeros((e, 1), I32))
    sel = jnp.logical_or(gt, jnp.logical_and(eq, idx <= last))
    sel_bf = jnp.where(sel, 1.0, 0.0).astype(BF16)
    blk = 512
    r = lax.broadcasted_iota(I32, (blk, blk), 0)
    c = lax.broadcasted_iota(I32, (blk, blk), 1)
    upper = jnp.where(r <= c, 1.0, 0.0).astype(BF16)
    carry = jnp.zeros((e, 1), F32)
    tile_cnt = []
    for b in range(t // blk):
        cum = _dot(sel_bf[:, b * blk:(b + 1) * blk], upper) + carry
        carry = cum[:, blk - 1:blk]
        slot = jnp.where(sel[:, b * blk:(b + 1) * blk], cum - 1.0, -1.0)
        pos_ref[:, b * blk:(b + 1) * blk] = slot.astype(I32)
        for j in range(blk // COMB_TILE):
            tile_cnt.append(cum[:, (j + 1) * COMB_TILE - 1:(j + 1) * COMB_TILE])
    lane = lax.broadcasted_iota(I32, (e, LANES), 1)
    cnt = jnp.zeros((e, LANES), F32)
    for i, v in enumerate(tile_cnt):
        cnt = jnp.where(lane == i, v, cnt)
    cnt_ref[...] = cnt.astype(I32)


class _Route(NamedTuple):
    pos: jax.Array
    start: jax.Array
    end: jax.Array
    win: jax.Array
    fits: jax.Array


def _router(aff_t, cap):
    e, t = aff_t.shape
    ntile = t // COMB_TILE
    pos, cnt = pl.pallas_call(
        functools.partial(_router_kernel, cap=cap),
        out_shape=[jax.ShapeDtypeStruct((e, t), I32), jax.ShapeDtypeStruct((e, LANES), I32)],
        name="router",
    )(aff_t)
    end = cnt[:, :ntile]
    start = jnp.concatenate([jnp.zeros((e, 1), I32), end[:, :-1]], axis=1)
    win = jnp.minimum(start // BF16_ROWS * BF16_ROWS, cap - COMB_WIN)
    fits = jnp.all(end - win <= COMB_WIN, axis=0).astype(I32)
    flat = lambda a: a.T.reshape(ntile * e)
    return _Route(pos, flat(start), flat(end), flat(win), fits)


def _dispatch_kernel(start_ref, end_ref, win_ref, fits_ref, pos_ref, h_ref, o_ref):
    i = pl.program_id(0)
    ne, cap, _ = o_ref.shape
    tm = h_ref.shape[0]

    @pl.when(i == 0)
    def _():
        o_ref[...] = jnp.zeros(o_ref.shape, o_ref.dtype)

    h = h_ref[...]
    row = lax.broadcasted_iota(I32, (COMB_WIN, tm), 0)
    row_l = lax.broadcasted_iota(I32, (COMB_WIN, LANES), 0)

    def place(e, r0, rows):
        rel = r0 - start_ref[i * ne + e] + row_l
        mine = jnp.logical_and(rel >= 0, rel < end_ref[i * ne + e] - start_ref[i * ne + e])
        for c in range(D_MODEL // LANES):
            sl = slice(c * LANES, (c + 1) * LANES)
            old = o_ref[e, pl.ds(r0, COMB_WIN), sl].astype(F32)
            o_ref[e, pl.ds(r0, COMB_WIN), sl] = jnp.where(mine, rows[:, sl], old).astype(o_ref.dtype)

    @pl.when(fits_ref[i] == 1)
    def _():
        sel = [jnp.where(pos_ref[e:e + 1, :] == win_ref[i * ne + e] + row, 1.0, 0.0).astype(BF16) for e in range(ne)]
        rows = _dot(jnp.concatenate(sel, axis=0), h)
        for e in range(ne):
            place(e, pl.multiple_of(win_ref[i * ne + e], BF16_ROWS), rows[e * COMB_WIN:(e + 1) * COMB_WIN])

    @pl.when(fits_ref[i] == 0)
    def _():
        for e in range(ne):
            w0 = win_ref[i * ne + e]

            def window(w, carry):
                r0 = pl.multiple_of(jnp.minimum(w0 + w * COMB_WIN, cap - COMB_WIN), BF16_ROWS)
                sel = jnp.where(pos_ref[e:e + 1, :] == r0 + row, 1.0, 0.0).astype(BF16)
                place(e, r0, _dot(sel, h))
                return carry

            lax.fori_loop(0, (end_ref[i * ne + e] - w0 + COMB_WIN - 1) // COMB_WIN, window, 0)


def _dispatch(route, h2, cap):
    e, t = route.pos.shape
    tm = COMB_TILE
    return pl.pallas_call(
        _dispatch_kernel,
        grid_spec=pltpu.PrefetchScalarGridSpec(
            num_scalar_prefetch=4,
            grid=(t // tm,),
            in_specs=[pl.BlockSpec((e, tm), lambda i, *_: (0, i)),
                      pl.BlockSpec((tm, D_MODEL), lambda i, *_: (i, 0))],
            out_specs=pl.BlockSpec((e, cap, D_MODEL), lambda i, *_: (0, 0, 0))),
        out_shape=jax.ShapeDtypeStruct((e, cap, D_MODEL), BF16),
        compiler_params=_cparams(("arbitrary",), VMEM_LIMIT),
        name="dispatch",
    )(route.start, route.end, route.win, route.fits, route.pos, h2)


def _ffn_kernel(xc_ref, xl_ref, wg_ref, wu_ref, wd_ref, oc_ref, ol_ref, acc_ref):
    f = pl.program_id(1)
    cc = xc_ref.shape[1]
    xe = jnp.concatenate([xc_ref[0], xl_ref[0]], axis=0)
    hg = _dot(xe, wg_ref[0].astype(BF16))
    hu = _dot(xe, wu_ref[0].astype(BF16))
    part = _dot((_silu(hg) * hu).astype(BF16), wd_ref[0].astype(BF16))

    @pl.when(f == 0)
    def _():
        acc_ref[...] = part

    @pl.when(f > 0)
    def _():
        acc_ref[...] += part

    @pl.when(f == pl.num_programs(1) - 1)
    def _():
        oc_ref[0] = acc_ref[:cc, :].astype(BF16)
        ol_ref[0] = acc_ref[cc:, :].astype(BF16)


def _ffn(xe_c, xe_l, w_gate, w_up, w_down):
    e, cc, d = xe_c.shape
    cl = xe_l.shape[1]
    tf = 1024
    return pl.pallas_call(
        _ffn_kernel,
        grid=(e, D_FF // tf),
        in_specs=[pl.BlockSpec((1, cc, d), lambda i, f: (i, 0, 0)),
                  pl.BlockSpec((1, cl, d), lambda i, f: (i, 0, 0)),
                  pl.BlockSpec((1, d, tf), lambda i, f: (i, 0, f)),
                  pl.BlockSpec((1, d, tf), lambda i, f: (i, 0, f)),
                  pl.BlockSpec((1, tf, d), lambda i, f: (i, f, 0))],
        out_specs=[pl.BlockSpec((1, cc, d), lambda i, f: (i, 0, 0)),
                   pl.BlockSpec((1, cl, d), lambda i, f: (i, 0, 0))],
        out_shape=[jax.ShapeDtypeStruct((e, cc, d), BF16), jax.ShapeDtypeStruct((e, cl, d), BF16)],
        scratch_shapes=[pltpu.VMEM((cc + cl, d), F32)],
        compiler_params=_cparams(("parallel", "arbitrary"), VMEM_LIMIT),
        name="ffn",
    )(xe_c, xe_l, w_gate, w_up, w_down)


def _combine_kernel(win_ref, fits_ref, pos_ref, aff_ref, ye_ref, x1_ref, g2_ref, gp_ref, o_ref, rhs_ref, y_ref):
    i = pl.program_id(0)
    tm = pos_ref.shape[0]
    cap = ye_ref.shape[1]
    pos = pos_ref[...]
    aff = aff_ref[...]

    @pl.when(fits_ref[i] == 1)
    def _():
        per_col = LANES // COMB_WIN
        lane = lax.broadcasted_iota(I32, (tm, LANES), 1)
        cols = []
        for c in range(N_EXPERTS // per_col):
            col = jnp.zeros((tm, LANES), F32)
            for j in range(per_col):
                e = c * per_col + j
                w0 = pl.multiple_of(win_ref[i * N_EXPERTS + e], BF16_ROWS)
                rhs_ref[e * COMB_WIN:(e + 1) * COMB_WIN, :] = ye_ref[e, pl.ds(w0, COMB_WIN), :]
                p = pos[:, e:e + 1]
                tgt = jnp.where(p >= 0, p - w0 + j * COMB_WIN, -1)
                col = jnp.where(lane == tgt, aff[:, e:e + 1], col)
            cols.append(col)
        sel = jnp.concatenate(cols, axis=1)
        hi = sel.astype(BF16)
        lo = (sel - hi.astype(F32)).astype(BF16)
        rhs = rhs_ref[...]
        y_ref[...] = _dot(hi, rhs) + _dot(lo, rhs)

    @pl.when(fits_ref[i] == 0)
    def _():
        slot = lax.broadcasted_iota(I32, (tm, cap), 1)
        y = jnp.zeros((tm, D_MODEL), F32)
        for e in range(N_EXPERTS):
            onehot = jnp.where(pos[:, e:e + 1] == slot, 1.0, 0.0).astype(BF16)
            y = y + aff[:, e:e + 1] * _dot(onehot, ye_ref[e])
        y_ref[...] = y

    o_ref[...] = x1_ref[...] + g2_ref[0] * _rms(y_ref[...], gp_ref[...])


def _combine(route, aff, ye, x1, mod3, gpost, row0, rows_per_cond):
    t = x1.shape[0]
    pos_t, win, fits = route.pos.T, route.win, route.fits
    e, cap, d = ye.shape
    tm = COMB_TILE
    return pl.pallas_call(
        _combine_kernel,
        grid_spec=pltpu.PrefetchScalarGridSpec(
            num_scalar_prefetch=2,
            grid=(t // tm,),
            in_specs=[pl.BlockSpec((tm, e), lambda i, *_: (i, 0)),
                      pl.BlockSpec((tm, LANES), lambda i, *_: (i, 0)),
                      pl.BlockSpec((e, cap, d), lambda i, *_: (0, 0, 0)),
                      pl.BlockSpec((tm, d), lambda i, *_: (i, 0)),
                      _mod_spec(5, tm, row0, rows_per_cond),
                      pl.BlockSpec((1, d), lambda i, *_: (0, 0))],
            out_specs=pl.BlockSpec((tm, d), lambda i, *_: (i, 0)),
            scratch_shapes=[pltpu.VMEM((e * COMB_WIN, d), BF16), pltpu.VMEM((tm, d), F32)]),
        out_shape=jax.ShapeDtypeStruct((t, d), F32),
        compiler_params=_cparams(("parallel",), VMEM_LIMIT),
        name="combine",
    )(win, fits, pos_t, aff, ye, x1, mod3, gpost.reshape(1, d))


def _head_rows(v):
    flat = jnp.concatenate([v.reshape(2 * S_HEADS), jnp.zeros((LANES - 2 * S_HEADS,), F32)])
    return flat.reshape(1, LANES), flat.reshape(LANES, 1)


def kernel(x_prompt, x_sample, cache_k, cache_v, state_ssd, c, c_ctx, w_ada, b_ada, g_pre_mix, g_post_mix, g_pre_ffn, g_post_ffn, w_in, lam_q1, lam_k1, lam_q2, lam_k2, w_subln, conv_w, conv_b, dt_bias, a_log, d_skip, w_ssd_norm, w_o_attn, w_o_ssd, w_out, w_router, w_gate, w_up, w_down):
    nb_c, n_c, d = x_prompt.shape
    nb_l, n_l, _ = x_sample.shape
    npast = cache_k.shape[2]
    depth = w_in.shape[0]
    assert depth == 1 and d == D_MODEL
    t_c, t_l = nb_c * n_c, nb_l * n_l
    lyr = 0
    lam_init = 0.8 - 0.6 * math.exp(-0.3 * lyr)

    cond8 = jnp.concatenate([c_ctx[None, :], c, jnp.zeros((8 - 1 - nb_l, d), F32)], axis=0)
    mod3 = _ada(cond8, w_ada[lyr], b_ada[lyr]).reshape(8, 1, N_MOD * d)

    w_in_t = w_in[lyr].T
    w_gates_t = w_in_t[IN_GATES:]
    w_dt_t = jnp.concatenate([w_in_t[IN_DT:IN_GATES], jnp.zeros((LANES - 2 * S_HEADS, d), F32)], axis=0).astype(BF16)
    w_dt = w_dt_t.T
    tn = INPROJ_TN
    blocks_of = lambda col0, width: [col0 // tn + j for j in range(width // tn)]
    proj_blocks = (blocks_of(IN_Z, S_INNER + S_CONV_CH) + [0] * (2 * d // tn) + blocks_of(IN_Q, d))
    gate_tiles = (PROJ_GATES // tn, PROJ_Q // tn)
    lam_p = jnp.stack([lam_q1[lyr], lam_k1[lyr], lam_q2[lyr], lam_k2[lyr]], axis=0)
    wsub = w_subln[lyr].reshape(1, A_V_DIM)
    dtb_row, dtb_col = _head_rows(dt_bias[lyr])
    alog_row, alog_col = _head_rows(a_log[lyr])
    dskip = jnp.repeat(d_skip[lyr], S_HEADDIM).reshape(1, S_INNER)
    wn = w_ssd_norm[lyr].reshape(1, S_INNER)
    woa = w_o_attn[lyr].astype(BF16)
    wos = w_o_ssd[lyr].astype(BF16)
    wout = w_out[lyr].astype(BF16)
    wr = jnp.concatenate([w_router[lyr], jnp.zeros((d, LANES - N_EXPERTS), F32)], axis=1)
    gpost = g_post_mix[lyr].reshape(1, d)
    gpre = g_pre_ffn[lyr].reshape(1, d)
    cos, sin = _rope_tables(n_l)

    xc = x_prompt.reshape(t_c, d)
    xl = x_sample.reshape(t_l, d)
    h_c = _prenorm(xc, mod3, g_pre_mix[lyr], 0, t_c)
    h_l = _prenorm(xl, mod3, g_pre_mix[lyr], 1, n_l)
    proj_c, proj_l = _inproj(h_c, h_l, w_in_t, proj_blocks, BF16, w_gates_t, gate_tiles)
    k_c, k_l = _inproj(h_c, h_l, w_in_t, blocks_of(IN_K, d), F32)
    v_c, v_l = _inproj(h_c, h_l, w_in_t, blocks_of(IN_V, d), F32)

    def mixer(x, h, proj, k, v, row0, rows_per_cond, nbatch, n, ctx):
        dt, dt_t = _dtproj(h, w_dt, w_dt_t)
        if ctx is None:
            o_attn = _attn_ctx(lam_p, proj, k, v, wsub, nbatch, n, lam_init)
            h0 = None
        else:
            kc, vc, h0 = ctx
            o_attn = _attn_lat(lam_p, proj, k, v, kc, vc, cos, sin, wsub, nbatch, n, npast, lam_init)
        xa = _conv(proj, conv_w[lyr], conv_b[lyr], nbatch, n)
        res = _ssd(xa, dt, dt_t, dtb_row, alog_row, dtb_col, alog_col, h0, nbatch, n, emit_final=ctx is None)
        yf, yb = res[0], res[1]
        x1, h2, aff = _mixout(yf, yb, xa, proj, o_attn, x, mod3, dskip, wn, woa, wos, wout, gpost, gpre, wr,
                              row0, rows_per_cond)
        fin = res[2] if ctx is None else None
        return x1, h2, aff, fin

    x1_c, h2_c, aff_c, fin_c = mixer(xc, h_c, proj_c, k_c, v_c, 0, t_c, nb_c, n_c, None)
    kc = cache_k[:, lyr].reshape(nb_l * npast, A_HEADS * 2 * A_HEAD_DIM)
    vc = cache_v[:, lyr].reshape(nb_l * npast, A_HEADS * A_V_DIM)
    h0 = state_ssd[:, lyr].reshape(nb_l, 2, S_INNER, S_STATE)
    x1_l, h2_l, aff_l, _ = mixer(xl, h_l, proj_l, k_l, v_l, 1, n_l, nb_l, n_l, (kc, vc, h0))

    cap_c = EC_FACTOR * t_c // N_EXPERTS
    cap_l = EC_FACTOR * t_l // N_EXPERTS
    route_c = _router(aff_c[:, :N_EXPERTS].T, cap_c)
    route_l = _router(aff_l[:, :N_EXPERTS].T, cap_l)
    xe_c = _dispatch(route_c, h2_c, cap_c)
    xe_l = _dispatch(route_l, h2_l, cap_l)
    ye_c, ye_l = _ffn(xe_c, xe_l, w_gate[lyr], w_up[lyr], w_down[lyr])
    out_c = _combine(route_c, aff_c, ye_c, x1_c, mod3, g_post_ffn[lyr], 0, t_c)
    out_l = _combine(route_l, aff_l, ye_l, x1_l, mod3, g_post_ffn[lyr], 1, n_l)

    y_prompt = out_c.reshape(nb_c, n_c, d)
    y_sample = out_l.reshape(nb_l, n_l, d)
    new_k = k_c.reshape(nb_c, 1, n_c, A_HEADS, 2 * A_HEAD_DIM)
    new_v = v_c.reshape(nb_c, 1, n_c, A_HEADS, A_V_DIM)
    new_state = fin_c.reshape(nb_c, 1, 2, S_HEADS, S_HEADDIM, S_STATE)
    return (y_prompt, y_sample, new_k, new_v, new_state)
```

```python
import functools
import math
from typing import NamedTuple

import jax
import jax.numpy as jnp
import numpy as np
from jax import lax
from jax.experimental import pallas as pl
from jax.experimental.pallas import tpu as pltpu

F32 = jnp.float32
BF16 = jnp.bfloat16
I32 = jnp.int32

D_MODEL = 1024
GRID_W = 64
EPS = 1e-6
N_MOD = 6
A_HEADS = 8
A_HEAD_DIM = 64
A_V_DIM = 128
ROPE_BASE = 10000.0
S_INNER = 2048
S_HEADDIM = 64
S_HEADS = 32
S_GROUPS = 4
S_GROUP_W = S_INNER // S_GROUPS
S_STATE = 128
S_CHUNK = 128
S_CONV_CH = S_INNER + 2 * S_GROUPS * S_STATE
N_EXPERTS = 16
EC_FACTOR = 2
D_FF = 2048
LANES = 128
IN_Q, IN_K, IN_V, IN_Z, IN_XBC = 0, 1024, 2048, 3072, 5120
IN_DT = IN_XBC + S_CONV_CH
IN_GATES = IN_DT + 2 * S_HEADS
PROJ_Z, PROJ_XBC, PROJ_GATES, PROJ_Q = 0, 2048, 5120, 7168
VMEM_LIMIT = 56 * 1024 * 1024
LOG2E = 1.4426950408889634


def _cparams(sem, vmem=None):
    return pltpu.CompilerParams(dimension_semantics=sem, vmem_limit_bytes=vmem)


def _dot(a, b):
    return jnp.dot(a, b, preferred_element_type=F32)


def _dot_nt(a, b):
    return lax.dot_general(a, b, (((1,), (1,)), ((), ())), preferred_element_type=F32)


def _split3(a):
    a1 = a.astype(BF16)
    r1 = a - a1.astype(F32)
    a2 = r1.astype(BF16)
    a3 = (r1 - a2.astype(F32)).astype(BF16)
    return a1, a2, a3


def _dot_f32_lhs(a, m):
    a1, a2, a3 = _split3(a)
    return _dot(a1, m) + (_dot(a2, m) + _dot(a3, m))


def _dot_f32_rhs(m, b):
    b1, b2, b3 = _split3(b)
    return _dot(m, b1) + (_dot(m, b2) + _dot(m, b3))


def _dot_hi(a, b):
    a1, a2, _ = _split3(a)
    b1, b2, _ = _split3(b)
    return _dot(a1, b1) + (_dot(a1, b2) + _dot(a2, b1))


def _silu(x):
    return x * jax.nn.sigmoid(x)


def _rms(x, w):
    return x * lax.rsqrt(jnp.mean(x * x, axis=-1, keepdims=True) + EPS) * w


def _ada_kernel(c_ref, w_ref, b_ref, o_ref):
    o_ref[...] = _dot_hi(_silu(c_ref[...]), w_ref[...]) + b_ref[...]


def _ada(cond8, w_ada, b_ada):
    n = w_ada.shape[1]
    tn = 1536
    return pl.pallas_call(
        _ada_kernel,
        grid=(n // tn,),
        in_specs=[pl.BlockSpec((8, D_MODEL), lambda j: (0, 0)),
                  pl.BlockSpec((D_MODEL, tn), lambda j: (0, j)),
                  pl.BlockSpec((1, tn), lambda j: (0, j))],
        out_specs=pl.BlockSpec((8, tn), lambda j: (0, j)),
        out_shape=jax.ShapeDtypeStruct((8, n), F32),
        compiler_params=_cparams(("arbitrary",), VMEM_LIMIT),
        name="ada",
    )(cond8, w_ada, b_ada.reshape(1, n))


def _mod_spec(col, tm, row0, rows_per_cond):
    return pl.BlockSpec((1, 1, D_MODEL), lambda i, *_: (row0 + (i * tm) // rows_per_cond, 0, col))


def _prenorm_kernel(x_ref, sh_ref, sc_ref, g_ref, o_ref):
    xn = _rms(x_ref[...], g_ref[...])
    o_ref[...] = (xn * (1.0 + sc_ref[0]) + sh_ref[0]).astype(o_ref.dtype)


def _prenorm(x, mod3, gain, row0, rows_per_cond):
    t = x.shape[0]
    tm = 512
    return pl.pallas_call(
        _prenorm_kernel,
        grid=(t // tm,),
        in_specs=[pl.BlockSpec((tm, D_MODEL), lambda i: (i, 0)),
                  _mod_spec(0, tm, row0, rows_per_cond),
                  _mod_spec(1, tm, row0, rows_per_cond),
                  pl.BlockSpec((1, D_MODEL), lambda i: (0, 0))],
        out_specs=pl.BlockSpec((tm, D_MODEL), lambda i: (i, 0)),
        out_shape=jax.ShapeDtypeStruct((t, D_MODEL), BF16),
        compiler_params=_cparams(("parallel",)),
        name="prenorm",
    )(x, mod3, mod3, gain.reshape(1, D_MODEL))


INPROJ_TM = 2048
INPROJ_TN = 512


def _inproj_kernel(blk_ref, nxt_ref, hc_ref, hl_ref, w_ref, wn_ref, oc_ref, ol_ref, wbf_ref,
                   *, sh_lo, sh_hi, shift, n_ctx_tiles):
    del blk_ref, nxt_ref
    j = pl.program_id(0)
    i = pl.program_id(1)
    shifted = jnp.logical_and(j >= sh_lo, j < sh_hi)

    if shift:
        @pl.when(jnp.logical_and(i == 0, shifted))
        def _():
            w = jnp.concatenate([w_ref[shift:, :], wn_ref[:shift, :]], axis=0)
            wbf_ref[...] = w.T.astype(BF16)

    @pl.when(jnp.logical_and(i == 0, jnp.logical_not(shifted)))
    def _():
        wbf_ref[...] = w_ref[...].T.astype(BF16)

    @pl.when(i < n_ctx_tiles)
    def _():
        oc_ref[...] = _dot(hc_ref[...], wbf_ref[...]).astype(oc_ref.dtype)

    @pl.when(i >= n_ctx_tiles)
    def _():
        ol_ref[...] = _dot(hl_ref[...], wbf_ref[...]).astype(ol_ref.dtype)


def _inproj(h_c, h_l, w_t, w_rows, out_dtype, shifted_tiles=(0, 0)):
    tm, tn = INPROJ_TM, INPROJ_TN
    t_c, k = h_c.shape
    t_l = h_l.shape[0]
    assert t_c % tm == 0 and t_l == tm
    nct = t_c // tm
    ncol = len(w_rows)
    sh_lo, sh_hi = shifted_tiles
    shifts = {w_rows[j] % tn for j in range(sh_lo, sh_hi)}
    assert len(shifts) <= 1 and all(w_rows[j] % tn == 0 for j in range(ncol) if not sh_lo <= j < sh_hi)
    shift = shifts.pop() if shifts else 0
    assert shift % 8 == 0
    blocks = [r // tn for r in w_rows]
    nxt = [b + 1 if sh_lo <= j < sh_hi else blocks[sh_lo] if sh_hi > sh_lo else 0 for j, b in enumerate(blocks)]
    return pl.pallas_call(
        functools.partial(_inproj_kernel, sh_lo=sh_lo, sh_hi=sh_hi, shift=shift, n_ctx_tiles=nct),
        grid_spec=pltpu.PrefetchScalarGridSpec(
            num_scalar_prefetch=2,
            grid=(ncol, nct + 1),
            in_specs=[pl.BlockSpec((tm, k), lambda j, i, blk, nx: (jnp.minimum(i, nct - 1), 0)),
                      pl.BlockSpec((tm, k), lambda j, i, blk, nx: (0, 0)),
                      pl.BlockSpec((tn, k), lambda j, i, blk, nx: (blk[j], 0)),
                      pl.BlockSpec((tn, k), lambda j, i, blk, nx: (nx[j], 0))],
            out_specs=[pl.BlockSpec((tm, tn), lambda j, i, blk, nx: (jnp.minimum(i, nct - 1), j)),
                       pl.BlockSpec((tm, tn), lambda j, i, blk, nx: (0, j))],
            scratch_shapes=[pltpu.VMEM((k, tn), BF16)]),
        out_shape=[jax.ShapeDtypeStruct((t_c, ncol * tn), out_dtype),
                   jax.ShapeDtypeStruct((t_l, ncol * tn), out_dtype)],
        compiler_params=_cparams(("arbitrary", "arbitrary"), VMEM_LIMIT),
        name="inproj",
    )(jnp.asarray(blocks, I32), jnp.asarray(nxt, I32), h_c, h_l, w_t, w_t)


def _dtproj_kernel(h_ref, w_ref, wt_ref, o_ref, ot_ref):
    h = h_ref[...]
    o_ref[...] = _dot(h, w_ref[...])
    ot_ref[...] = _dot_nt(wt_ref[...], h)


def _dtproj(h, w_dt, w_dt_t):
    t, k = h.shape
    tm = 1024
    return pl.pallas_call(
        _dtproj_kernel,
        grid=(t // tm,),
        in_specs=[pl.BlockSpec((tm, k), lambda i: (i, 0)),
                  pl.BlockSpec((k, LANES), lambda i: (0, 0)),
                  pl.BlockSpec((LANES, k), lambda i: (0, 0))],
        out_specs=[pl.BlockSpec((tm, LANES), lambda i: (i, 0)),
                   pl.BlockSpec((LANES, tm), lambda i: (0, i))],
        out_shape=[jax.ShapeDtypeStruct((t, LANES), F32), jax.ShapeDtypeStruct((LANES, t), F32)],
        compiler_params=_cparams(("parallel",)),
        name="dtproj",
    )(h, w_dt, w_dt_t)


def _lam(lp_ref, lam_init):
    lp = lp_ref[...]
    s1 = jnp.sum(lp[0:1] * lp[1:2], axis=-1, keepdims=True)
    s2 = jnp.sum(lp[2:3] * lp[3:4], axis=-1, keepdims=True)
    return jnp.exp(s1) - jnp.exp(s2) + lam_init


def _diff_attn(q, k, v, lam, wsub, lam_init):
    nq = q.shape[0]
    lane = lax.broadcasted_iota(I32, q.shape, 1)
    q = q * (A_HEAD_DIM ** -0.5 * LOG2E)
    q1 = jnp.where(lane < A_HEAD_DIM, q, 0.0).astype(BF16)
    q2 = jnp.where(lane >= A_HEAD_DIM, q, 0.0).astype(BF16)
    s = _dot_nt(jnp.concatenate([q1, q2], axis=0), k)
    p = jnp.exp2(s - jnp.max(s, axis=-1, keepdims=True))
    r = 1.0 / jnp.sum(p, axis=-1, keepdims=True)
    pv = _dot(p.astype(BF16), v) * r
    o = pv[:nq] - lam * pv[nq:]
    return _rms(o, wsub) * (1.0 - lam_init)


def _attn_ctx_kernel(lp_ref, q_ref, k_ref, v_ref, ws_ref, o_ref, *, lam_init):
    lam = _lam(lp_ref, lam_init)
    for h in range(A_HEADS):
        sl = slice(h * LANES, (h + 1) * LANES)
        o = _diff_attn(q_ref[:, sl].astype(F32), k_ref[:, sl].astype(BF16), v_ref[:, sl].astype(BF16), lam,
                       ws_ref[...], lam_init)
        o_ref[:, sl] = o.astype(o_ref.dtype)


def _attn_ctx(lam_p, proj, k, v, wsub, nbatch, n, lam_init):
    w = A_HEADS * LANES
    blk = pl.BlockSpec((n, w), lambda b: (b, 0))
    return pl.pallas_call(
        functools.partial(_attn_ctx_kernel, lam_init=lam_init),
        grid=(nbatch,),
        in_specs=[pl.BlockSpec((4, A_HEAD_DIM), lambda b: (0, 0)),
                  pl.BlockSpec((n, w), lambda b: (b, PROJ_Q // w)), blk, blk,
                  pl.BlockSpec((1, A_V_DIM), lambda b: (0, 0))],
        out_specs=blk,
        out_shape=jax.ShapeDtypeStruct(k.shape, BF16),
        compiler_params=_cparams(("parallel",), VMEM_LIMIT),
        name="attn_ctx",
    )(lam_p, proj, k, v, wsub)


def _rope(x, cos, sin_signed):
    lane = lax.broadcasted_iota(I32, x.shape, 1)
    quarter = A_HEAD_DIM // 4
    partner = jnp.where((lane % (2 * quarter)) < quarter,
                        pltpu.roll(x, LANES - quarter, axis=1),
                        pltpu.roll(x, quarter, axis=1))
    return x * cos + partner * sin_signed


def _attn_lat_kernel(lp_ref, q_ref, k_ref, v_ref, kc_ref, vc_ref, cos_ref, sin_ref, ws_ref, o_ref, *, lam_init, tq):
    lam = _lam(lp_ref, lam_init)
    k = _rope(k_ref[...], cos_ref[...], sin_ref[...])
    k_all = jnp.concatenate([kc_ref[...].astype(BF16), k.astype(BF16)], axis=0)
    v_all = jnp.concatenate([vc_ref[...].astype(BF16), v_ref[...].astype(BF16)], axis=0)
    for i in range(q_ref.shape[0] // tq):
        rows = slice(i * tq, (i + 1) * tq)
        q = _rope(q_ref[rows, :].astype(F32), cos_ref[rows, :], sin_ref[rows, :])
        o_ref[rows, :] = _diff_attn(q, k_all, v_all, lam, ws_ref[...], lam_init).astype(o_ref.dtype)


def _attn_lat(lam_p, proj, k, v, kc, vc, cos, sin, wsub, nbatch, n, npast, lam_init):
    blk = pl.BlockSpec((n, LANES), lambda b, h: (b, h))
    cblk = pl.BlockSpec((npast, LANES), lambda b, h: (b, h))
    tab = pl.BlockSpec((n, LANES), lambda b, h: (0, 0))
    return pl.pallas_call(
        functools.partial(_attn_lat_kernel, lam_init=lam_init, tq=256),
        grid=(nbatch, A_HEADS),
        in_specs=[pl.BlockSpec((4, A_HEAD_DIM), lambda b, h: (0, 0)),
                  pl.BlockSpec((n, LANES), lambda b, h: (b, PROJ_Q // LANES + h)), blk, blk, cblk, cblk, tab, tab,
                  pl.BlockSpec((1, A_V_DIM), lambda b, h: (0, 0))],
        out_specs=blk,
        out_shape=jax.ShapeDtypeStruct(k.shape, BF16),
        compiler_params=_cparams(("parallel", "parallel"), VMEM_LIMIT),
        name="attn_lat",
    )(lam_p, proj, k, v, kc, vc, cos, sin, wsub)


def _rope_tables(n):
    pos = np.arange(n)
    rowcol = np.stack([pos // GRID_W, pos % GRID_W], axis=-1).astype(np.float64)
    nf = A_HEAD_DIM // 4
    inv_freq = np.power(ROPE_BASE, -np.arange(nf, dtype=np.float64) / nf)
    lane = np.arange(LANES)
    axis = (lane % A_HEAD_DIM) // (2 * nf)
    ang = rowcol[:, axis] * inv_freq[lane % nf][None, :]
    sign = np.where((lane % (2 * nf)) < nf, -1.0, 1.0)
    return jnp.asarray(np.cos(ang), F32), jnp.asarray(np.sin(ang) * sign[None, :], F32)


def _conv_kernel(x_ref, w_ref, b_ref, o_ref, *, n):
    x = x_ref[...].astype(F32)
    rows = x.shape[0]
    pos = lax.broadcasted_iota(I32, x.shape, 0) % n
    prev = jnp.where(pos == 0, 0.0, pltpu.roll(x, 1, axis=0))
    nxt = jnp.where(pos == n - 1, 0.0, pltpu.roll(x, rows - 1, axis=0))
    w = w_ref[...]
    o_ref[...] = _silu(prev * w[0:1] + x * w[1:2] + nxt * w[2:3] + b_ref[...]).astype(o_ref.dtype)


def _conv(proj, conv_w, conv_b, n):
    tc = 1024
    tr = 1024
    c = S_CONV_CH
    t = proj.shape[0]
    assert tr % n == 0 and t % tr == 0
    return pl.pallas_call(
        functools.partial(_conv_kernel, n=n),
        grid=(t // tr, c // tc),
        in_specs=[pl.BlockSpec((tr, tc), lambda b, j: (b, PROJ_XBC // tc + j)),
                  pl.BlockSpec((3, tc), lambda b, j: (0, j)),
                  pl.BlockSpec((1, tc), lambda b, j: (0, j))],
        out_specs=pl.BlockSpec((tr, tc), lambda b, j: (b, j)),
        out_shape=jax.ShapeDtypeStruct((t, c), BF16),
        compiler_params=_cparams(("parallel", "parallel"), VMEM_LIMIT),
        name="conv",
    )(proj, conv_w, conv_b.reshape(1, c))


S_PAIRS = S_INNER // LANES


def _ssd_direction(xa_ref, dt_raw, dt_raw_t, dtb_row, alog_row, dtb_col, alog_col, ht_ref, y_ref, backward):
    col0 = S_HEADS if backward else 0
    row = lax.broadcasted_iota(I32, (S_CHUNK, S_CHUNK), 0)
    col = lax.broadcasted_iota(I32, (S_CHUNK, S_CHUNK), 1)
    upper = jnp.where(row <= col, 1.0, 0.0).astype(BF16)
    lower = jnp.where(row >= col, 1.0, 0.0).astype(BF16)
    dt = jax.nn.softplus(dt_raw + dtb_row)
    dt_t = jax.nn.softplus(dt_raw_t + dtb_col)
    da = dt * (-jnp.exp(alog_row) * LOG2E)
    da_t = dt_t * (-jnp.exp(alog_col) * LOG2E)
    if backward:
        cum = _dot_f32_rhs(upper, da)
        cum_t = _dot_f32_lhs(da_t, lower)
        last = 0
        keep = row <= col
    else:
        cum = _dot_f32_rhs(lower, da)
        cum_t = _dot_f32_lhs(da_t, upper)
        last = S_CHUNK - 1
        keep = row >= col
    w_t = dt_t * jnp.exp2(cum_t[:, last:last + 1] - cum_t)
    e_tot = jnp.exp2(cum[last:last + 1, :])
    lo_half = col < S_HEADDIM
    lo_bf = jnp.where(lo_half, 1.0, 0.0).astype(BF16)
    hi_bf = jnp.where(lo_half, 0.0, 1.0).astype(BF16)
    for g in range(S_GROUPS):
        b_bf = xa_ref[:, S_INNER + g * S_STATE:S_INNER + (g + 1) * S_STATE]
        c_bf = xa_ref[:, S_INNER + (S_GROUPS + g) * S_STATE:S_INNER + (S_GROUPS + g + 1) * S_STATE]
        cb = _dot_nt(c_bf, b_bf)
        bt = b_bf.astype(F32).T
        c_f = c_bf.astype(F32)
        for pr in range(S_PAIRS // S_GROUPS):
            hp = g * (S_PAIRS // S_GROUPS) + pr
            c0 = col0 + 2 * hp
            x_pair = xa_ref[:, hp * LANES:(hp + 1) * LANES]
            bd_x = jnp.concatenate([x_pair * lo_bf, x_pair * hi_bf], axis=0)
            ht = ht_ref[hp]
            bd_h = jnp.concatenate([jnp.where(lo_half, ht, 0.0), jnp.where(lo_half, 0.0, ht)], axis=0).astype(BF16)
            ms, ces, bws = [], [], []
            for cc in (c0, c0 + 1):
                bc = jnp.broadcast_to(cum[:, cc:cc + 1], (S_CHUNK, S_CHUNK))
                decay = jnp.exp2(jnp.where(keep, bc - cum_t[cc:cc + 1, :], -1e30))
                ms.append((cb * decay * dt_t[cc:cc + 1, :]).astype(BF16))
                ces.append((c_f * jnp.exp2(bc)).astype(BF16))
                bws.append((bt * w_t[cc:cc + 1, :]).astype(BF16))
            y_ref[:, hp * LANES:(hp + 1) * LANES] = _dot(jnp.concatenate(ms + ces, axis=1),
                                                         jnp.concatenate([bd_x, bd_h], axis=0)).astype(y_ref.dtype)
            tot = jnp.where(lo_half[0:1], e_tot[:, c0:c0 + 1], e_tot[:, c0 + 1:c0 + 2])
            ht_ref[hp] = ht * tot + _dot(jnp.concatenate(bws, axis=1), bd_x)


def _ssd_kernel(*refs, has_h0, emit_final):
    xf_ref, xb_ref, dtf_ref, dtb_ref, dtft_ref, dtbt_ref, br_ref, ar_ref, bc_ref, ac_ref = refs[:10]
    refs = refs[10:]
    if has_h0:
        h0_ref, refs = refs[0], refs[1:]
    yf_ref, yb_ref = refs[:2]
    refs = refs[2:]
    if emit_final:
        fin_ref, refs = refs[0], refs[1:]
    htf_ref, htb_ref = refs
    c = pl.program_id(1)
    nc = pl.num_programs(1)

    @pl.when(c == 0)
    def _():
        for d, ht_ref in enumerate((htf_ref, htb_ref)):
            for hp in range(S_PAIRS):
                if has_h0:
                    ht_ref[hp] = h0_ref[0, d, hp * LANES:(hp + 1) * LANES, :].T
                else:
                    ht_ref[hp] = jnp.zeros((S_STATE, LANES), F32)

    _ssd_direction(xf_ref, dtf_ref[...], dtft_ref[...], br_ref[...], ar_ref[...], bc_ref[...], ac_ref[...],
                   htf_ref, yf_ref, backward=False)
    _ssd_direction(xb_ref, dtb_ref[...], dtbt_ref[...], br_ref[...], ar_ref[...], bc_ref[...], ac_ref[...],
                   htb_ref, yb_ref, backward=True)

    if emit_final:
        @pl.when(c == nc - 1)
        def _():
            for d, ht_ref in enumerate((htf_ref, htb_ref)):
                for hp in range(S_PAIRS):
                    fin_ref[0, d, hp * LANES:(hp + 1) * LANES, :] = ht_ref[hp].T


def _ssd(xa, dt, dt_t, dtb_row, alog_row, dtb_col, alog_col, h0, nbatch, n, emit_final):
    nc = n // S_CHUNK
    t = xa.shape[0]
    fwd = lambda b, c: (b * nc + c, 0)
    bwd = lambda b, c: (b * nc + nc - 1 - c, 0)
    const = lambda b, c: (0, 0)
    in_specs = [pl.BlockSpec((S_CHUNK, S_CONV_CH), fwd),
                pl.BlockSpec((S_CHUNK, S_CONV_CH), bwd),
                pl.BlockSpec((S_CHUNK, LANES), fwd),
                pl.BlockSpec((S_CHUNK, LANES), bwd),
                pl.BlockSpec((LANES, S_CHUNK), lambda b, c: (0, b * nc + c)),
                pl.BlockSpec((LANES, S_CHUNK), lambda b, c: (0, b * nc + nc - 1 - c)),
                pl.BlockSpec((1, LANES), const), pl.BlockSpec((1, LANES), const),
                pl.BlockSpec((LANES, 1), const), pl.BlockSpec((LANES, 1), const)]
    args = [xa, xa, dt, dt, dt_t, dt_t, dtb_row, alog_row, dtb_col, alog_col]
    if h0 is not None:
        in_specs.append(pl.BlockSpec((1, 2, S_INNER, S_STATE), lambda b, c: (b, 0, 0, 0)))
        args.append(h0)
    out_specs = [pl.BlockSpec((S_CHUNK, S_INNER), fwd), pl.BlockSpec((S_CHUNK, S_INNER), bwd)]
    out_shape = [jax.ShapeDtypeStruct((t, S_INNER), BF16), jax.ShapeDtypeStruct((t, S_INNER), BF16)]
    if emit_final:
        out_specs.append(pl.BlockSpec((1, 2, S_INNER, S_STATE), lambda b, c: (b, 0, 0, 0)))
        out_shape.append(jax.ShapeDtypeStruct((nbatch, 2, S_INNER, S_STATE), F32))
    return pl.pallas_call(
        functools.partial(_ssd_kernel, has_h0=h0 is not None, emit_final=emit_final),
        grid=(nbatch, nc),
        in_specs=in_specs,
        out_specs=out_specs,
        out_shape=out_shape,
        scratch_shapes=[pltpu.VMEM((S_PAIRS, S_STATE, LANES), F32),
                        pltpu.VMEM((S_PAIRS, S_STATE, LANES), F32)],
        compiler_params=_cparams(("parallel", "arbitrary"), VMEM_LIMIT),
        name="ssd",
    )(*args)


def _mixout_kernel(yf_ref, yb_ref, xs_ref, z_ref, oa_ref, ga_ref, gs_ref, x_ref, g1_ref, sh2_ref, sc2_ref,
                   dskip_ref, wn_ref, woa_ref, wos_ref, wout_ref, gpost_ref, gpre_ref, wr_ref,
                   x1_ref, h2_ref, aff_ref):
    wn = wn_ref[...]
    half = x_ref.shape[0] // 2
    halves = [slice(r * half, (r + 1) * half) for r in range(2)]

    def ssd_post(rows):
        y = (yf_ref[rows, :].astype(F32) + yb_ref[rows, :].astype(F32)
             + xs_ref[rows, :].astype(F32) * dskip_ref[...])
        y = y * _silu(z_ref[rows, :].astype(F32))
        parts = []
        for g in range(S_GROUPS):
            sl = slice(g * S_GROUP_W, (g + 1) * S_GROUP_W)
            parts.append(_rms(y[:, sl], wn[:, sl]).astype(BF16))
        return jnp.concatenate(parts, axis=1)

    o_ssds = [ssd_post(rows) for rows in halves]
    for rows, o_ssd in zip(halves, o_ssds):
        a = _dot(oa_ref[rows, :], woa_ref[...])
        s = _dot(o_ssd, wos_ref[...])
        merged = (jax.nn.sigmoid(ga_ref[rows, :].astype(F32)) * a
                  + jax.nn.sigmoid(gs_ref[rows, :].astype(F32)) * s)
        ym = _dot(merged.astype(BF16), wout_ref[...])
        x1 = x_ref[rows, :] + g1_ref[0] * _rms(ym, gpost_ref[...])
        x1_ref[rows, :] = x1
        h2 = _rms(x1, gpre_ref[...]) * (1.0 + sc2_ref[0]) + sh2_ref[0]
        h2_ref[rows, :] = h2.astype(BF16)
        logits = _dot_hi(h2, wr_ref[...])
        lane = lax.broadcasted_iota(I32, logits.shape, 1)
        logits = jnp.where(lane < N_EXPERTS, logits, -1e30)
        p = jnp.exp(logits - jnp.max(logits, axis=-1, keepdims=True))
        aff_ref[rows, :] = p / jnp.sum(p, axis=-1, keepdims=True)


def _mixout(yf, yb, xa, proj, o_attn, x, mod3, dskip, wn, woa, wos, wout, gpost, gpre, wr, row0, rows_per_cond):
    t = x.shape[0]
    tm = 512
    tok = lambda w: pl.BlockSpec((tm, w), lambda i: (i, 0))
    const = lambda a: pl.BlockSpec(a.shape, lambda i: (0,) * a.ndim, pipeline_mode=pl.Buffered(1))
    return pl.pallas_call(
        _mixout_kernel,
        grid=(t // tm,),
        in_specs=[tok(S_INNER), tok(S_INNER), tok(S_INNER),
                  pl.BlockSpec((tm, S_INNER), lambda i: (i, PROJ_Z // S_INNER)), tok(D_MODEL),
                  pl.BlockSpec((tm, D_MODEL), lambda i: (i, PROJ_GATES // D_MODEL)),
                  pl.BlockSpec((tm, D_MODEL), lambda i: (i, PROJ_GATES // D_MODEL + 1)),
                  tok(D_MODEL),
                  _mod_spec(2, tm, row0, rows_per_cond),
                  _mod_spec(3, tm, row0, rows_per_cond),
                  _mod_spec(4, tm, row0, rows_per_cond),
                  const(dskip), const(wn), const(woa), const(wos), const(wout), const(gpost), const(gpre),
                  const(wr)],
        out_specs=[tok(D_MODEL), tok(D_MODEL), tok(LANES)],
        out_shape=[jax.ShapeDtypeStruct((t, D_MODEL), F32), jax.ShapeDtypeStruct((t, D_MODEL), BF16),
                   jax.ShapeDtypeStruct((t, LANES), F32)],
        compiler_params=_cparams(("parallel",), VMEM_LIMIT),
        name="mixout",
    )(yf, yb, xa, proj, o_attn, proj, proj, x, mod3, mod3, mod3, dskip, wn, woa, wos, wout, gpost, gpre, wr)


COMB_TILE = 256
COMB_WIN = 64
BF16_ROWS = 16


def _router_kernel(aff_ref, pos_ref, cnt_ref, *, cap):
    aff = aff_ref[...]
    e, t = aff.shape
    idx = lax.broadcasted_iota(I32, (e, t), 1)

    def count(m):
        return jnp.sum(jnp.where(m, 1.0, 0.0), axis=-1, keepdims=True)

    def value_bit(i, thr):
        cand = thr | jnp.left_shift(jnp.int32(1), 30 - i)
        return jnp.where(count(aff >= pltpu.bitcast(cand, F32)) >= cap, cand, thr)

    thr = pltpu.bitcast(lax.fori_loop(0, 31, value_bit, jnp.zeros((e, 1), I32)), F32)
    gt = aff > thr
    eq = aff == thr
    need = cap - count(gt)
    nbits = (t - 1).bit_length()

    def index_bit(i, v):
        cand = v | jnp.left_shift(jnp.int32(1), nbits - 1 - i)
        return jnp.where(count(jnp.logical_and(eq, idx < cand)) < need, cand, v)

    last = lax.fori_loop(0, nbits, index_bit, jnp.zeros((e, 1), I32))
    sel = jnp.logical_or(gt, jnp.logical_and(eq, idx <= last))
    sel_bf = jnp.where(sel, 1.0, 0.0).astype(BF16)
    blk = 512
    r = lax.broadcasted_iota(I32, (blk, blk), 0)
    c = lax.broadcasted_iota(I32, (blk, blk), 1)
    upper = jnp.where(r <= c, 1.0, 0.0).astype(BF16)
    carry = jnp.zeros((e, 1), F32)
    tile_cnt = []
    for b in range(t // blk):
        cum = _dot(sel_bf[:, b * blk:(b + 1) * blk], upper) + carry
        carry = cum[:, blk - 1:blk]
        slot = jnp.where(sel[:, b * blk:(b + 1) * blk], cum - 1.0, -1.0)
        pos_ref[:, b * blk:(b + 1) * blk] = slot.astype(I32)
        for j in range(blk // COMB_TILE):
            tile_cnt.append(cum[:, (j + 1) * COMB_TILE - 1:(j + 1) * COMB_TILE])
    lane = lax.broadcasted_iota(I32, (e, LANES), 1)
    cnt = jnp.zeros((e, LANES), F32)
    for i, v in enumerate(tile_cnt):
        cnt = jnp.where(lane == i, v, cnt)
    cnt_ref[...] = cnt.astype(I32)


class _Route(NamedTuple):
    pos: jax.Array
    start: jax.Array
    end: jax.Array
    win: jax.Array
    fits: jax.Array


def _router(aff_t, cap):
    e, t = aff_t.shape
    ntile = t // COMB_TILE
    pos, cnt = pl.pallas_call(
        functools.partial(_router_kernel, cap=cap),
        out_shape=[jax.ShapeDtypeStruct((e, t), I32), jax.ShapeDtypeStruct((e, LANES), I32)],
        name="router",
    )(aff_t)
    end = cnt[:, :ntile]
    start = jnp.concatenate([jnp.zeros((e, 1), I32), end[:, :-1]], axis=1)
    win = jnp.minimum(start // BF16_ROWS * BF16_ROWS, cap - COMB_WIN)
    fits = jnp.all(end - win <= COMB_WIN, axis=0).astype(I32)
    flat = lambda a: a.T.reshape(ntile * e)
    return _Route(pos, flat(start), flat(end), flat(win), fits)


def _dispatch_kernel(start_ref, end_ref, win_ref, fits_ref, pos_ref, h_ref, o_ref):
    i = pl.program_id(0)
    ne, cap, _ = o_ref.shape
    tm = h_ref.shape[0]

    @pl.when(i == 0)
    def _():
        o_ref[...] = jnp.zeros(o_ref.shape, o_ref.dtype)

    h = h_ref[...]
    row = lax.broadcasted_iota(I32, (COMB_WIN, tm), 0)
    row_l = lax.broadcasted_iota(I32, (COMB_WIN, LANES), 0)

    def place(e, r0, rows):
        rel = r0 - start_ref[i * ne + e] + row_l
        mine = jnp.logical_and(rel >= 0, rel < end_ref[i * ne + e] - start_ref[i * ne + e])
        for c in range(D_MODEL // LANES):
            sl = slice(c * LANES, (c + 1) * LANES)
            old = o_ref[e, pl.ds(r0, COMB_WIN), sl].astype(F32)
            o_ref[e, pl.ds(r0, COMB_WIN), sl] = jnp.where(mine, rows[:, sl], old).astype(o_ref.dtype)

    @pl.when(fits_ref[i] == 1)
    def _():
        sel = [jnp.where(pos_ref[e:e + 1, :] == win_ref[i * ne + e] + row, 1.0, 0.0).astype(BF16) for e in range(ne)]
        rows = _dot(jnp.concatenate(sel, axis=0), h)
        for e in range(ne):
            place(e, pl.multiple_of(win_ref[i * ne + e], BF16_ROWS), rows[e * COMB_WIN:(e + 1) * COMB_WIN])

    @pl.when(fits_ref[i] == 0)
    def _():
        for e in range(ne):
            w0 = win_ref[i * ne + e]

            def window(w, carry):
                r0 = pl.multiple_of(jnp.minimum(w0 + w * COMB_WIN, cap - COMB_WIN), BF16_ROWS)
                sel = jnp.where(pos_ref[e:e + 1, :] == r0 + row, 1.0, 0.0).astype(BF16)
                place(e, r0, _dot(sel, h))
                return carry

            lax.fori_loop(0, (end_ref[i * ne + e] - w0 + COMB_WIN - 1) // COMB_WIN, window, 0)


def _dispatch(route, h2, cap):
    e, t = route.pos.shape
    tm = COMB_TILE
    return pl.pallas_call(
        _dispatch_kernel,
        grid_spec=pltpu.PrefetchScalarGridSpec(
            num_scalar_prefetch=4,
            grid=(t // tm,),
            in_specs=[pl.BlockSpec((e, tm), lambda i, *_: (0, i)),
                      pl.BlockSpec((tm, D_MODEL), lambda i, *_: (i, 0))],
            out_specs=pl.BlockSpec((e, cap, D_MODEL), lambda i, *_: (0, 0, 0))),
        out_shape=jax.ShapeDtypeStruct((e, cap, D_MODEL), BF16),
        compiler_params=_cparams(("arbitrary",), VMEM_LIMIT),
        name="dispatch",
    )(route.start, route.end, route.win, route.fits, route.pos, h2)


def _ffn_kernel(xc_ref, xl_ref, wg_ref, wu_ref, wd_ref, oc_ref, ol_ref, acc_ref):
    f = pl.program_id(1)
    cc = xc_ref.shape[1]
    xe = jnp.concatenate([xc_ref[0], xl_ref[0]], axis=0)
    hg = _dot(xe, wg_ref[0].astype(BF16))
    hu = _dot(xe, wu_ref[0].astype(BF16))
    part = _dot((_silu(hg) * hu).astype(BF16), wd_ref[0].astype(BF16))

    @pl.when(f == 0)
    def _():
        acc_ref[...] = part

    @pl.when(f > 0)
    def _():
        acc_ref[...] += part

    @pl.when(f == pl.num_programs(1) - 1)
    def _():
        oc_ref[0] = acc_ref[:cc, :].astype(BF16)
        ol_ref[0] = acc_ref[cc:, :].astype(BF16)


def _ffn(xe_c, xe_l, w_gate, w_up, w_down):
    e, cc, d = xe_c.shape
    cl = xe_l.shape[1]
    tf = 1024
    return pl.pallas_call(
        _ffn_kernel,
        grid=(e, D_FF // tf),
        in_specs=[pl.BlockSpec((1, cc, d), lambda i, f: (i, 0, 0)),
                  pl.BlockSpec((1, cl, d), lambda i, f: (i, 0, 0)),
                  pl.BlockSpec((1, d, tf), lambda i, f: (i, 0, f)),
                  pl.BlockSpec((1, d, tf), lambda i, f: (i, 0, f)),
                  pl.BlockSpec((1, tf, d), lambda i, f: (i, f, 0))],
        out_specs=[pl.BlockSpec((1, cc, d), lambda i, f: (i, 0, 0)),
                   pl.BlockSpec((1, cl, d), lambda i, f: (i, 0, 0))],
        out_shape=[jax.ShapeDtypeStruct((e, cc, d), BF16), jax.ShapeDtypeStruct((e, cl, d), BF16)],
        scratch_shapes=[pltpu.VMEM((cc + cl, d), F32)],
        compiler_params=_cparams(("parallel", "arbitrary"), VMEM_LIMIT),
        name="ffn",
    )(xe_c, xe_l, w_gate, w_up, w_down)


def _combine_kernel(win_ref, fits_ref, pos_ref, aff_ref, ye_ref, x1_ref, g2_ref, gp_ref, o_ref, rhs_ref, y_ref):
    i = pl.program_id(0)
    tm = pos_ref.shape[0]
    cap = ye_ref.shape[1]
    pos = pos_ref[...]
    aff = aff_ref[...]

    @pl.when(fits_ref[i] == 1)
    def _():
        per_col = LANES // COMB_WIN
        lane = lax.broadcasted_iota(I32, (tm, LANES), 1)
        cols = []
        for c in range(N_EXPERTS // per_col):
            col = jnp.zeros((tm, LANES), F32)
            for j in range(per_col):
                e = c * per_col + j
                w0 = pl.multiple_of(win_ref[i * N_EXPERTS + e], BF16_ROWS)
                rhs_ref[e * COMB_WIN:(e + 1) * COMB_WIN, :] = ye_ref[e, pl.ds(w0, COMB_WIN), :]
                p = pos[:, e:e + 1]
                tgt = jnp.where(p >= 0, p - w0 + j * COMB_WIN, -1)
                col = jnp.where(lane == tgt, aff[:, e:e + 1], col)
            cols.append(col)
        sel = jnp.concatenate(cols, axis=1)
        hi = sel.astype(BF16)
        lo = (sel - hi.astype(F32)).astype(BF16)
        rhs = rhs_ref[...]
        y_ref[...] = _dot(hi, rhs) + _dot(lo, rhs)

    @pl.when(fits_ref[i] == 0)
    def _():
        slot = lax.broadcasted_iota(I32, (tm, cap), 1)
        y = jnp.zeros((tm, D_MODEL), F32)
        for e in range(N_EXPERTS):
            onehot = jnp.where(pos[:, e:e + 1] == slot, 1.0, 0.0).astype(BF16)
            y = y + aff[:, e:e + 1] * _dot(onehot, ye_ref[e])
        y_ref[...] = y

    o_ref[...] = x1_ref[...] + g2_ref[0] * _rms(y_ref[...], gp_ref[...])


def _combine(route, aff, ye, x1, mod3, gpost, row0, rows_per_cond):
    t = x1.shape[0]
    pos_t, win, fits = route.pos.T, route.win, route.fits
    e, cap, d = ye.shape
    tm = COMB_TILE
    return pl.pallas_call(
        _combine_kernel,
        grid_spec=pltpu.PrefetchScalarGridSpec(
            num_scalar_prefetch=2,
            grid=(t // tm,),
            in_specs=[pl.BlockSpec((tm, e), lambda i, *_: (i, 0)),
                      pl.BlockSpec((tm, LANES), lambda i, *_: (i, 0)),
                      pl.BlockSpec((e, cap, d), lambda i, *_: (0, 0, 0)),
                      pl.BlockSpec((tm, d), lambda i, *_: (i, 0)),
                      _mod_spec(5, tm, row0, rows_per_cond),
                      pl.BlockSpec((1, d), lambda i, *_: (0, 0))],
            out_specs=pl.BlockSpec((tm, d), lambda i, *_: (i, 0)),
            scratch_shapes=[pltpu.VMEM((e * COMB_WIN, d), BF16), pltpu.VMEM((tm, d), F32)]),
        out_shape=jax.ShapeDtypeStruct((t, d), F32),
        compiler_params=_cparams(("parallel",), VMEM_LIMIT),
        name="combine",
    )(win, fits, pos_t, aff, ye, x1, mod3, gpost.reshape(1, d))


def _head_rows(v):
    flat = jnp.concatenate([v.reshape(2 * S_HEADS), jnp.zeros((LANES - 2 * S_HEADS,), F32)])
    return flat.reshape(1, LANES), flat.reshape(LANES, 1)


def kernel(x_prompt, x_sample, cache_k, cache_v, state_ssd, c, c_ctx, w_ada, b_ada, g_pre_mix, g_post_mix, g_pre_ffn, g_post_ffn, w_in, lam_q1, lam_k1, lam_q2, lam_k2, w_subln, conv_w, conv_b, dt_bias, a_log, d_skip, w_ssd_norm, w_o_attn, w_o_ssd, w_out, w_router, w_gate, w_up, w_down):
    nb_c, n_c, d = x_prompt.shape
    nb_l, n_l, _ = x_sample.shape
    npast = cache_k.shape[2]
    depth = w_in.shape[0]
    assert depth == 1 and d == D_MODEL
    t_c, t_l = nb_c * n_c, nb_l * n_l
    lyr = 0
    lam_init = 0.8 - 0.6 * math.exp(-0.3 * lyr)

    cond8 = jnp.concatenate([c_ctx[None, :], c, jnp.zeros((8 - 1 - nb_l, d), F32)], axis=0)
    mod3 = _ada(cond8, w_ada[lyr], b_ada[lyr]).reshape(8, 1, N_MOD * d)

    w_in_t = w_in[lyr].T
    w_dt_t = jnp.concatenate([w_in_t[IN_DT:IN_GATES], jnp.zeros((LANES - 2 * S_HEADS, d), F32)], axis=0).astype(BF16)
    w_dt = w_dt_t.T
    tn = INPROJ_TN
    rows_of = lambda col0, width: [col0 + j * tn for j in range(width // tn)]
    proj_rows = rows_of(IN_Z, S_INNER + S_CONV_CH) + rows_of(IN_GATES, 2 * d) + rows_of(IN_Q, d)
    gate_tiles = (PROJ_GATES // tn, PROJ_Q // tn)
    lam_p = jnp.stack([lam_q1[lyr], lam_k1[lyr], lam_q2[lyr], lam_k2[lyr]], axis=0)
    wsub = w_subln[lyr].reshape(1, A_V_DIM)
    dtb_row, dtb_col = _head_rows(dt_bias[lyr])
    alog_row, alog_col = _head_rows(a_log[lyr])
    dskip = jnp.repeat(d_skip[lyr], S_HEADDIM).reshape(1, S_INNER)
    wn = w_ssd_norm[lyr].reshape(1, S_INNER)
    woa = w_o_attn[lyr].astype(BF16)
    wos = w_o_ssd[lyr].astype(BF16)
    wout = w_out[lyr].astype(BF16)
    wr = jnp.concatenate([w_router[lyr], jnp.zeros((d, LANES - N_EXPERTS), F32)], axis=1)
    gpost = g_post_mix[lyr].reshape(1, d)
    gpre = g_pre_ffn[lyr].reshape(1, d)
    cos, sin = _rope_tables(n_l)

    xc = x_prompt.reshape(t_c, d)
    xl = x_sample.reshape(t_l, d)
    h_c = _prenorm(xc, mod3, g_pre_mix[lyr], 0, t_c)
    h_l = _prenorm(xl, mod3, g_pre_mix[lyr], 1, n_l)
    proj_c, proj_l = _inproj(h_c, h_l, w_in_t, proj_rows, BF16, gate_tiles)
    k_c, k_l = _inproj(h_c, h_l, w_in_t, rows_of(IN_K, d), F32)
    v_c, v_l = _inproj(h_c, h_l, w_in_t, rows_of(IN_V, d), F32)

    def mixer(x, h, proj, k, v, row0, rows_per_cond, nbatch, n, ctx):
        dt, dt_t = _dtproj(h, w_dt, w_dt_t)
        if ctx is None:
            o_attn = _attn_ctx(lam_p, proj, k, v, wsub, nbatch, n, lam_init)
            h0 = None
        else:
            kc, vc, h0 = ctx
            o_attn = _attn_lat(lam_p, proj, k, v, kc, vc, cos, sin, wsub, nbatch, n, npast, lam_init)
        xa = _conv(proj, conv_w[lyr], conv_b[lyr], n)
        res = _ssd(xa, dt, dt_t, dtb_row, alog_row, dtb_col, alog_col, h0, nbatch, n, emit_final=ctx is None)
        yf, yb = res[0], res[1]
        x1, h2, aff = _mixout(yf, yb, xa, proj, o_attn, x, mod3, dskip, wn, woa, wos, wout, gpost, gpre, wr,
                              row0, rows_per_cond)
        fin = res[2] if ctx is None else None
        return x1, h2, aff, fin

    x1_c, h2_c, aff_c, fin_c = mixer(xc, h_c, proj_c, k_c, v_c, 0, t_c, nb_c, n_c, None)
    kc = cache_k[:, lyr].reshape(nb_l * npast, A_HEADS * 2 * A_HEAD_DIM)
    vc = cache_v[:, lyr].reshape(nb_l * npast, A_HEADS * A_V_DIM)
    h0 = state_ssd[:, lyr].reshape(nb_l, 2, S_INNER, S_STATE)
    x1_l, h2_l, aff_l, _ = mixer(xl, h_l, proj_l, k_l, v_l, 1, n_l, nb_l, n_l, (kc, vc, h0))

    cap_c = EC_FACTOR * t_c // N_EXPERTS
    cap_l = EC_FACTOR * t_l // N_EXPERTS
    route_c = _router(aff_c[:, :N_EXPERTS].T, cap_c)
    route_l = _router(aff_l[:, :N_EXPERTS].T, cap_l)
    xe_c = _dispatch(route_c, h2_c, cap_c)
    xe_l = _dispatch(route_l, h2_l, cap_l)
    ye_c, ye_l = _ffn(xe_c, xe_l, w_gate[lyr], w_up[lyr], w_down[lyr])
    out_c = _combine(route_c, aff_c, ye_c, x1_c, mod3, g_post_ffn[lyr], 0, t_c)
    out_l = _combine(route_l, aff_l, ye_l, x1_l, mod3, g_post_ffn[lyr], 1, n_l)

    y_prompt = out_c.reshape(nb_c, n_c, d)
    y_sample = out_l.reshape(nb_l, n_l, d)
    new_k = k_c.reshape(nb_c, 1, n_c, A_HEADS, 2 * A_HEAD_DIM)
    new_v = v_c.reshape(nb_c, 1, n_c, A_HEADS, A_V_DIM)
    new_state = fin_c.reshape(nb_c, 1, 2, S_HEADS, S_HEADDIM, S_STATE)
    return (y_prompt, y_sample, new_k, new_v, new_state)
```

```python
import functools
import math
from typing import NamedTuple

import jax
import jax.numpy as jnp
import numpy as np
from jax import lax
from jax.experimental import pallas as pl
from jax.experimental.pallas import tpu as pltpu

F32 = jnp.float32
BF16 = jnp.bfloat16
I32 = jnp.int32

D_MODEL = 1024
GRID_W = 64
EPS = 1e-6
N_MOD = 6
A_HEADS = 8
A_HEAD_DIM = 64
A_V_DIM = 128
ROPE_BASE = 10000.0
S_INNER = 2048
S_HEADDIM = 64
S_HEADS = 32
S_GROUPS = 4
S_GROUP_W = S_INNER // S_GROUPS
S_STATE = 128
S_CHUNK = 128
S_CONV_CH = S_INNER + 2 * S_GROUPS * S_STATE
N_EXPERTS = 16
EC_FACTOR = 2
D_FF = 2048
LANES = 128
IN_Q, IN_K, IN_V, IN_Z, IN_XBC = 0, 1024, 2048, 3072, 5120
IN_DT = IN_XBC + S_CONV_CH
IN_GATES = IN_DT + 2 * S_HEADS
PROJ_Z, PROJ_XBC, PROJ_GATES, PROJ_Q = 0, 2048, 5120, 7168
VMEM_LIMIT = 56 * 1024 * 1024
LOG2E = 1.4426950408889634


def _cparams(sem, vmem=None):
    return pltpu.CompilerParams(dimension_semantics=sem, vmem_limit_bytes=vmem)


def _dot(a, b):
    return jnp.dot(a, b, preferred_element_type=F32)


def _dot_nt(a, b):
    return lax.dot_general(a, b, (((1,), (1,)), ((), ())), preferred_element_type=F32)


def _split3(a):
    a1 = a.astype(BF16)
    r1 = a - a1.astype(F32)
    a2 = r1.astype(BF16)
    a3 = (r1 - a2.astype(F32)).astype(BF16)
    return a1, a2, a3


def _dot_f32_lhs(a, m):
    a1, a2, a3 = _split3(a)
    return _dot(a1, m) + (_dot(a2, m) + _dot(a3, m))


def _dot_f32_rhs(m, b):
    b1, b2, b3 = _split3(b)
    return _dot(m, b1) + (_dot(m, b2) + _dot(m, b3))


def _dot_hi(a, b):
    a1, a2, _ = _split3(a)
    b1, b2, _ = _split3(b)
    return _dot(a1, b1) + (_dot(a1, b2) + _dot(a2, b1))


def _silu(x):
    return x * jax.nn.sigmoid(x)


def _rms(x, w):
    return x * lax.rsqrt(jnp.mean(x * x, axis=-1, keepdims=True) + EPS) * w


def _ada_kernel(c_ref, w_ref, b_ref, o_ref):
    o_ref[...] = _dot_hi(_silu(c_ref[...]), w_ref[...]) + b_ref[...]


def _ada(cond8, w_ada, b_ada):
    n = w_ada.shape[1]
    tn = 1536
    return pl.pallas_call(
        _ada_kernel,
        grid=(n // tn,),
        in_specs=[pl.BlockSpec((8, D_MODEL), lambda j: (0, 0)),
                  pl.BlockSpec((D_MODEL, tn), lambda j: (0, j)),
                  pl.BlockSpec((1, tn), lambda j: (0, j))],
        out_specs=pl.BlockSpec((8, tn), lambda j: (0, j)),
        out_shape=jax.ShapeDtypeStruct((8, n), F32),
        compiler_params=_cparams(("arbitrary",), VMEM_LIMIT),
        name="ada",
    )(cond8, w_ada, b_ada.reshape(1, n))


def _mod_spec(col, tm, row0, rows_per_cond):
    return pl.BlockSpec((1, 1, D_MODEL), lambda i, *_: (row0 + (i * tm) // rows_per_cond, 0, col))


def _prenorm_kernel(x_ref, sh_ref, sc_ref, g_ref, w_ref, wt_ref, o_ref, dt_ref, dtt_ref):
    xn = _rms(x_ref[...], g_ref[...])
    h = (xn * (1.0 + sc_ref[0]) + sh_ref[0]).astype(BF16)
    o_ref[...] = h
    dt_ref[...] = _dot(h, w_ref[...])
    dtt_ref[...] = _dot_nt(wt_ref[...], h)


def _prenorm(x, mod3, gain, w_dt, w_dt_t, row0, rows_per_cond):
    t = x.shape[0]
    tm = 512
    return pl.pallas_call(
        _prenorm_kernel,
        grid=(t // tm,),
        in_specs=[pl.BlockSpec((tm, D_MODEL), lambda i: (i, 0)),
                  _mod_spec(0, tm, row0, rows_per_cond),
                  _mod_spec(1, tm, row0, rows_per_cond),
                  pl.BlockSpec((1, D_MODEL), lambda i: (0, 0)),
                  pl.BlockSpec((D_MODEL, LANES), lambda i: (0, 0)),
                  pl.BlockSpec((LANES, D_MODEL), lambda i: (0, 0))],
        out_specs=[pl.BlockSpec((tm, D_MODEL), lambda i: (i, 0)),
                   pl.BlockSpec((tm, LANES), lambda i: (i, 0)),
                   pl.BlockSpec((LANES, tm), lambda i: (0, i))],
        out_shape=[jax.ShapeDtypeStruct((t, D_MODEL), BF16), jax.ShapeDtypeStruct((t, LANES), F32),
                   jax.ShapeDtypeStruct((LANES, t), F32)],
        compiler_params=_cparams(("parallel",)),
        name="prenorm",
    )(x, mod3, mod3, gain.reshape(1, D_MODEL), w_dt, w_dt_t)


INPROJ_TM = 2048
INPROJ_TN = 512


def _inproj_kernel(blk_ref, nxt_ref, hc_ref, hl_ref, w_ref, wn_ref, oc_ref, ol_ref, wbf_ref,
                   *, sh_lo, sh_hi, shift, n_ctx_tiles):
    del blk_ref, nxt_ref
    j = pl.program_id(0)
    i = pl.program_id(1)
    shifted = jnp.logical_and(j >= sh_lo, j < sh_hi)

    if shift:
        @pl.when(jnp.logical_and(i == 0, shifted))
        def _():
            w = jnp.concatenate([w_ref[shift:, :], wn_ref[:shift, :]], axis=0)
            wbf_ref[...] = w.T.astype(BF16)

    @pl.when(jnp.logical_and(i == 0, jnp.logical_not(shifted)))
    def _():
        wbf_ref[...] = w_ref[...].T.astype(BF16)

    @pl.when(i < n_ctx_tiles)
    def _():
        tm = oc_ref.shape[0]
        rows = pl.ds(pl.multiple_of(jnp.minimum(i, n_ctx_tiles - 1) * tm, tm), tm)
        oc_ref[...] = _dot(hc_ref[rows, :], wbf_ref[...]).astype(oc_ref.dtype)

    @pl.when(i >= n_ctx_tiles)
    def _():
        ol_ref[...] = _dot(hl_ref[...], wbf_ref[...]).astype(ol_ref.dtype)


def _inproj(h_c, h_l, w_t, w_rows, out_dtype, shifted_tiles=(0, 0)):
    tm, tn = INPROJ_TM, INPROJ_TN
    t_c, k = h_c.shape
    t_l = h_l.shape[0]
    assert t_c % tm == 0 and t_l == tm
    nct = t_c // tm
    ncol = len(w_rows)
    sh_lo, sh_hi = shifted_tiles
    shifts = {w_rows[j] % tn for j in range(sh_lo, sh_hi)}
    assert len(shifts) <= 1 and all(w_rows[j] % tn == 0 for j in range(ncol) if not sh_lo <= j < sh_hi)
    shift = shifts.pop() if shifts else 0
    assert shift % 8 == 0
    blocks = [r // tn for r in w_rows]
    nxt = [b + 1 if sh_lo <= j < sh_hi else blocks[sh_lo] if sh_hi > sh_lo else 0 for j, b in enumerate(blocks)]
    return pl.pallas_call(
        functools.partial(_inproj_kernel, sh_lo=sh_lo, sh_hi=sh_hi, shift=shift, n_ctx_tiles=nct),
        grid_spec=pltpu.PrefetchScalarGridSpec(
            num_scalar_prefetch=2,
            grid=(ncol, nct + 1),
            in_specs=[pl.BlockSpec((t_c, k), lambda j, i, blk, nx: (0, 0), pipeline_mode=pl.Buffered(1)),
                      pl.BlockSpec((t_l, k), lambda j, i, blk, nx: (0, 0), pipeline_mode=pl.Buffered(1)),
                      pl.BlockSpec((tn, k), lambda j, i, blk, nx: (blk[j], 0)),
                      pl.BlockSpec((tn, k), lambda j, i, blk, nx: (nx[j], 0))],
            out_specs=[pl.BlockSpec((tm, tn), lambda j, i, blk, nx: (jnp.minimum(i, nct - 1), j)),
                       pl.BlockSpec((tm, tn), lambda j, i, blk, nx: (0, j))],
            scratch_shapes=[pltpu.VMEM((k, tn), BF16)]),
        out_shape=[jax.ShapeDtypeStruct((t_c, ncol * tn), out_dtype),
                   jax.ShapeDtypeStruct((t_l, ncol * tn), out_dtype)],
        compiler_params=_cparams(("arbitrary", "arbitrary"), VMEM_LIMIT),
        name="inproj",
    )(jnp.asarray(blocks, I32), jnp.asarray(nxt, I32), h_c, h_l, w_t, w_t)


def _lam(lp_ref, lam_init):
    lp = lp_ref[...]
    s1 = jnp.sum(lp[0:1] * lp[1:2], axis=-1, keepdims=True)
    s2 = jnp.sum(lp[2:3] * lp[3:4], axis=-1, keepdims=True)
    return jnp.exp(s1) - jnp.exp(s2) + lam_init


def _with_ones(v):
    return jnp.concatenate([v, jnp.ones(v.shape, v.dtype)], axis=1)


def _diff_attn(q, k, v1, lam, wsub, lam_init):
    nq = q.shape[0]
    lane = lax.broadcasted_iota(I32, q.shape, 1)
    q = q * (A_HEAD_DIM ** -0.5 * LOG2E)
    q1 = jnp.where(lane < A_HEAD_DIM, q, 0.0).astype(BF16)
    q2 = jnp.where(lane >= A_HEAD_DIM, q, 0.0).astype(BF16)
    s = _dot_nt(jnp.concatenate([q1, q2], axis=0), k)
    p = jnp.exp2(s - jnp.max(s, axis=-1, keepdims=True)).astype(BF16)
    pv = _dot(p, v1)
    pv = pv[:, :A_V_DIM] / pv[:, A_V_DIM:]
    o = pv[:nq] - lam * pv[nq:]
    return _rms(o, wsub) * (1.0 - lam_init)


def _attn_ctx_kernel(lp_ref, q_ref, k_ref, v_ref, ws_ref, o_ref, *, lam_init):
    lam = _lam(lp_ref, lam_init)
    for h in range(A_HEADS):
        sl = slice(h * LANES, (h + 1) * LANES)
        o = _diff_attn(q_ref[:, sl].astype(F32), k_ref[:, sl].astype(BF16), _with_ones(v_ref[:, sl].astype(BF16)),
                       lam, ws_ref[...], lam_init)
        o_ref[:, sl] = o.astype(o_ref.dtype)


def _attn_ctx(lam_p, proj, k, v, wsub, nbatch, n, lam_init):
    w = A_HEADS * LANES
    blk = pl.BlockSpec((n, w), lambda b: (b, 0))
    return pl.pallas_call(
        functools.partial(_attn_ctx_kernel, lam_init=lam_init),
        grid=(nbatch,),
        in_specs=[pl.BlockSpec((4, A_HEAD_DIM), lambda b: (0, 0)),
                  pl.BlockSpec((n, w), lambda b: (b, PROJ_Q // w)), blk, blk,
                  pl.BlockSpec((1, A_V_DIM), lambda b: (0, 0))],
        out_specs=blk,
        out_shape=jax.ShapeDtypeStruct(k.shape, BF16),
        compiler_params=_cparams(("parallel",), VMEM_LIMIT),
        name="attn_ctx",
    )(lam_p, proj, k, v, wsub)


def _rope(x, cos, sin_signed):
    lane = lax.broadcasted_iota(I32, x.shape, 1)
    quarter = A_HEAD_DIM // 4
    partner = jnp.where((lane % (2 * quarter)) < quarter,
                        pltpu.roll(x, LANES - quarter, axis=1),
                        pltpu.roll(x, quarter, axis=1))
    return x * cos + partner * sin_signed


def _attn_lat_kernel(lp_ref, q_ref, k_ref, v_ref, kc_ref, vc_ref, cos_ref, sin_ref, ws_ref, o_ref, *, lam_init, tq):
    lam = _lam(lp_ref, lam_init)
    k = _rope(k_ref[...], cos_ref[...], sin_ref[...])
    k_all = jnp.concatenate([kc_ref[...].astype(BF16), k.astype(BF16)], axis=0)
    v_all = _with_ones(jnp.concatenate([vc_ref[...].astype(BF16), v_ref[...].astype(BF16)], axis=0))
    for i in range(q_ref.shape[0] // tq):
        rows = slice(i * tq, (i + 1) * tq)
        q = _rope(q_ref[rows, :].astype(F32), cos_ref[rows, :], sin_ref[rows, :])
        o_ref[rows, :] = _diff_attn(q, k_all, v_all, lam, ws_ref[...], lam_init).astype(o_ref.dtype)


def _attn_lat(lam_p, proj, k, v, kc, vc, cos, sin, wsub, nbatch, n, npast, lam_init):
    blk = pl.BlockSpec((n, LANES), lambda b, h: (b, h))
    cblk = pl.BlockSpec((npast, LANES), lambda b, h: (b, h))
    tab = pl.BlockSpec((n, LANES), lambda b, h: (0, 0))
    return pl.pallas_call(
        functools.partial(_attn_lat_kernel, lam_init=lam_init, tq=256),
        grid=(nbatch, A_HEADS),
        in_specs=[pl.BlockSpec((4, A_HEAD_DIM), lambda b, h: (0, 0)),
                  pl.BlockSpec((n, LANES), lambda b, h: (b, PROJ_Q // LANES + h)), blk, blk, cblk, cblk, tab, tab,
                  pl.BlockSpec((1, A_V_DIM), lambda b, h: (0, 0))],
        out_specs=blk,
        out_shape=jax.ShapeDtypeStruct(k.shape, BF16),
        compiler_params=_cparams(("parallel", "parallel"), VMEM_LIMIT),
        name="attn_lat",
    )(lam_p, proj, k, v, kc, vc, cos, sin, wsub)


def _rope_tables(n):
    pos = np.arange(n)
    rowcol = np.stack([pos // GRID_W, pos % GRID_W], axis=-1).astype(np.float64)
    nf = A_HEAD_DIM // 4
    inv_freq = np.power(ROPE_BASE, -np.arange(nf, dtype=np.float64) / nf)
    lane = np.arange(LANES)
    axis = (lane % A_HEAD_DIM) // (2 * nf)
    ang = rowcol[:, axis] * inv_freq[lane % nf][None, :]
    sign = np.where((lane % (2 * nf)) < nf, -1.0, 1.0)
    return jnp.asarray(np.cos(ang), F32), jnp.asarray(np.sin(ang) * sign[None, :], F32)


def _conv_kernel(x_ref, w_ref, b_ref, o_ref, *, n):
    x = x_ref[...].astype(F32)
    rows = x.shape[0]
    pos = lax.broadcasted_iota(I32, x.shape, 0) % n
    prev = jnp.where(pos == 0, 0.0, pltpu.roll(x, 1, axis=0))
    nxt = jnp.where(pos == n - 1, 0.0, pltpu.roll(x, rows - 1, axis=0))
    w = w_ref[...]
    o_ref[...] = _silu(prev * w[0:1] + x * w[1:2] + nxt * w[2:3] + b_ref[...]).astype(o_ref.dtype)


def _conv(proj, conv_w, conv_b, n):
    tc = 1024
    tr = 1024
    c = S_CONV_CH
    t = proj.shape[0]
    assert tr % n == 0 and t % tr == 0
    return pl.pallas_call(
        functools.partial(_conv_kernel, n=n),
        grid=(t // tr, c // tc),
        in_specs=[pl.BlockSpec((tr, tc), lambda b, j: (b, PROJ_XBC // tc + j)),
                  pl.BlockSpec((3, tc), lambda b, j: (0, j)),
                  pl.BlockSpec((1, tc), lambda b, j: (0, j))],
        out_specs=pl.BlockSpec((tr, tc), lambda b, j: (b, j)),
        out_shape=jax.ShapeDtypeStruct((t, c), BF16),
        compiler_params=_cparams(("parallel", "parallel"), VMEM_LIMIT),
        name="conv",
    )(proj, conv_w, conv_b.reshape(1, c))


S_PAIRS = S_INNER // LANES


def _ssd_direction(xa_ref, dt_raw, dt_raw_t, dtb_row, alog_row, dtb_col, alog_col, ht_ref, y_ref, backward):
    col0 = S_HEADS if backward else 0
    row = lax.broadcasted_iota(I32, (S_CHUNK, S_CHUNK), 0)
    col = lax.broadcasted_iota(I32, (S_CHUNK, S_CHUNK), 1)
    upper = jnp.where(row <= col, 1.0, 0.0).astype(BF16)
    lower = jnp.where(row >= col, 1.0, 0.0).astype(BF16)
    dt = jax.nn.softplus(dt_raw + dtb_row)
    dt_t = jax.nn.softplus(dt_raw_t + dtb_col)
    da = dt * (-jnp.exp(alog_row) * LOG2E)
    da_t = dt_t * (-jnp.exp(alog_col) * LOG2E)
    if backward:
        cum = _dot_f32_rhs(upper, da)
        cum_t = _dot_f32_lhs(da_t, lower)
        last = 0
        keep = row <= col
    else:
        cum = _dot_f32_rhs(lower, da)
        cum_t = _dot_f32_lhs(da_t, upper)
        last = S_CHUNK - 1
        keep = row >= col
    w_t = dt_t * jnp.exp2(cum_t[:, last:last + 1] - cum_t)
    e_tot = jnp.exp2(cum[last:last + 1, :])
    lo_half = col < S_HEADDIM
    lo_bf = jnp.where(lo_half, 1.0, 0.0).astype(BF16)
    hi_bf = jnp.where(lo_half, 0.0, 1.0).astype(BF16)
    for g in range(S_GROUPS):
        b_bf = xa_ref[:, S_INNER + g * S_STATE:S_INNER + (g + 1) * S_STATE]
        c_bf = xa_ref[:, S_INNER + (S_GROUPS + g) * S_STATE:S_INNER + (S_GROUPS + g + 1) * S_STATE]
        cb = _dot_nt(c_bf, b_bf)
        bt = b_bf.astype(F32).T
        c_f = c_bf.astype(F32)
        for pr in range(S_PAIRS // S_GROUPS):
            hp = g * (S_PAIRS // S_GROUPS) + pr
            c0 = col0 + 2 * hp
            x_pair = xa_ref[:, hp * LANES:(hp + 1) * LANES]
            bd_x = jnp.concatenate([x_pair * lo_bf, x_pair * hi_bf], axis=0)
            ht = ht_ref[hp]
            bd_h = jnp.concatenate([jnp.where(lo_half, ht, 0.0), jnp.where(lo_half, 0.0, ht)], axis=0).astype(BF16)
            ms, ces, bws = [], [], []
            for cc in (c0, c0 + 1):
                bc = jnp.broadcast_to(cum[:, cc:cc + 1], (S_CHUNK, S_CHUNK))
                decay = jnp.exp2(jnp.where(keep, bc - cum_t[cc:cc + 1, :], -1e30))
                ms.append((cb * decay * dt_t[cc:cc + 1, :]).astype(BF16))
                ces.append((c_f * jnp.exp2(bc)).astype(BF16))
                bws.append((bt * w_t[cc:cc + 1, :]).astype(BF16))
            y_ref[:, hp * LANES:(hp + 1) * LANES] = _dot(jnp.concatenate(ms + ces, axis=1),
                                                         jnp.concatenate([bd_x, bd_h], axis=0)).astype(y_ref.dtype)
            tot = jnp.where(lo_half[0:1], e_tot[:, c0:c0 + 1], e_tot[:, c0 + 1:c0 + 2])
            ht_ref[hp] = ht * tot + _dot(jnp.concatenate(bws, axis=1), bd_x)


def _ssd_kernel(*refs, has_h0, emit_final):
    xf_ref, xb_ref, dtf_ref, dtb_ref, dtft_ref, dtbt_ref, br_ref, ar_ref, bc_ref, ac_ref = refs[:10]
    refs = refs[10:]
    if has_h0:
        h0_ref, refs = refs[0], refs[1:]
    yf_ref, yb_ref = refs[:2]
    refs = refs[2:]
    if emit_final:
        fin_ref, refs = refs[0], refs[1:]
    htf_ref, htb_ref = refs
    c = pl.program_id(1)
    nc = pl.num_programs(1)

    @pl.when(c == 0)
    def _():
        for d, ht_ref in enumerate((htf_ref, htb_ref)):
            for hp in range(S_PAIRS):
                if has_h0:
                    ht_ref[hp] = h0_ref[0, d, hp * LANES:(hp + 1) * LANES, :].T
                else:
                    ht_ref[hp] = jnp.zeros((S_STATE, LANES), F32)

    _ssd_direction(xf_ref, dtf_ref[...], dtft_ref[...], br_ref[...], ar_ref[...], bc_ref[...], ac_ref[...],
                   htf_ref, yf_ref, backward=False)
    _ssd_direction(xb_ref, dtb_ref[...], dtbt_ref[...], br_ref[...], ar_ref[...], bc_ref[...], ac_ref[...],
                   htb_ref, yb_ref, backward=True)

    if emit_final:
        @pl.when(c == nc - 1)
        def _():
            for d, ht_ref in enumerate((htf_ref, htb_ref)):
                for hp in range(S_PAIRS):
                    fin_ref[0, d, hp * LANES:(hp + 1) * LANES, :] = ht_ref[hp].T


def _ssd(xa, dt, dt_t, dtb_row, alog_row, dtb_col, alog_col, h0, nbatch, n, emit_final):
    nc = n // S_CHUNK
    t = xa.shape[0]
    fwd = lambda b, c: (b * nc + c, 0)
    bwd = lambda b, c: (b * nc + nc - 1 - c, 0)
    const = lambda b, c: (0, 0)
    in_specs = [pl.BlockSpec((S_CHUNK, S_CONV_CH), fwd),
                pl.BlockSpec((S_CHUNK, S_CONV_CH), bwd),
                pl.BlockSpec((S_CHUNK, LANES), fwd),
                pl.BlockSpec((S_CHUNK, LANES), bwd),
                pl.BlockSpec((LANES, S_CHUNK), lambda b, c: (0, b * nc + c)),
                pl.BlockSpec((LANES, S_CHUNK), lambda b, c: (0, b * nc + nc - 1 - c)),
                pl.BlockSpec((1, LANES), const), pl.BlockSpec((1, LANES), const),
                pl.BlockSpec((LANES, 1), const), pl.BlockSpec((LANES, 1), const)]
    args = [xa, xa, dt, dt, dt_t, dt_t, dtb_row, alog_row, dtb_col, alog_col]
    if h0 is not None:
        in_specs.append(pl.BlockSpec((1, 2, S_INNER, S_STATE), lambda b, c: (b, 0, 0, 0)))
        args.append(h0)
    out_specs = [pl.BlockSpec((S_CHUNK, S_INNER), fwd), pl.BlockSpec((S_CHUNK, S_INNER), bwd)]
    out_shape = [jax.ShapeDtypeStruct((t, S_INNER), BF16), jax.ShapeDtypeStruct((t, S_INNER), BF16)]
    if emit_final:
        out_specs.append(pl.BlockSpec((1, 2, S_INNER, S_STATE), lambda b, c: (b, 0, 0, 0)))
        out_shape.append(jax.ShapeDtypeStruct((nbatch, 2, S_INNER, S_STATE), F32))
    return pl.pallas_call(
        functools.partial(_ssd_kernel, has_h0=h0 is not None, emit_final=emit_final),
        grid=(nbatch, nc),
        in_specs=in_specs,
        out_specs=out_specs,
        out_shape=out_shape,
        scratch_shapes=[pltpu.VMEM((S_PAIRS, S_STATE, LANES), F32),
                        pltpu.VMEM((S_PAIRS, S_STATE, LANES), F32)],
        compiler_params=_cparams(("parallel", "arbitrary"), VMEM_LIMIT),
        name="ssd",
    )(*args)


def _mixout_kernel(yf_ref, yb_ref, xs_ref, z_ref, oa_ref, ga_ref, gs_ref, x_ref, g1_ref, sh2_ref, sc2_ref,
                   dskip_ref, wn_ref, woa_ref, wos_ref, wout_ref, gpost_ref, gpre_ref, wr_ref,
                   x1_ref, h2_ref, aff_ref):
    wn = wn_ref[...]
    half = x_ref.shape[0] // 2
    halves = [slice(r * half, (r + 1) * half) for r in range(2)]

    def ssd_post(rows):
        y = (yf_ref[rows, :].astype(F32) + yb_ref[rows, :].astype(F32)
             + xs_ref[rows, :].astype(F32) * dskip_ref[...])
        y = y * _silu(z_ref[rows, :].astype(F32))
        parts = []
        for g in range(S_GROUPS):
            sl = slice(g * S_GROUP_W, (g + 1) * S_GROUP_W)
            parts.append(_rms(y[:, sl], wn[:, sl]).astype(BF16))
        return jnp.concatenate(parts, axis=1)

    o_ssds = [ssd_post(rows) for rows in halves]
    for rows, o_ssd in zip(halves, o_ssds):
        a = _dot(oa_ref[rows, :], woa_ref[...])
        s = _dot(o_ssd, wos_ref[...])
        merged = (jax.nn.sigmoid(ga_ref[rows, :].astype(F32)) * a
                  + jax.nn.sigmoid(gs_ref[rows, :].astype(F32)) * s)
        ym = _dot(merged.astype(BF16), wout_ref[...])
        x1 = x_ref[rows, :] + g1_ref[0] * _rms(ym, gpost_ref[...])
        x1_ref[rows, :] = x1
        h2 = _rms(x1, gpre_ref[...]) * (1.0 + sc2_ref[0]) + sh2_ref[0]
        h2_ref[rows, :] = h2.astype(BF16)
        logits = _dot_hi(h2, wr_ref[...])
        lane = lax.broadcasted_iota(I32, logits.shape, 1)
        logits = jnp.where(lane < N_EXPERTS, logits, -1e30)
        p = jnp.exp(logits - jnp.max(logits, axis=-1, keepdims=True))
        aff_ref[rows, :] = p / jnp.sum(p, axis=-1, keepdims=True)


def _mixout(yf, yb, xa, proj, o_attn, x, mod3, dskip, wn, woa, wos, wout, gpost, gpre, wr, row0, rows_per_cond):
    t = x.shape[0]
    tm = 512
    tok = lambda w: pl.BlockSpec((tm, w), lambda i: (i, 0))
    const = lambda a: pl.BlockSpec(a.shape, lambda i: (0,) * a.ndim, pipeline_mode=pl.Buffered(1))
    return pl.pallas_call(
        _mixout_kernel,
        grid=(t // tm,),
        in_specs=[tok(S_INNER), tok(S_INNER), tok(S_INNER),
                  pl.BlockSpec((tm, S_INNER), lambda i: (i, PROJ_Z // S_INNER)), tok(D_MODEL),
                  pl.BlockSpec((tm, D_MODEL), lambda i: (i, PROJ_GATES // D_MODEL)),
                  pl.BlockSpec((tm, D_MODEL), lambda i: (i, PROJ_GATES // D_MODEL + 1)),
                  tok(D_MODEL),
                  _mod_spec(2, tm, row0, rows_per_cond),
                  _mod_spec(3, tm, row0, rows_per_cond),
                  _mod_spec(4, tm, row0, rows_per_cond),
                  const(dskip), const(wn), const(woa), const(wos), const(wout), const(gpost), const(gpre),
                  const(wr)],
        out_specs=[tok(D_MODEL), tok(D_MODEL), tok(LANES)],
        out_shape=[jax.ShapeDtypeStruct((t, D_MODEL), F32), jax.ShapeDtypeStruct((t, D_MODEL), BF16),
                   jax.ShapeDtypeStruct((t, LANES), F32)],
        compiler_params=_cparams(("parallel",), VMEM_LIMIT),
        name="mixout",
    )(yf, yb, xa, proj, o_attn, proj, proj, x, mod3, mod3, mod3, dskip, wn, woa, wos, wout, gpost, gpre, wr)


COMB_TILE = 256
COMB_WIN = 64
BF16_ROWS = 16


def _router_kernel(aff_ref, pos_ref, cnt_ref, *, cap):
    aff = aff_ref[...]
    e, t = aff.shape
    idx = lax.broadcasted_iota(I32, (e, t), 1)

    def count(m):
        return jnp.sum(jnp.where(m, 1.0, 0.0), axis=-1, keepdims=True)

    def value_bit(i, thr):
        cand = thr | jnp.left_shift(jnp.int32(1), 30 - i)
        return jnp.where(count(aff >= pltpu.bitcast(cand, F32)) >= cap, cand, thr)

    thr = pltpu.bitcast(lax.fori_loop(0, 31, value_bit, jnp.zeros((e, 1), I32)), F32)
    gt = aff > thr
    eq = aff == thr
    need = cap - count(gt)
    nbits = (t - 1).bit_length()

    def index_bit(i, v):
        cand = v | jnp.left_shift(jnp.int32(1), nbits - 1 - i)
        return jnp.where(count(jnp.logical_and(eq, idx < cand)) < need, cand, v)

    last = lax.fori_loop(0, nbits, index_bit, jnp.zeros((e, 1), I32))
    sel = jnp.logical_or(gt, jnp.logical_and(eq, idx <= last))
    sel_bf = jnp.where(sel, 1.0, 0.0).astype(BF16)
    blk = 512
    r = lax.broadcasted_iota(I32, (blk, blk), 0)
    c = lax.broadcasted_iota(I32, (blk, blk), 1)
    upper = jnp.where(r <= c, 1.0, 0.0).astype(BF16)
    carry = jnp.zeros((e, 1), F32)
    tile_cnt = []
    for b in range(t // blk):
        cum = _dot(sel_bf[:, b * blk:(b + 1) * blk], upper) + carry
        carry = cum[:, blk - 1:blk]
        slot = jnp.where(sel[:, b * blk:(b + 1) * blk], cum - 1.0, -1.0)
        pos_ref[:, b * blk:(b + 1) * blk] = slot.astype(I32)
        for j in range(blk // COMB_TILE):
            tile_cnt.append(cum[:, (j + 1) * COMB_TILE - 1:(j + 1) * COMB_TILE])
    lane = lax.broadcasted_iota(I32, (e, LANES), 1)
    cnt = jnp.zeros((e, LANES), F32)
    for i, v in enumerate(tile_cnt):
        cnt = jnp.where(lane == i, v, cnt)
    cnt_ref[...] = cnt.astype(I32)


class _Route(NamedTuple):
    pos: jax.Array
    start: jax.Array
    end: jax.Array
    win: jax.Array
    fits: jax.Array


def _router(aff_t, cap):
    e, t = aff_t.shape
    ntile = t // COMB_TILE
    pos, cnt = pl.pallas_call(
        functools.partial(_router_kernel, cap=cap),
        out_shape=[jax.ShapeDtypeStruct((e, t), I32), jax.ShapeDtypeStruct((e, LANES), I32)],
        name="router",
    )(aff_t)
    end = cnt[:, :ntile]
    start = jnp.concatenate([jnp.zeros((e, 1), I32), end[:, :-1]], axis=1)
    win = jnp.minimum(start // BF16_ROWS * BF16_ROWS, cap - COMB_WIN)
    fits = jnp.all(end - win <= COMB_WIN, axis=0).astype(I32)
    flat = lambda a: a.T.reshape(ntile * e)
    return _Route(pos, flat(start), flat(end), flat(win), fits)


def _dispatch_kernel(start_ref, end_ref, win_ref, fits_ref, pos_ref, h_ref, o_ref):
    i = pl.program_id(0)
    ne, cap, _ = o_ref.shape
    tm = h_ref.shape[0]

    @pl.when(i == 0)
    def _():
        o_ref[...] = jnp.zeros(o_ref.shape, o_ref.dtype)

    h = h_ref[...]
    row = lax.broadcasted_iota(I32, (COMB_WIN, tm), 0)
    row_l = lax.broadcasted_iota(I32, (COMB_WIN, LANES), 0)

    def place(e, r0, rows):
        rel = r0 - start_ref[i * ne + e] + row_l
        mine = jnp.logical_and(rel >= 0, rel < end_ref[i * ne + e] - start_ref[i * ne + e])
        for c in range(D_MODEL // LANES):
            sl = slice(c * LANES, (c + 1) * LANES)
            old = o_ref[e, pl.ds(r0, COMB_WIN), sl].astype(F32)
            o_ref[e, pl.ds(r0, COMB_WIN), sl] = jnp.where(mine, rows[:, sl], old).astype(o_ref.dtype)

    @pl.when(fits_ref[i] == 1)
    def _():
        sel = [jnp.where(pos_ref[e:e + 1, :] == win_ref[i * ne + e] + row, 1.0, 0.0).astype(BF16) for e in range(ne)]
        rows = _dot(jnp.concatenate(sel, axis=0), h)
        for e in range(ne):
            place(e, pl.multiple_of(win_ref[i * ne + e], BF16_ROWS), rows[e * COMB_WIN:(e + 1) * COMB_WIN])

    @pl.when(fits_ref[i] == 0)
    def _():
        for e in range(ne):
            w0 = win_ref[i * ne + e]

            def window(w, carry):
                r0 = pl.multiple_of(jnp.minimum(w0 + w * COMB_WIN, cap - COMB_WIN), BF16_ROWS)
                sel = jnp.where(pos_ref[e:e + 1, :] == r0 + row, 1.0, 0.0).astype(BF16)
                place(e, r0, _dot(sel, h))
                return carry

            lax.fori_loop(0, (end_ref[i * ne + e] - w0 + COMB_WIN - 1) // COMB_WIN, window, 0)


def _dispatch(route, h2, cap):
    e, t = route.pos.shape
    tm = COMB_TILE
    return pl.pallas_call(
        _dispatch_kernel,
        grid_spec=pltpu.PrefetchScalarGridSpec(
            num_scalar_prefetch=4,
            grid=(t // tm,),
            in_specs=[pl.BlockSpec((e, tm), lambda i, *_: (0, i)),
                      pl.BlockSpec((tm, D_MODEL), lambda i, *_: (i, 0))],
            out_specs=pl.BlockSpec((e, cap, D_MODEL), lambda i, *_: (0, 0, 0))),
        out_shape=jax.ShapeDtypeStruct((e, cap, D_MODEL), BF16),
        compiler_params=_cparams(("arbitrary",), VMEM_LIMIT),
        name="dispatch",
    )(route.start, route.end, route.win, route.fits, route.pos, h2)


def _ffn_kernel(xc_ref, xl_ref, wg_ref, wu_ref, wd_ref, oc_ref, ol_ref, acc_ref):
    f = pl.program_id(1)
    cc = xc_ref.shape[1]
    xe = jnp.concatenate([xc_ref[0], xl_ref[0]], axis=0)
    hg = _dot(xe, wg_ref[0].astype(BF16))
    hu = _dot(xe, wu_ref[0].astype(BF16))
    part = _dot((_silu(hg) * hu).astype(BF16), wd_ref[0].astype(BF16))

    @pl.when(f == 0)
    def _():
        acc_ref[...] = part

    @pl.when(f > 0)
    def _():
        acc_ref[...] += part

    @pl.when(f == pl.num_programs(1) - 1)
    def _():
        oc_ref[0] = acc_ref[:cc, :].astype(BF16)
        ol_ref[0] = acc_ref[cc:, :].astype(BF16)


def _ffn(xe_c, xe_l, w_gate, w_up, w_down):
    e, cc, d = xe_c.shape
    cl = xe_l.shape[1]
    tf = 1024
    return pl.pallas_call(
        _ffn_kernel,
        grid=(e, D_FF // tf),
        in_specs=[pl.BlockSpec((1, cc, d), lambda i, f: (i, 0, 0)),
                  pl.BlockSpec((1, cl, d), lambda i, f: (i, 0, 0)),
                  pl.BlockSpec((1, d, tf), lambda i, f: (i, 0, f)),
                  pl.BlockSpec((1, d, tf), lambda i, f: (i, 0, f)),
                  pl.BlockSpec((1, tf, d), lambda i, f: (i, f, 0))],
        out_specs=[pl.BlockSpec((1, cc, d), lambda i, f: (i, 0, 0)),
                   pl.BlockSpec((1, cl, d), lambda i, f: (i, 0, 0))],
        out_shape=[jax.ShapeDtypeStruct((e, cc, d), BF16), jax.ShapeDtypeStruct((e, cl, d), BF16)],
        scratch_shapes=[pltpu.VMEM((cc + cl, d), F32)],
        compiler_params=_cparams(("parallel", "arbitrary"), VMEM_LIMIT),
        name="ffn",
    )(xe_c, xe_l, w_gate, w_up, w_down)


def _combine_kernel(win_ref, fits_ref, pos_ref, aff_ref, ye_ref, x1_ref, g2_ref, gp_ref, o_ref, rhs_ref, y_ref):
    i = pl.program_id(0)
    tm = pos_ref.shape[0]
    cap = ye_ref.shape[1]
    pos = pos_ref[...]
    aff = aff_ref[...]

    @pl.when(fits_ref[i] == 1)
    def _():
        per_col = LANES // COMB_WIN
        lane = lax.broadcasted_iota(I32, (tm, LANES), 1)
        cols = []
        for c in range(N_EXPERTS // per_col):
            col = jnp.zeros((tm, LANES), F32)
            for j in range(per_col):
                e = c * per_col + j
                w0 = pl.multiple_of(win_ref[i * N_EXPERTS + e], BF16_ROWS)
                rhs_ref[e * COMB_WIN:(e + 1) * COMB_WIN, :] = ye_ref[e, pl.ds(w0, COMB_WIN), :]
                p = pos[:, e:e + 1]
                tgt = jnp.where(p >= 0, p - w0 + j * COMB_WIN, -1)
                col = jnp.where(lane == tgt, aff[:, e:e + 1], col)
            cols.append(col)
        sel = jnp.concatenate(cols, axis=1)
        hi = sel.astype(BF16)
        lo = (sel - hi.astype(F32)).astype(BF16)
        rhs = rhs_ref[...]
        y_ref[...] = _dot(hi, rhs) + _dot(lo, rhs)

    @pl.when(fits_ref[i] == 0)
    def _():
        slot = lax.broadcasted_iota(I32, (tm, cap), 1)
        y = jnp.zeros((tm, D_MODEL), F32)
        for e in range(N_EXPERTS):
            onehot = jnp.where(pos[:, e:e + 1] == slot, 1.0, 0.0).astype(BF16)
            y = y + aff[:, e:e + 1] * _dot(onehot, ye_ref[e])
        y_ref[...] = y

    o_ref[...] = x1_ref[...] + g2_ref[0] * _rms(y_ref[...], gp_ref[...])


def _combine(route, aff, ye, x1, mod3, gpost, row0, rows_per_cond):
    t = x1.shape[0]
    pos_t, win, fits = route.pos.T, route.win, route.fits
    e, cap, d = ye.shape
    tm = COMB_TILE
    return pl.pallas_call(
        _combine_kernel,
        grid_spec=pltpu.PrefetchScalarGridSpec(
            num_scalar_prefetch=2,
            grid=(t // tm,),
            in_specs=[pl.BlockSpec((tm, e), lambda i, *_: (i, 0)),
                      pl.BlockSpec((tm, LANES), lambda i, *_: (i, 0)),
                      pl.BlockSpec((e, cap, d), lambda i, *_: (0, 0, 0)),
                      pl.BlockSpec((tm, d), lambda i, *_: (i, 0)),
                      _mod_spec(5, tm, row0, rows_per_cond),
                      pl.BlockSpec((1, d), lambda i, *_: (0, 0))],
            out_specs=pl.BlockSpec((tm, d), lambda i, *_: (i, 0)),
            scratch_shapes=[pltpu.VMEM((e * COMB_WIN, d), BF16), pltpu.VMEM((tm, d), F32)]),
        out_shape=jax.ShapeDtypeStruct((t, d), F32),
        compiler_params=_cparams(("parallel",), VMEM_LIMIT),
        name="combine",
    )(win, fits, pos_t, aff, ye, x1, mod3, gpost.reshape(1, d))


def _head_rows(v):
    flat = jnp.concatenate([v.reshape(2 * S_HEADS), jnp.zeros((LANES - 2 * S_HEADS,), F32)])
    return flat.reshape(1, LANES), flat.reshape(LANES, 1)


def kernel(x_prompt, x_sample, cache_k, cache_v, state_ssd, c, c_ctx, w_ada, b_ada, g_pre_mix, g_post_mix, g_pre_ffn, g_post_ffn, w_in, lam_q1, lam_k1, lam_q2, lam_k2, w_subln, conv_w, conv_b, dt_bias, a_log, d_skip, w_ssd_norm, w_o_attn, w_o_ssd, w_out, w_router, w_gate, w_up, w_down):
    nb_c, n_c, d = x_prompt.shape
    nb_l, n_l, _ = x_sample.shape
    npast = cache_k.shape[2]
    depth = w_in.shape[0]
    assert depth == 1 and d == D_MODEL
    t_c, t_l = nb_c * n_c, nb_l * n_l
    lyr = 0
    lam_init = 0.8 - 0.6 * math.exp(-0.3 * lyr)

    cond8 = jnp.concatenate([c_ctx[None, :], c, jnp.zeros((8 - 1 - nb_l, d), F32)], axis=0)
    mod3 = _ada(cond8, w_ada[lyr], b_ada[lyr]).reshape(8, 1, N_MOD * d)

    w_in_t = w_in[lyr].T
    w_dt_t = jnp.concatenate([w_in_t[IN_DT:IN_GATES], jnp.zeros((LANES - 2 * S_HEADS, d), F32)], axis=0).astype(BF16)
    w_dt = w_dt_t.T
    tn = INPROJ_TN
    rows_of = lambda col0, width: [col0 + j * tn for j in range(width // tn)]
    proj_rows = rows_of(IN_Z, S_INNER + S_CONV_CH) + rows_of(IN_GATES, 2 * d) + rows_of(IN_Q, d)
    gate_tiles = (PROJ_GATES // tn, PROJ_Q // tn)
    lam_p = jnp.stack([lam_q1[lyr], lam_k1[lyr], lam_q2[lyr], lam_k2[lyr]], axis=0)
    wsub = w_subln[lyr].reshape(1, A_V_DIM)
    dtb_row, dtb_col = _head_rows(dt_bias[lyr])
    alog_row, alog_col = _head_rows(a_log[lyr])
    dskip = jnp.repeat(d_skip[lyr], S_HEADDIM).reshape(1, S_INNER)
    wn = w_ssd_norm[lyr].reshape(1, S_INNER)
    woa = w_o_attn[lyr].astype(BF16)
    wos = w_o_ssd[lyr].astype(BF16)
    wout = w_out[lyr].astype(BF16)
    wr = jnp.concatenate([w_router[lyr], jnp.zeros((d, LANES - N_EXPERTS), F32)], axis=1)
    gpost = g_post_mix[lyr].reshape(1, d)
    gpre = g_pre_ffn[lyr].reshape(1, d)
    cos, sin = _rope_tables(n_l)

    xc = x_prompt.reshape(t_c, d)
    xl = x_sample.reshape(t_l, d)
    h_c, dt_c, dtt_c = _prenorm(xc, mod3, g_pre_mix[lyr], w_dt, w_dt_t, 0, t_c)
    h_l, dt_l, dtt_l = _prenorm(xl, mod3, g_pre_mix[lyr], w_dt, w_dt_t, 1, n_l)
    proj_c, proj_l = _inproj(h_c, h_l, w_in_t, proj_rows, BF16, gate_tiles)
    k_c, k_l = _inproj(h_c, h_l, w_in_t, rows_of(IN_K, d), F32)
    v_c, v_l = _inproj(h_c, h_l, w_in_t, rows_of(IN_V, d), F32)

    def mixer(x, dt, dt_t, proj, k, v, row0, rows_per_cond, nbatch, n, ctx):
        if ctx is None:
            o_attn = _attn_ctx(lam_p, proj, k, v, wsub, nbatch, n, lam_init)
            h0 = None
        else:
            kc, vc, h0 = ctx
            o_attn = _attn_lat(lam_p, proj, k, v, kc, vc, cos, sin, wsub, nbatch, n, npast, lam_init)
        xa = _conv(proj, conv_w[lyr], conv_b[lyr], n)
        res = _ssd(xa, dt, dt_t, dtb_row, alog_row, dtb_col, alog_col, h0, nbatch, n, emit_final=ctx is None)
        yf, yb = res[0], res[1]
        x1, h2, aff = _mixout(yf, yb, xa, proj, o_attn, x, mod3, dskip, wn, woa, wos, wout, gpost, gpre, wr,
                              row0, rows_per_cond)
        fin = res[2] if ctx is None else None
        return x1, h2, aff, fin

    x1_c, h2_c, aff_c, fin_c = mixer(xc, dt_c, dtt_c, proj_c, k_c, v_c, 0, t_c, nb_c, n_c, None)
    kc = cache_k[:, lyr].reshape(nb_l * npast, A_HEADS * 2 * A_HEAD_DIM)
    vc = cache_v[:, lyr].reshape(nb_l * npast, A_HEADS * A_V_DIM)
    h0 = state_ssd[:, lyr].reshape(nb_l, 2, S_INNER, S_STATE)
    x1_l, h2_l, aff_l, _ = mixer(xl, dt_l, dtt_l, proj_l, k_l, v_l, 1, n_l, nb_l, n_l, (kc, vc, h0))

    cap_c = EC_FACTOR * t_c // N_EXPERTS
    cap_l = EC_FACTOR * t_l // N_EXPERTS
    route_c = _router(aff_c[:, :N_EXPERTS].T, cap_c)
    route_l = _router(aff_l[:, :N_EXPERTS].T, cap_l)
    xe_c = _dispatch(route_c, h2_c, cap_c)
    xe_l = _dispatch(route_l, h2_l, cap_l)
    ye_c, ye_l = _ffn(xe_c, xe_l, w_gate[lyr], w_up[lyr], w_down[lyr])
    out_c = _combine(route_c, aff_c, ye_c, x1_c, mod3, g_post_ffn[lyr], 0, t_c)
    out_l = _combine(route_l, aff_l, ye_l, x1_l, mod3, g_post_ffn[lyr], 1, n_l)

    y_prompt = out_c.reshape(nb_c, n_c, d)
    y_sample = out_l.reshape(nb_l, n_l, d)
    new_k = k_c.reshape(nb_c, 1, n_c, A_HEADS, 2 * A_HEAD_DIM)
    new_v = v_c.reshape(nb_c, 1, n_c, A_HEADS, A_V_DIM)
    new_state = fin_c.reshape(nb_c, 1, 2, S_HEADS, S_HEADDIM, S_STATE)
    return (y_prompt, y_sample, new_k, new_v, new_state)
```

```python
import functools
import math
from typing import NamedTuple

import jax
import jax.numpy as jnp
import numpy as np
from jax import lax
from jax.experimental import pallas as pl
from jax.experimental.pallas import tpu as pltpu

F32 = jnp.float32
BF16 = jnp.bfloat16
I32 = jnp.int32

D_MODEL = 1024
GRID_W = 64
EPS = 1e-6
N_MOD = 6
A_HEADS = 8
A_HEAD_DIM = 64
A_V_DIM = 128
ROPE_BASE = 10000.0
S_INNER = 2048
S_HEADDIM = 64
S_HEADS = 32
S_GROUPS = 4
S_GROUP_W = S_INNER // S_GROUPS
S_STATE = 128
S_CHUNK = 128
S_CONV_CH = S_INNER + 2 * S_GROUPS * S_STATE
N_EXPERTS = 16
EC_FACTOR = 2
D_FF = 2048
LANES = 128
IN_Q, IN_K, IN_V, IN_Z, IN_XBC = 0, 1024, 2048, 3072, 5120
IN_DT = IN_XBC + S_CONV_CH
IN_GATES = IN_DT + 2 * S_HEADS
PROJ_XBC, PROJ_Q, PROJ_Z, PROJ_GATES = 0, 3072, 4096, 6144
VMEM_LIMIT = 56 * 1024 * 1024
LOG2E = 1.4426950408889634


def _cparams(sem, vmem=None):
    return pltpu.CompilerParams(dimension_semantics=sem, vmem_limit_bytes=vmem)


def _dot(a, b):
    return jnp.dot(a, b, preferred_element_type=F32)


def _dot_nt(a, b):
    return lax.dot_general(a, b, (((1,), (1,)), ((), ())), preferred_element_type=F32)


def _split3(a):
    a1 = a.astype(BF16)
    r1 = a - a1.astype(F32)
    a2 = r1.astype(BF16)
    a3 = (r1 - a2.astype(F32)).astype(BF16)
    return a1, a2, a3


def _dot_hi(a, b):
    a1, a2, _ = _split3(a)
    b1, b2, _ = _split3(b)
    return _dot(a1, b1) + (_dot(a1, b2) + _dot(a2, b1))


def _silu(x):
    return x * jax.nn.sigmoid(x)


def _rms(x, w):
    return x * lax.rsqrt(jnp.mean(x * x, axis=-1, keepdims=True) + EPS) * w


def _ada_kernel(c_ref, w_ref, b_ref, o_ref):
    o_ref[...] = _dot_hi(_silu(c_ref[...]), w_ref[...]) + b_ref[...]


def _ada(cond8, w_ada, b_ada):
    n = w_ada.shape[1]
    tn = 1536
    return pl.pallas_call(
        _ada_kernel,
        grid=(n // tn,),
        in_specs=[pl.BlockSpec((8, D_MODEL), lambda j: (0, 0)),
                  pl.BlockSpec((D_MODEL, tn), lambda j: (0, j)),
                  pl.BlockSpec((1, tn), lambda j: (0, j))],
        out_specs=pl.BlockSpec((8, tn), lambda j: (0, j)),
        out_shape=jax.ShapeDtypeStruct((8, n), F32),
        compiler_params=_cparams(("arbitrary",), VMEM_LIMIT),
        name="ada",
    )(cond8, w_ada, b_ada.reshape(1, n))


def _mod_spec(col, tm, row0, rows_per_cond):
    return pl.BlockSpec((1, 1, D_MODEL), lambda i, *_: (row0 + (i * tm) // rows_per_cond, 0, col))


def _prenorm_kernel(x_ref, sh_ref, sc_ref, g_ref, wt_ref, o_ref, dtt_ref):
    xn = _rms(x_ref[...], g_ref[...])
    h = (xn * (1.0 + sc_ref[0]) + sh_ref[0]).astype(BF16)
    o_ref[...] = h
    dtt_ref[...] = _dot_nt(wt_ref[...], h)


def _prenorm(x, mod3, gain, w_dt_t, row0, rows_per_cond):
    t = x.shape[0]
    tm = 1024
    return pl.pallas_call(
        _prenorm_kernel,
        grid=(t // tm,),
        in_specs=[pl.BlockSpec((tm, D_MODEL), lambda i: (i, 0)),
                  _mod_spec(0, tm, row0, rows_per_cond),
                  _mod_spec(1, tm, row0, rows_per_cond),
                  pl.BlockSpec((1, D_MODEL), lambda i: (0, 0)),
                  pl.BlockSpec((LANES, D_MODEL), lambda i: (0, 0))],
        out_specs=[pl.BlockSpec((tm, D_MODEL), lambda i: (i, 0)),
                   pl.BlockSpec((LANES, tm), lambda i: (0, i))],
        out_shape=[jax.ShapeDtypeStruct((t, D_MODEL), BF16), jax.ShapeDtypeStruct((LANES, t), F32)],
        compiler_params=_cparams(("parallel",), VMEM_LIMIT),
        name="prenorm",
    )(x, mod3, mod3, gain.reshape(1, D_MODEL), w_dt_t)


INPROJ_TM = 2048
INPROJ_TN = 512


def _conv_silu(x, w, b, n):
    rows = x.shape[0]
    pos = lax.broadcasted_iota(I32, x.shape, 0) % n
    prev = jnp.where(pos == 0, 0.0, pltpu.roll(x, 1, axis=0))
    nxt = jnp.where(pos == n - 1, 0.0, pltpu.roll(x, rows - 1, axis=0))
    return _silu(prev * w[0:1] + x * w[1:2] + nxt * w[2:3] + b)


def _inproj_kernel(blk_ref, nxt_ref, hc_ref, hl_ref, w_ref, wn_ref, cw_ref, cb_ref, oc_ref, ol_ref, wbf_ref,
                   *, sh_lo, sh_hi, shift, n_ctx_tiles, cv_lo, cv_hi, n_c, n_l):
    del blk_ref, nxt_ref
    j = pl.program_id(0)
    i = pl.program_id(1)
    shifted = jnp.logical_and(j >= sh_lo, j < sh_hi)
    conv = jnp.logical_and(j >= cv_lo, j < cv_hi)

    if shift:
        @pl.when(jnp.logical_and(i == 0, shifted))
        def _():
            w = jnp.concatenate([w_ref[shift:, :], wn_ref[:shift, :]], axis=0)
            wbf_ref[...] = w.T.astype(BF16)

    @pl.when(jnp.logical_and(i == 0, jnp.logical_not(shifted)))
    def _():
        wbf_ref[...] = w_ref[...].T.astype(BF16)

    def ctx_rows():
        tm = oc_ref.shape[0]
        return pl.ds(pl.multiple_of(jnp.minimum(i, n_ctx_tiles - 1) * tm, tm), tm)

    is_ctx = i < n_ctx_tiles
    plain = jnp.logical_not(conv)

    @pl.when(jnp.logical_and(is_ctx, plain))
    def _():
        oc_ref[...] = _dot(hc_ref[ctx_rows(), :], wbf_ref[...]).astype(oc_ref.dtype)

    @pl.when(jnp.logical_and(jnp.logical_not(is_ctx), plain))
    def _():
        ol_ref[...] = _dot(hl_ref[...], wbf_ref[...]).astype(ol_ref.dtype)

    if cv_hi > cv_lo:
        @pl.when(jnp.logical_and(is_ctx, conv))
        def _():
            y = _dot(hc_ref[ctx_rows(), :], wbf_ref[...])
            oc_ref[...] = _conv_silu(y, cw_ref[...], cb_ref[...], n_c).astype(oc_ref.dtype)

        @pl.when(jnp.logical_and(jnp.logical_not(is_ctx), conv))
        def _():
            y = _dot(hl_ref[...], wbf_ref[...])
            ol_ref[...] = _conv_silu(y, cw_ref[...], cb_ref[...], n_l).astype(ol_ref.dtype)


def _inproj(h_c, h_l, w_t, w_rows, out_dtype, conv_w, conv_b, seq_lens, shifted_tiles=(0, 0), conv_tiles=(0, 0)):
    tm, tn = INPROJ_TM, INPROJ_TN
    t_c, k = h_c.shape
    t_l = h_l.shape[0]
    assert t_c % tm == 0 and t_l == tm
    nct = t_c // tm
    ncol = len(w_rows)
    sh_lo, sh_hi = shifted_tiles
    shifts = {w_rows[j] % tn for j in range(sh_lo, sh_hi)}
    assert len(shifts) <= 1 and all(w_rows[j] % tn == 0 for j in range(ncol) if not sh_lo <= j < sh_hi)
    shift = shifts.pop() if shifts else 0
    assert shift % 8 == 0
    blocks = [r // tn for r in w_rows]
    nxt = [b + 1 if sh_lo <= j < sh_hi else blocks[sh_lo] if sh_hi > sh_lo else 0 for j, b in enumerate(blocks)]
    cv_lo, cv_hi = conv_tiles
    assert conv_w.shape[1] >= (cv_hi - cv_lo) * tn
    conv_col = lambda j, i, blk, nx: (0, jnp.clip(j - cv_lo, 0, max(cv_hi - cv_lo - 1, 0)))
    return pl.pallas_call(
        functools.partial(_inproj_kernel, sh_lo=sh_lo, sh_hi=sh_hi, shift=shift, n_ctx_tiles=nct,
                          cv_lo=cv_lo, cv_hi=cv_hi, n_c=seq_lens[0], n_l=seq_lens[1]),
        grid_spec=pltpu.PrefetchScalarGridSpec(
            num_scalar_prefetch=2,
            grid=(ncol, nct + 1),
            in_specs=[pl.BlockSpec((t_c, k), lambda j, i, blk, nx: (0, 0), pipeline_mode=pl.Buffered(1)),
                      pl.BlockSpec((t_l, k), lambda j, i, blk, nx: (0, 0), pipeline_mode=pl.Buffered(1)),
                      pl.BlockSpec((tn, k), lambda j, i, blk, nx: (blk[j], 0)),
                      pl.BlockSpec((tn, k), lambda j, i, blk, nx: (nx[j], 0)),
                      pl.BlockSpec((3, tn), conv_col),
                      pl.BlockSpec((1, tn), conv_col)],
            out_specs=[pl.BlockSpec((tm, tn), lambda j, i, blk, nx: (jnp.minimum(i, nct - 1), j)),
                       pl.BlockSpec((tm, tn), lambda j, i, blk, nx: (0, j))],
            scratch_shapes=[pltpu.VMEM((k, tn), BF16)]),
        out_shape=[jax.ShapeDtypeStruct((t_c, ncol * tn), out_dtype),
                   jax.ShapeDtypeStruct((t_l, ncol * tn), BF16)],
        compiler_params=_cparams(("arbitrary", "arbitrary"), VMEM_LIMIT),
        name="inproj",
    )(jnp.asarray(blocks, I32), jnp.asarray(nxt, I32), h_c, h_l, w_t, w_t, conv_w, conv_b)


def _lam(lp_ref, lam_init):
    lp = lp_ref[...]
    s1 = jnp.sum(lp[0:1] * lp[1:2], axis=-1, keepdims=True)
    s2 = jnp.sum(lp[2:3] * lp[3:4], axis=-1, keepdims=True)
    return jnp.exp(s1) - jnp.exp(s2) + lam_init


ATTN_TQ = 64


def _with_ones(v):
    return jnp.concatenate([v, jnp.ones(v.shape, v.dtype)], axis=1)


def _diff_attn(q, k, v1, lam, wsub, lam_init):
    nq = q.shape[0]
    lane = lax.broadcasted_iota(I32, q.shape, 1)
    q = q * (A_HEAD_DIM ** -0.5 * LOG2E)
    q1 = jnp.where(lane < A_HEAD_DIM, q, 0.0).astype(BF16)
    q2 = jnp.where(lane >= A_HEAD_DIM, q, 0.0).astype(BF16)
    s = _dot_nt(jnp.concatenate([q1, q2], axis=0), k)
    p = jnp.exp2(s - jnp.max(s, axis=-1, keepdims=True)).astype(BF16)
    pv = _dot(p, v1)
    pv = pv[:, :A_V_DIM] / pv[:, A_V_DIM:]
    o = pv[:nq] - lam * pv[nq:]
    return _rms(o, wsub) * (1.0 - lam_init)


def _attn_ctx_kernel(lp_ref, q_ref, k_ref, v_ref, ws_ref, o_ref, *, lam_init, tq):
    lam = _lam(lp_ref, lam_init)
    for h in range(A_HEADS):
        sl = slice(h * LANES, (h + 1) * LANES)
        k = k_ref[:, sl].astype(BF16)
        v1 = _with_ones(v_ref[:, sl].astype(BF16))
        for i in range(q_ref.shape[0] // tq):
            rows = slice(i * tq, (i + 1) * tq)
            o = _diff_attn(q_ref[rows, sl].astype(F32), k, v1, lam, ws_ref[...], lam_init)
            o_ref[rows, sl] = o.astype(o_ref.dtype)


def _attn_ctx(lam_p, proj, k, v, wsub, nbatch, n, lam_init):
    w = A_HEADS * LANES
    blk = pl.BlockSpec((n, w), lambda b: (b, 0))
    return pl.pallas_call(
        functools.partial(_attn_ctx_kernel, lam_init=lam_init, tq=ATTN_TQ),
        grid=(nbatch,),
        in_specs=[pl.BlockSpec((4, A_HEAD_DIM), lambda b: (0, 0)),
                  pl.BlockSpec((n, w), lambda b: (b, PROJ_Q // w)), blk, blk,
                  pl.BlockSpec((1, A_V_DIM), lambda b: (0, 0))],
        out_specs=blk,
        out_shape=jax.ShapeDtypeStruct(k.shape, BF16),
        compiler_params=_cparams(("parallel",), VMEM_LIMIT),
        name="attn_ctx",
    )(lam_p, proj, k, v, wsub)


def _rope(x, cos, sin_signed):
    lane = lax.broadcasted_iota(I32, x.shape, 1)
    quarter = A_HEAD_DIM // 4
    partner = jnp.where((lane % (2 * quarter)) < quarter,
                        pltpu.roll(x, LANES - quarter, axis=1),
                        pltpu.roll(x, quarter, axis=1))
    return x * cos + partner * sin_signed


def _attn_lat_kernel(lp_ref, q_ref, k_ref, v_ref, kc_ref, vc_ref, cos_ref, sin_ref, ws_ref, o_ref, *, lam_init, tq):
    lam = _lam(lp_ref, lam_init)
    k = _rope(k_ref[...].astype(F32), cos_ref[...], sin_ref[...])
    k_all = jnp.concatenate([kc_ref[...].astype(BF16), k.astype(BF16)], axis=0)
    v_all = _with_ones(jnp.concatenate([vc_ref[...].astype(BF16), v_ref[...].astype(BF16)], axis=0))
    for i in range(q_ref.shape[0] // tq):
        rows = slice(i * tq, (i + 1) * tq)
        q = _rope(q_ref[rows, :].astype(F32), cos_ref[rows, :], sin_ref[rows, :])
        o_ref[rows, :] = _diff_attn(q, k_all, v_all, lam, ws_ref[...], lam_init).astype(o_ref.dtype)


def _attn_lat(lam_p, proj, k, v, kc, vc, cos, sin, wsub, nbatch, n, npast, lam_init):
    blk = pl.BlockSpec((n, LANES), lambda b, h: (b, h))
    cblk = pl.BlockSpec((npast, LANES), lambda b, h: (b, h))
    tab = pl.BlockSpec((n, LANES), lambda b, h: (0, 0))
    return pl.pallas_call(
        functools.partial(_attn_lat_kernel, lam_init=lam_init, tq=ATTN_TQ),
        grid=(nbatch, A_HEADS),
        in_specs=[pl.BlockSpec((4, A_HEAD_DIM), lambda b, h: (0, 0)),
                  pl.BlockSpec((n, LANES), lambda b, h: (b, PROJ_Q // LANES + h)), blk, blk, cblk, cblk, tab, tab,
                  pl.BlockSpec((1, A_V_DIM), lambda b, h: (0, 0))],
        out_specs=blk,
        out_shape=jax.ShapeDtypeStruct(k.shape, BF16),
        compiler_params=_cparams(("parallel", "parallel"), VMEM_LIMIT),
        name="attn_lat",
    )(lam_p, proj, k, v, kc, vc, cos, sin, wsub)


def _rope_tables(n):
    pos = np.arange(n)
    rowcol = np.stack([pos // GRID_W, pos % GRID_W], axis=-1).astype(np.float64)
    nf = A_HEAD_DIM // 4
    inv_freq = np.power(ROPE_BASE, -np.arange(nf, dtype=np.float64) / nf)
    lane = np.arange(LANES)
    axis = (lane % A_HEAD_DIM) // (2 * nf)
    ang = rowcol[:, axis] * inv_freq[lane % nf][None, :]
    sign = np.where((lane % (2 * nf)) < nf, -1.0, 1.0)
    return jnp.asarray(np.cos(ang), F32), jnp.asarray(np.sin(ang) * sign[None, :], F32)


S_PAIRS = S_INNER // LANES


def _ssd_direction(xa_ref, dt_raw_t, dtb_col, alog_col, ht_ref, y_ref, backward):
    col0 = S_HEADS if backward else 0
    row = lax.broadcasted_iota(I32, (S_CHUNK, S_CHUNK), 0)
    col = lax.broadcasted_iota(I32, (S_CHUNK, S_CHUNK), 1)
    dt_t = jax.nn.softplus(dt_raw_t + dtb_col)
    da_t = dt_t * (-jnp.exp(alog_col) * LOG2E)
    if backward:
        last = 0
        keep = row <= col
    else:
        last = S_CHUNK - 1
        keep = row >= col
    tri = jnp.where(keep, 1.0, 0.0).astype(BF16)
    pieces = _split3(da_t)
    cum = sum(_dot_nt(tri, p) for p in pieces[1:]) + _dot_nt(tri, pieces[0])
    cum_t = sum(_dot_nt(p, tri) for p in pieces[1:]) + _dot_nt(pieces[0], tri)
    w_t = dt_t * jnp.exp2(cum_t[:, last:last + 1] - cum_t)
    e_tot = jnp.exp2(cum[last:last + 1, :])
    lo_half = col < S_HEADDIM
    lo_bf = jnp.where(lo_half, 1.0, 0.0).astype(BF16)
    hi_bf = jnp.where(lo_half, 0.0, 1.0).astype(BF16)
    for g in range(S_GROUPS):
        b_bf = xa_ref[:, S_INNER + g * S_STATE:S_INNER + (g + 1) * S_STATE]
        c_bf = xa_ref[:, S_INNER + (S_GROUPS + g) * S_STATE:S_INNER + (S_GROUPS + g + 1) * S_STATE]
        cb = _dot_nt(c_bf, b_bf)
        bt = b_bf.astype(F32).T
        c_f = c_bf.astype(F32)
        for pr in range(S_PAIRS // S_GROUPS):
            hp = g * (S_PAIRS // S_GROUPS) + pr
            c0 = col0 + 2 * hp
            x_pair = xa_ref[:, hp * LANES:(hp + 1) * LANES]
            bd_x = jnp.concatenate([x_pair * lo_bf, x_pair * hi_bf], axis=0)
            ht = ht_ref[hp]
            bd_h = jnp.concatenate([jnp.where(lo_half, ht, 0.0), jnp.where(lo_half, 0.0, ht)], axis=0).astype(BF16)
            ms, ces, bws = [], [], []
            for cc in (c0, c0 + 1):
                bc = jnp.broadcast_to(cum[:, cc:cc + 1], (S_CHUNK, S_CHUNK))
                decay = jnp.exp2(jnp.where(keep, bc - cum_t[cc:cc + 1, :], -1e30))
                ms.append((cb * decay * dt_t[cc:cc + 1, :]).astype(BF16))
                ces.append((c_f * jnp.exp2(bc)).astype(BF16))
                bws.append((bt * w_t[cc:cc + 1, :]).astype(BF16))
            y_ref[:, hp * LANES:(hp + 1) * LANES] = _dot(jnp.concatenate(ms + ces, axis=1),
                                                         jnp.concatenate([bd_x, bd_h], axis=0)).astype(y_ref.dtype)
            tot = jnp.where(lo_half[0:1], e_tot[:, c0:c0 + 1], e_tot[:, c0 + 1:c0 + 2])
            ht_ref[hp] = ht * tot + _dot(jnp.concatenate(bws, axis=1), bd_x)


def _ssd_kernel(*refs, has_h0, emit_final):
    xf_ref, xb_ref, dtft_ref, dtbt_ref, bc_ref, ac_ref = refs[:6]
    refs = refs[6:]
    if has_h0:
        h0_ref, refs = refs[0], refs[1:]
    yf_ref, yb_ref = refs[:2]
    refs = refs[2:]
    if emit_final:
        fin_ref, refs = refs[0], refs[1:]
    htf_ref, htb_ref = refs
    c = pl.program_id(1)
    nc = pl.num_programs(1)

    @pl.when(c == 0)
    def _():
        for d, ht_ref in enumerate((htf_ref, htb_ref)):
            for hp in range(S_PAIRS):
                if has_h0:
                    ht_ref[hp] = h0_ref[0, d, hp * LANES:(hp + 1) * LANES, :].T
                else:
                    ht_ref[hp] = jnp.zeros((S_STATE, LANES), F32)

    _ssd_direction(xf_ref, dtft_ref[...], bc_ref[...], ac_ref[...], htf_ref, yf_ref, backward=False)
    _ssd_direction(xb_ref, dtbt_ref[...], bc_ref[...], ac_ref[...], htb_ref, yb_ref, backward=True)

    if emit_final:
        @pl.when(c == nc - 1)
        def _():
            for d, ht_ref in enumerate((htf_ref, htb_ref)):
                for hp in range(S_PAIRS):
                    fin_ref[0, d, hp * LANES:(hp + 1) * LANES, :] = ht_ref[hp].T


def _ssd(xa, dt_t, dtb_col, alog_col, h0, nbatch, n, emit_final):
    nc = n // S_CHUNK
    t = xa.shape[0]
    fwd = lambda b, c: (b * nc + c, 0)
    bwd = lambda b, c: (b * nc + nc - 1 - c, 0)
    const = lambda b, c: (0, 0)
    in_specs = [pl.BlockSpec((S_CHUNK, S_CONV_CH), fwd),
                pl.BlockSpec((S_CHUNK, S_CONV_CH), bwd),
                pl.BlockSpec((LANES, S_CHUNK), lambda b, c: (0, b * nc + c)),
                pl.BlockSpec((LANES, S_CHUNK), lambda b, c: (0, b * nc + nc - 1 - c)),
                pl.BlockSpec((LANES, 1), const), pl.BlockSpec((LANES, 1), const)]
    args = [xa, xa, dt_t, dt_t, dtb_col, alog_col]
    if h0 is not None:
        in_specs.append(pl.BlockSpec((1, 2, S_INNER, S_STATE), lambda b, c: (b, 0, 0, 0)))
        args.append(h0)
    out_specs = [pl.BlockSpec((S_CHUNK, S_INNER), fwd), pl.BlockSpec((S_CHUNK, S_INNER), bwd)]
    out_shape = [jax.ShapeDtypeStruct((t, S_INNER), BF16), jax.ShapeDtypeStruct((t, S_INNER), BF16)]
    if emit_final:
        out_specs.append(pl.BlockSpec((1, 2, S_INNER, S_STATE), lambda b, c: (b, 0, 0, 0)))
        out_shape.append(jax.ShapeDtypeStruct((nbatch, 2, S_INNER, S_STATE), F32))
    return pl.pallas_call(
        functools.partial(_ssd_kernel, has_h0=h0 is not None, emit_final=emit_final),
        grid=(nbatch, nc),
        in_specs=in_specs,
        out_specs=out_specs,
        out_shape=out_shape,
        scratch_shapes=[pltpu.VMEM((S_PAIRS, S_STATE, LANES), F32),
                        pltpu.VMEM((S_PAIRS, S_STATE, LANES), F32)],
        compiler_params=_cparams(("parallel", "arbitrary"), VMEM_LIMIT),
        name="ssd",
    )(*args)


MIXOUT_SPLIT = 2


def _mixout_kernel(yf_ref, yb_ref, xs_ref, z_ref, oa_ref, ga_ref, gs_ref, x_ref, g1_ref, sh2_ref, sc2_ref,
                   dskip_ref, wn_ref, woa_ref, wos_ref, wout_ref, gpost_ref, gpre_ref, wr_ref,
                   x1_ref, h2_ref, aff_ref):
    wn = wn_ref[...]
    sub = x_ref.shape[0] // MIXOUT_SPLIT
    for r in range(MIXOUT_SPLIT):
        rows = slice(r * sub, (r + 1) * sub)
        y = (yf_ref[rows, :].astype(F32) + yb_ref[rows, :].astype(F32)
             + xs_ref[rows, :].astype(F32) * dskip_ref[...])
        y = y * _silu(z_ref[rows, :].astype(F32))
        parts = []
        for g in range(S_GROUPS):
            sl = slice(g * S_GROUP_W, (g + 1) * S_GROUP_W)
            parts.append(_rms(y[:, sl], wn[:, sl]).astype(BF16))
        o_ssd = jnp.concatenate(parts, axis=1)
        a = _dot(oa_ref[rows, :], woa_ref[...])
        s = _dot(o_ssd, wos_ref[...])
        merged = (jax.nn.sigmoid(ga_ref[rows, :].astype(F32)) * a
                  + jax.nn.sigmoid(gs_ref[rows, :].astype(F32)) * s)
        ym = _dot(merged.astype(BF16), wout_ref[...])
        x1 = x_ref[rows, :] + g1_ref[0] * _rms(ym, gpost_ref[...])
        x1_ref[rows, :] = x1
        h2 = _rms(x1, gpre_ref[...]) * (1.0 + sc2_ref[0]) + sh2_ref[0]
        h2_ref[rows, :] = h2.astype(BF16)
        logits = _dot_hi(h2, wr_ref[...])
        lane = lax.broadcasted_iota(I32, logits.shape, 1)
        logits = jnp.where(lane < N_EXPERTS, logits, -1e30)
        p = jnp.exp(logits - jnp.max(logits, axis=-1, keepdims=True))
        aff_ref[rows, :] = p / jnp.sum(p, axis=-1, keepdims=True)


def _mixout(yf, yb, xa, proj, o_attn, x, mod3, dskip, wn, woa, wos, wout, gpost, gpre, wr, row0, rows_per_cond):
    t = x.shape[0]
    tm = 512
    tok = lambda w: pl.BlockSpec((tm, w), lambda i: (i, 0))
    const = lambda a: pl.BlockSpec(a.shape, lambda i: (0,) * a.ndim, pipeline_mode=pl.Buffered(1))
    return pl.pallas_call(
        _mixout_kernel,
        grid=(t // tm,),
        in_specs=[tok(S_INNER), tok(S_INNER), tok(S_INNER),
                  pl.BlockSpec((tm, S_INNER), lambda i: (i, PROJ_Z // S_INNER)), tok(D_MODEL),
                  pl.BlockSpec((tm, D_MODEL), lambda i: (i, PROJ_GATES // D_MODEL)),
                  pl.BlockSpec((tm, D_MODEL), lambda i: (i, PROJ_GATES // D_MODEL + 1)),
                  tok(D_MODEL),
                  _mod_spec(2, tm, row0, rows_per_cond),
                  _mod_spec(3, tm, row0, rows_per_cond),
                  _mod_spec(4, tm, row0, rows_per_cond),
                  const(dskip), const(wn), const(woa), const(wos), const(wout), const(gpost), const(gpre),
                  const(wr)],
        out_specs=[tok(D_MODEL), tok(D_MODEL), tok(LANES)],
        out_shape=[jax.ShapeDtypeStruct((t, D_MODEL), F32), jax.ShapeDtypeStruct((t, D_MODEL), BF16),
                   jax.ShapeDtypeStruct((t, LANES), F32)],
        compiler_params=_cparams(("parallel",), VMEM_LIMIT),
        name="mixout",
    )(yf, yb, xa, proj, o_attn, proj, proj, x, mod3, mod3, mod3, dskip, wn, woa, wos, wout, gpost, gpre, wr)


COMB_TILE = 256
COMB_WIN = 64
BF16_ROWS = 16


def _router_kernel(aff_ref, pos_ref, cnt_ref, *, cap):
    aff = aff_ref[...]
    e, t = aff.shape
    idx = lax.broadcasted_iota(I32, (e, t), 1)

    def count(m):
        return jnp.sum(jnp.where(m, 1.0, 0.0), axis=-1, keepdims=True)

    def value_bit(i, thr):
        cand = thr | jnp.left_shift(jnp.int32(1), 30 - i)
        return jnp.where(count(aff >= pltpu.bitcast(cand, F32)) >= cap, cand, thr)

    thr = pltpu.bitcast(lax.fori_loop(0, 31, value_bit, jnp.zeros((e, 1), I32)), F32)
    gt = aff > thr
    eq = aff == thr
    need = cap - count(gt)
    nbits = (t - 1).bit_length()

    def index_bit(i, v):
        cand = v | jnp.left_shift(jnp.int32(1), nbits - 1 - i)
        return jnp.where(count(jnp.logical_and(eq, idx < cand)) < need, cand, v)

    last = lax.fori_loop(0, nbits, index_bit, jnp.zeros((e, 1), I32))
    sel = jnp.logical_or(gt, jnp.logical_and(eq, idx <= last))
    sel_bf = jnp.where(sel, 1.0, 0.0).astype(BF16)
    blk = 512
    r = lax.broadcasted_iota(I32, (blk, blk), 0)
    c = lax.broadcasted_iota(I32, (blk, blk), 1)
    upper = jnp.where(r <= c, 1.0, 0.0).astype(BF16)
    carry = jnp.zeros((e, 1), F32)
    tile_cnt = []
    for b in range(t // blk):
        cum = _dot(sel_bf[:, b * blk:(b + 1) * blk], upper) + carry
        carry = cum[:, blk - 1:blk]
        slot = jnp.where(sel[:, b * blk:(b + 1) * blk], cum - 1.0, -1.0)
        pos_ref[:, b * blk:(b + 1) * blk] = slot.astype(I32)
        for j in range(blk // COMB_TILE):
            tile_cnt.append(cum[:, (j + 1) * COMB_TILE - 1:(j + 1) * COMB_TILE])
    lane = lax.broadcasted_iota(I32, (e, LANES), 1)
    cnt = jnp.zeros((e, LANES), F32)
    for i, v in enumerate(tile_cnt):
        cnt = jnp.where(lane == i, v, cnt)
    cnt_ref[...] = cnt.astype(I32)


class _Route(NamedTuple):
    pos: jax.Array
    start: jax.Array
    end: jax.Array
    win: jax.Array
    fits: jax.Array


def _router(aff_t, cap):
    e, t = aff_t.shape
    ntile = t // COMB_TILE
    pos, cnt = pl.pallas_call(
        functools.partial(_router_kernel, cap=cap),
        out_shape=[jax.ShapeDtypeStruct((e, t), I32), jax.ShapeDtypeStruct((e, LANES), I32)],
        name="router",
    )(aff_t)
    end = cnt[:, :ntile]
    start = jnp.concatenate([jnp.zeros((e, 1), I32), end[:, :-1]], axis=1)
    win = jnp.minimum(start // BF16_ROWS * BF16_ROWS, cap - COMB_WIN)
    fits = jnp.all(end - win <= COMB_WIN, axis=0).astype(I32)
    flat = lambda a: a.T.reshape(ntile * e)
    return _Route(pos, flat(start), flat(end), flat(win), fits)


def _dispatch_kernel(start_ref, end_ref, win_ref, fits_ref, pos_ref, h_ref, o_ref):
    i = pl.program_id(0)
    ne, cap, _ = o_ref.shape
    tm = h_ref.shape[0]

    @pl.when(i == 0)
    def _():
        o_ref[...] = jnp.zeros(o_ref.shape, o_ref.dtype)

    h = h_ref[...]
    row = lax.broadcasted_iota(I32, (COMB_WIN, tm), 0)
    row_l = lax.broadcasted_iota(I32, (COMB_WIN, LANES), 0)

    def place(e, r0, rows):
        rel = r0 - start_ref[i * ne + e] + row_l
        mine = jnp.logical_and(rel >= 0, rel < end_ref[i * ne + e] - start_ref[i * ne + e])
        for c in range(D_MODEL // LANES):
            sl = slice(c * LANES, (c + 1) * LANES)
            old = o_ref[e, pl.ds(r0, COMB_WIN), sl].astype(F32)
            o_ref[e, pl.ds(r0, COMB_WIN), sl] = jnp.where(mine, rows[:, sl], old).astype(o_ref.dtype)

    @pl.when(fits_ref[i] == 1)
    def _():
        sel = [jnp.where(pos_ref[e:e + 1, :] == win_ref[i * ne + e] + row, 1.0, 0.0).astype(BF16) for e in range(ne)]
        rows = _dot(jnp.concatenate(sel, axis=0), h)
        for e in range(ne):
            place(e, pl.multiple_of(win_ref[i * ne + e], BF16_ROWS), rows[e * COMB_WIN:(e + 1) * COMB_WIN])

    @pl.when(fits_ref[i] == 0)
    def _():
        for e in range(ne):
            w0 = win_ref[i * ne + e]

            def window(w, carry):
                r0 = pl.multiple_of(jnp.minimum(w0 + w * COMB_WIN, cap - COMB_WIN), BF16_ROWS)
                sel = jnp.where(pos_ref[e:e + 1, :] == r0 + row, 1.0, 0.0).astype(BF16)
                place(e, r0, _dot(sel, h))
                return carry

            lax.fori_loop(0, (end_ref[i * ne + e] - w0 + COMB_WIN - 1) // COMB_WIN, window, 0)


def _dispatch(route, h2, cap):
    e, t = route.pos.shape
    tm = COMB_TILE
    return pl.pallas_call(
        _dispatch_kernel,
        grid_spec=pltpu.PrefetchScalarGridSpec(
            num_scalar_prefetch=4,
            grid=(t // tm,),
            in_specs=[pl.BlockSpec((e, tm), lambda i, *_: (0, i)),
                      pl.BlockSpec((tm, D_MODEL), lambda i, *_: (i, 0))],
            out_specs=pl.BlockSpec((e, cap, D_MODEL), lambda i, *_: (0, 0, 0))),
        out_shape=jax.ShapeDtypeStruct((e, cap, D_MODEL), BF16),
        compiler_params=_cparams(("arbitrary",), VMEM_LIMIT),
        name="dispatch",
    )(route.start, route.end, route.win, route.fits, route.pos, h2)


def _ffn_kernel(xc_ref, xl_ref, wg_ref, wu_ref, wd_ref, oc_ref, ol_ref, acc_ref):
    f = pl.program_id(1)
    cc = xc_ref.shape[1]
    xe = jnp.concatenate([xc_ref[0], xl_ref[0]], axis=0)
    hg = _dot(xe, wg_ref[0].astype(BF16))
    hu = _dot(xe, wu_ref[0].astype(BF16))
    part = _dot((_silu(hg) * hu).astype(BF16), wd_ref[0].astype(BF16))

    @pl.when(f == 0)
    def _():
        acc_ref[...] = part

    @pl.when(f > 0)
    def _():
        acc_ref[...] += part

    @pl.when(f == pl.num_programs(1) - 1)
    def _():
        oc_ref[0] = acc_ref[:cc, :].astype(BF16)
        ol_ref[0] = acc_ref[cc:, :].astype(BF16)


def _ffn(xe_c, xe_l, w_gate, w_up, w_down):
    e, cc, d = xe_c.shape
    cl = xe_l.shape[1]
    tf = 1024
    return pl.pallas_call(
        _ffn_kernel,
        grid=(e, D_FF // tf),
        in_specs=[pl.BlockSpec((1, cc, d), lambda i, f: (i, 0, 0)),
                  pl.BlockSpec((1, cl, d), lambda i, f: (i, 0, 0)),
                  pl.BlockSpec((1, d, tf), lambda i, f: (i, 0, f)),
                  pl.BlockSpec((1, d, tf), lambda i, f: (i, 0, f)),
                  pl.BlockSpec((1, tf, d), lambda i, f: (i, f, 0))],
        out_specs=[pl.BlockSpec((1, cc, d), lambda i, f: (i, 0, 0)),
                   pl.BlockSpec((1, cl, d), lambda i, f: (i, 0, 0))],
        out_shape=[jax.ShapeDtypeStruct((e, cc, d), BF16), jax.ShapeDtypeStruct((e, cl, d), BF16)],
        scratch_shapes=[pltpu.VMEM((cc + cl, d), F32)],
        compiler_params=_cparams(("parallel", "arbitrary"), VMEM_LIMIT),
        name="ffn",
    )(xe_c, xe_l, w_gate, w_up, w_down)


def _combine_kernel(win_ref, fits_ref, pos_ref, aff_ref, ye_ref, x1_ref, g2_ref, gp_ref, o_ref, rhs_ref, y_ref):
    i = pl.program_id(0)
    tm = pos_ref.shape[0]
    cap = ye_ref.shape[1]
    pos = pos_ref[...]
    aff = aff_ref[...]

    @pl.when(fits_ref[i] == 1)
    def _():
        per_col = LANES // COMB_WIN
        lane = lax.broadcasted_iota(I32, (tm, LANES), 1)
        cols = []
        for c in range(N_EXPERTS // per_col):
            col = jnp.zeros((tm, LANES), F32)
            for j in range(per_col):
                e = c * per_col + j
                w0 = pl.multiple_of(win_ref[i * N_EXPERTS + e], BF16_ROWS)
                rhs_ref[e * COMB_WIN:(e + 1) * COMB_WIN, :] = ye_ref[e, pl.ds(w0, COMB_WIN), :]
                p = pos[:, e:e + 1]
                tgt = jnp.where(p >= 0, p - w0 + j * COMB_WIN, -1)
                col = jnp.where(lane == tgt, aff[:, e:e + 1], col)
            cols.append(col)
        sel = jnp.concatenate(cols, axis=1)
        hi = sel.astype(BF16)
        lo = (sel - hi.astype(F32)).astype(BF16)
        rhs = rhs_ref[...]
        y_ref[...] = _dot(hi, rhs) + _dot(lo, rhs)

    @pl.when(fits_ref[i] == 0)
    def _():
        slot = lax.broadcasted_iota(I32, (tm, cap), 1)
        y = jnp.zeros((tm, D_MODEL), F32)
        for e in range(N_EXPERTS):
            onehot = jnp.where(pos[:, e:e + 1] == slot, 1.0, 0.0).astype(BF16)
            y = y + aff[:, e:e + 1] * _dot(onehot, ye_ref[e])
        y_ref[...] = y

    o_ref[...] = x1_ref[...] + g2_ref[0] * _rms(y_ref[...], gp_ref[...])


def _combine(route, aff, ye, x1, mod3, gpost, row0, rows_per_cond):
    t = x1.shape[0]
    pos_t, win, fits = route.pos.T, route.win, route.fits
    e, cap, d = ye.shape
    tm = COMB_TILE
    return pl.pallas_call(
        _combine_kernel,
        grid_spec=pltpu.PrefetchScalarGridSpec(
            num_scalar_prefetch=2,
            grid=(t // tm,),
            in_specs=[pl.BlockSpec((tm, e), lambda i, *_: (i, 0)),
                      pl.BlockSpec((tm, LANES), lambda i, *_: (i, 0)),
                      pl.BlockSpec((e, cap, d), lambda i, *_: (0, 0, 0)),
                      pl.BlockSpec((tm, d), lambda i, *_: (i, 0)),
                      _mod_spec(5, tm, row0, rows_per_cond),
                      pl.BlockSpec((1, d), lambda i, *_: (0, 0))],
            out_specs=pl.BlockSpec((tm, d), lambda i, *_: (i, 0)),
            scratch_shapes=[pltpu.VMEM((e * COMB_WIN, d), BF16), pltpu.VMEM((tm, d), F32)]),
        out_shape=jax.ShapeDtypeStruct((t, d), F32),
        compiler_params=_cparams(("parallel",), VMEM_LIMIT),
        name="combine",
    )(win, fits, pos_t, aff, ye, x1, mod3, gpost.reshape(1, d))


def _head_col(v):
    return jnp.concatenate([v.reshape(2 * S_HEADS), jnp.zeros((LANES - 2 * S_HEADS,), F32)]).reshape(LANES, 1)


def kernel(x_prompt, x_sample, cache_k, cache_v, state_ssd, c, c_ctx, w_ada, b_ada, g_pre_mix, g_post_mix, g_pre_ffn, g_post_ffn, w_in, lam_q1, lam_k1, lam_q2, lam_k2, w_subln, conv_w, conv_b, dt_bias, a_log, d_skip, w_ssd_norm, w_o_attn, w_o_ssd, w_out, w_router, w_gate, w_up, w_down):
    nb_c, n_c, d = x_prompt.shape
    nb_l, n_l, _ = x_sample.shape
    npast = cache_k.shape[2]
    depth = w_in.shape[0]
    assert depth == 1 and d == D_MODEL
    t_c, t_l = nb_c * n_c, nb_l * n_l
    lyr = 0
    lam_init = 0.8 - 0.6 * math.exp(-0.3 * lyr)

    cond8 = jnp.concatenate([c_ctx[None, :], c, jnp.zeros((8 - 1 - nb_l, d), F32)], axis=0)
    mod3 = _ada(cond8, w_ada[lyr], b_ada[lyr]).reshape(8, 1, N_MOD * d)

    w_in_t = w_in[lyr].T
    w_dt_t = jnp.concatenate([w_in_t[IN_DT:IN_GATES], jnp.zeros((LANES - 2 * S_HEADS, d), F32)], axis=0).astype(BF16)
    tn = INPROJ_TN
    rows_of = lambda col0, width: [col0 + j * tn for j in range(width // tn)]
    proj_rows = rows_of(IN_XBC, S_CONV_CH) + rows_of(IN_Q, d) + rows_of(IN_Z, S_INNER) + rows_of(IN_GATES, 2 * d)
    gate_tiles = (PROJ_GATES // tn, (PROJ_GATES + 2 * d) // tn)
    conv_tiles = (PROJ_XBC // tn, (PROJ_XBC + S_CONV_CH) // tn)
    cw, cb = conv_w[lyr], conv_b[lyr].reshape(1, S_CONV_CH)
    lam_p = jnp.stack([lam_q1[lyr], lam_k1[lyr], lam_q2[lyr], lam_k2[lyr]], axis=0)
    wsub = w_subln[lyr].reshape(1, A_V_DIM)
    dtb_col = _head_col(dt_bias[lyr])
    alog_col = _head_col(a_log[lyr])
    dskip = jnp.repeat(d_skip[lyr], S_HEADDIM).reshape(1, S_INNER)
    wn = w_ssd_norm[lyr].reshape(1, S_INNER)
    woa = w_o_attn[lyr].astype(BF16)
    wos = w_o_ssd[lyr].astype(BF16)
    wout = w_out[lyr].astype(BF16)
    wr = jnp.concatenate([w_router[lyr], jnp.zeros((d, LANES - N_EXPERTS), F32)], axis=1)
    gpost = g_post_mix[lyr].reshape(1, d)
    gpre = g_pre_ffn[lyr].reshape(1, d)
    cos, sin = _rope_tables(n_l)

    xc = x_prompt.reshape(t_c, d)
    xl = x_sample.reshape(t_l, d)
    h_c, dtt_c = _prenorm(xc, mod3, g_pre_mix[lyr], w_dt_t, 0, t_c)
    h_l, dtt_l = _prenorm(xl, mod3, g_pre_mix[lyr], w_dt_t, 1, n_l)
    proj_c, proj_l = _inproj(h_c, h_l, w_in_t, proj_rows, BF16, cw, cb, (n_c, n_l), gate_tiles, conv_tiles)
    k_c, k_l = _inproj(h_c, h_l, w_in_t, rows_of(IN_K, d), F32, cw, cb, (n_c, n_l))
    v_c, v_l = _inproj(h_c, h_l, w_in_t, rows_of(IN_V, d), F32, cw, cb, (n_c, n_l))

    def mixer(x, dt_t, proj, k, v, row0, rows_per_cond, nbatch, n, ctx):
        if ctx is None:
            o_attn = _attn_ctx(lam_p, proj, k, v, wsub, nbatch, n, lam_init)
            h0 = None
        else:
            kc, vc, h0 = ctx
            o_attn = _attn_lat(lam_p, proj, k, v, kc, vc, cos, sin, wsub, nbatch, n, npast, lam_init)
        xa = proj
        res = _ssd(xa, dt_t, dtb_col, alog_col, h0, nbatch, n, emit_final=ctx is None)
        yf, yb = res[0], res[1]
        x1, h2, aff = _mixout(yf, yb, xa, proj, o_attn, x, mod3, dskip, wn, woa, wos, wout, gpost, gpre, wr,
                              row0, rows_per_cond)
        fin = res[2] if ctx is None else None
        return x1, h2, aff, fin

    x1_c, h2_c, aff_c, fin_c = mixer(xc, dtt_c, proj_c, k_c, v_c, 0, t_c, nb_c, n_c, None)
    kc = cache_k[:, lyr].reshape(nb_l * npast, A_HEADS * 2 * A_HEAD_DIM)
    vc = cache_v[:, lyr].reshape(nb_l * npast, A_HEADS * A_V_DIM)
    h0 = state_ssd[:, lyr].reshape(nb_l, 2, S_INNER, S_STATE)
    x1_l, h2_l, aff_l, _ = mixer(xl, dtt_l, proj_l, k_l, v_l, 1, n_l, nb_l, n_l, (kc, vc, h0))

    cap_c = EC_FACTOR * t_c // N_EXPERTS
    cap_l = EC_FACTOR * t_l // N_EXPERTS
    route_c = _router(aff_c[:, :N_EXPERTS].T, cap_c)
    route_l = _router(aff_l[:, :N_EXPERTS].T, cap_l)
    xe_c = _dispatch(route_c, h2_c, cap_c)
    xe_l = _dispatch(route_l, h2_l, cap_l)
    ye_c, ye_l = _ffn(xe_c, xe_l, w_gate[lyr], w_up[lyr], w_down[lyr])
    out_c = _combine(route_c, aff_c, ye_c, x1_c, mod3, g_post_ffn[lyr], 0, t_c)
    out_l = _combine(route_l, aff_l, ye_l, x1_l, mod3, g_post_ffn[lyr], 1, n_l)

    y_prompt = out_c.reshape(nb_c, n_c, d)
    y_sample = out_l.reshape(nb_l, n_l, d)
    new_k = k_c.reshape(nb_c, 1, n_c, A_HEADS, 2 * A_HEAD_DIM)
    new_v = v_c.reshape(nb_c, 1, n_c, A_HEADS, A_V_DIM)
    new_state = fin_c.reshape(nb_c, 1, 2, S_HEADS, S_HEADDIM, S_STATE)
    return (y_prompt, y_sample, new_k, new_v, new_state)
```

```python
import functools
import math
from typing import NamedTuple

import jax
import jax.numpy as jnp
import numpy as np
from jax import lax
from jax.experimental import pallas as pl
from jax.experimental.pallas import tpu as pltpu

F32 = jnp.float32
BF16 = jnp.bfloat16
I32 = jnp.int32

D_MODEL = 1024
GRID_W = 64
EPS = 1e-6
N_MOD = 6
A_HEADS = 8
A_HEAD_DIM = 64
A_V_DIM = 128
ROPE_BASE = 10000.0
S_INNER = 2048
S_HEADDIM = 64
S_HEADS = 32
S_GROUPS = 4
S_GROUP_W = S_INNER // S_GROUPS
S_STATE = 128
S_CHUNK = 128
S_CONV_CH = S_INNER + 2 * S_GROUPS * S_STATE
N_EXPERTS = 16
EC_FACTOR = 2
D_FF = 2048
LANES = 128
IN_Q, IN_K, IN_V, IN_Z, IN_XBC = 0, 1024, 2048, 3072, 5120
IN_DT = IN_XBC + S_CONV_CH
IN_GATES = IN_DT + 2 * S_HEADS
PROJ_XBC, PROJ_Q, PROJ_Z, PROJ_GATES = 0, 3072, 4096, 6144
VMEM_LIMIT = 56 * 1024 * 1024
LOG2E = 1.4426950408889634


def _cparams(sem, vmem=None):
    return pltpu.CompilerParams(dimension_semantics=sem, vmem_limit_bytes=vmem)


def _dot(a, b):
    return jnp.dot(a, b, preferred_element_type=F32)


def _dot_nt(a, b):
    return lax.dot_general(a, b, (((1,), (1,)), ((), ())), preferred_element_type=F32)


def _split3(a):
    a1 = a.astype(BF16)
    r1 = a - a1.astype(F32)
    a2 = r1.astype(BF16)
    a3 = (r1 - a2.astype(F32)).astype(BF16)
    return a1, a2, a3


def _dot_hi(a, b):
    a1, a2, _ = _split3(a)
    b1, b2, _ = _split3(b)
    return _dot(a1, b1) + (_dot(a1, b2) + _dot(a2, b1))


def _silu(x):
    return x * jax.nn.sigmoid(x)


def _rms(x, w):
    return x * lax.rsqrt(jnp.mean(x * x, axis=-1, keepdims=True) + EPS) * w


def _ada_kernel(c_ref, w_ref, b_ref, o_ref):
    o_ref[...] = _dot_hi(_silu(c_ref[...]), w_ref[...]) + b_ref[...]


def _ada(cond8, w_ada, b_ada):
    n = w_ada.shape[1]
    tn = 1536
    return pl.pallas_call(
        _ada_kernel,
        grid=(n // tn,),
        in_specs=[pl.BlockSpec((8, D_MODEL), lambda j: (0, 0)),
                  pl.BlockSpec((D_MODEL, tn), lambda j: (0, j)),
                  pl.BlockSpec((1, tn), lambda j: (0, j))],
        out_specs=pl.BlockSpec((8, tn), lambda j: (0, j)),
        out_shape=jax.ShapeDtypeStruct((8, n), F32),
        compiler_params=_cparams(("arbitrary",), VMEM_LIMIT),
        name="ada",
    )(cond8, w_ada, b_ada.reshape(1, n))


def _mod_spec(col, tm, row0, rows_per_cond):
    return pl.BlockSpec((1, 1, D_MODEL), lambda i, *_: (row0 + (i * tm) // rows_per_cond, 0, col))


def _prenorm_kernel(x_ref, sh_ref, sc_ref, g_ref, wt_ref, o_ref, dtt_ref):
    xn = _rms(x_ref[...], g_ref[...])
    h = (xn * (1.0 + sc_ref[0]) + sh_ref[0]).astype(BF16)
    o_ref[...] = h
    dtt_ref[...] = _dot_nt(wt_ref[...], h)


def _prenorm(x, mod3, gain, w_dt_t, row0, rows_per_cond):
    t = x.shape[0]
    tm = 1024
    return pl.pallas_call(
        _prenorm_kernel,
        grid=(t // tm,),
        in_specs=[pl.BlockSpec((tm, D_MODEL), lambda i: (i, 0)),
                  _mod_spec(0, tm, row0, rows_per_cond),
                  _mod_spec(1, tm, row0, rows_per_cond),
                  pl.BlockSpec((1, D_MODEL), lambda i: (0, 0)),
                  pl.BlockSpec((LANES, D_MODEL), lambda i: (0, 0))],
        out_specs=[pl.BlockSpec((tm, D_MODEL), lambda i: (i, 0)),
                   pl.BlockSpec((LANES, tm), lambda i: (0, i))],
        out_shape=[jax.ShapeDtypeStruct((t, D_MODEL), BF16), jax.ShapeDtypeStruct((LANES, t), F32)],
        compiler_params=_cparams(("parallel",), VMEM_LIMIT),
        name="prenorm",
    )(x, mod3, mod3, gain.reshape(1, D_MODEL), w_dt_t)


INPROJ_TM = 2048
INPROJ_TN = 512


def _conv_silu(x, w, b, n):
    rows = x.shape[0]
    pos = lax.broadcasted_iota(I32, x.shape, 0) % n
    prev = jnp.where(pos == 0, 0.0, pltpu.roll(x, 1, axis=0))
    nxt = jnp.where(pos == n - 1, 0.0, pltpu.roll(x, rows - 1, axis=0))
    return _silu(prev * w[0:1] + x * w[1:2] + nxt * w[2:3] + b)


def _inproj_kernel(blk_ref, nxt_ref, hc_ref, hl_ref, w_ref, wn_ref, cw_ref, cb_ref, oc_ref, ol_ref, wbf_ref,
                   *, sh_lo, sh_hi, shift, n_ctx_tiles, cv_lo, cv_hi, n_c, n_l):
    del blk_ref, nxt_ref
    j = pl.program_id(0)
    i = pl.program_id(1)
    shifted = jnp.logical_and(j >= sh_lo, j < sh_hi)
    conv = jnp.logical_and(j >= cv_lo, j < cv_hi)

    if shift:
        @pl.when(jnp.logical_and(i == 0, shifted))
        def _():
            w = jnp.concatenate([w_ref[shift:, :], wn_ref[:shift, :]], axis=0)
            wbf_ref[...] = w.T.astype(BF16)

    @pl.when(jnp.logical_and(i == 0, jnp.logical_not(shifted)))
    def _():
        wbf_ref[...] = w_ref[...].T.astype(BF16)

    def ctx_rows():
        tm = oc_ref.shape[0]
        return pl.ds(pl.multiple_of(jnp.minimum(i, n_ctx_tiles - 1) * tm, tm), tm)

    is_ctx = i < n_ctx_tiles
    plain = jnp.logical_not(conv)

    @pl.when(jnp.logical_and(is_ctx, plain))
    def _():
        oc_ref[...] = _dot(hc_ref[ctx_rows(), :], wbf_ref[...]).astype(oc_ref.dtype)

    @pl.when(jnp.logical_and(jnp.logical_not(is_ctx), plain))
    def _():
        ol_ref[...] = _dot(hl_ref[...], wbf_ref[...]).astype(ol_ref.dtype)

    if cv_hi > cv_lo:
        @pl.when(jnp.logical_and(is_ctx, conv))
        def _():
            y = _dot(hc_ref[ctx_rows(), :], wbf_ref[...])
            oc_ref[...] = _conv_silu(y, cw_ref[...], cb_ref[...], n_c).astype(oc_ref.dtype)

        @pl.when(jnp.logical_and(jnp.logical_not(is_ctx), conv))
        def _():
            y = _dot(hl_ref[...], wbf_ref[...])
            ol_ref[...] = _conv_silu(y, cw_ref[...], cb_ref[...], n_l).astype(ol_ref.dtype)


def _inproj(h_c, h_l, w_t, w_rows, out_dtype, conv_w, conv_b, seq_lens, shifted_tiles=(0, 0), conv_tiles=(0, 0)):
    tm, tn = INPROJ_TM, INPROJ_TN
    t_c, k = h_c.shape
    t_l = h_l.shape[0]
    assert t_c % tm == 0 and t_l == tm
    nct = t_c // tm
    ncol = len(w_rows)
    sh_lo, sh_hi = shifted_tiles
    shifts = {w_rows[j] % tn for j in range(sh_lo, sh_hi)}
    assert len(shifts) <= 1 and all(w_rows[j] % tn == 0 for j in range(ncol) if not sh_lo <= j < sh_hi)
    shift = shifts.pop() if shifts else 0
    assert shift % 8 == 0
    blocks = [r // tn for r in w_rows]
    nxt = [b + 1 if sh_lo <= j < sh_hi else blocks[sh_lo] if sh_hi > sh_lo else 0 for j, b in enumerate(blocks)]
    cv_lo, cv_hi = conv_tiles
    assert conv_w.shape[1] >= (cv_hi - cv_lo) * tn
    conv_col = lambda j, i, blk, nx: (0, jnp.clip(j - cv_lo, 0, max(cv_hi - cv_lo - 1, 0)))
    return pl.pallas_call(
        functools.partial(_inproj_kernel, sh_lo=sh_lo, sh_hi=sh_hi, shift=shift, n_ctx_tiles=nct,
                          cv_lo=cv_lo, cv_hi=cv_hi, n_c=seq_lens[0], n_l=seq_lens[1]),
        grid_spec=pltpu.PrefetchScalarGridSpec(
            num_scalar_prefetch=2,
            grid=(ncol, nct + 1),
            in_specs=[pl.BlockSpec((t_c, k), lambda j, i, blk, nx: (0, 0), pipeline_mode=pl.Buffered(1)),
                      pl.BlockSpec((t_l, k), lambda j, i, blk, nx: (0, 0), pipeline_mode=pl.Buffered(1)),
                      pl.BlockSpec((tn, k), lambda j, i, blk, nx: (blk[j], 0)),
                      pl.BlockSpec((tn, k), lambda j, i, blk, nx: (nx[j], 0)),
                      pl.BlockSpec((3, tn), conv_col),
                      pl.BlockSpec((1, tn), conv_col)],
            out_specs=[pl.BlockSpec((tm, tn), lambda j, i, blk, nx: (jnp.minimum(i, nct - 1), j)),
                       pl.BlockSpec((tm, tn), lambda j, i, blk, nx: (0, j))],
            scratch_shapes=[pltpu.VMEM((k, tn), BF16)]),
        out_shape=[jax.ShapeDtypeStruct((t_c, ncol * tn), out_dtype),
                   jax.ShapeDtypeStruct((t_l, ncol * tn), BF16)],
        compiler_params=_cparams(("arbitrary", "arbitrary"), VMEM_LIMIT),
        name="inproj",
    )(jnp.asarray(blocks, I32), jnp.asarray(nxt, I32), h_c, h_l, w_t, w_t, conv_w, conv_b)


def _lam(lp_ref, lam_init):
    lp = lp_ref[...]
    s1 = jnp.sum(lp[0:1] * lp[1:2], axis=-1, keepdims=True)
    s2 = jnp.sum(lp[2:3] * lp[3:4], axis=-1, keepdims=True)
    return jnp.exp(s1) - jnp.exp(s2) + lam_init


ATTN_TQ = 64


def _with_ones(v):
    return jnp.concatenate([v, jnp.ones(v.shape, v.dtype)], axis=1)


def _diff_attn(q, k, v1, lam, wsub, lam_init):
    nq = q.shape[0]
    lane = lax.broadcasted_iota(I32, q.shape, 1)
    q = q * (A_HEAD_DIM ** -0.5 * LOG2E)
    q1 = jnp.where(lane < A_HEAD_DIM, q, 0.0).astype(BF16)
    q2 = jnp.where(lane >= A_HEAD_DIM, q, 0.0).astype(BF16)
    s = _dot_nt(jnp.concatenate([q1, q2], axis=0), k)
    p = jnp.exp2(s - jnp.max(s, axis=-1, keepdims=True)).astype(BF16)
    pv = _dot(p, v1)
    pv = pv[:, :A_V_DIM] / pv[:, A_V_DIM:]
    o = pv[:nq] - lam * pv[nq:]
    return _rms(o, wsub) * (1.0 - lam_init)


def _attn_ctx_kernel(lp_ref, q_ref, k_ref, v_ref, ws_ref, o_ref, *, lam_init, tq):
    lam = _lam(lp_ref, lam_init)
    for h in range(A_HEADS):
        sl = slice(h * LANES, (h + 1) * LANES)
        k = k_ref[:, sl].astype(BF16)
        v1 = _with_ones(v_ref[:, sl].astype(BF16))
        for i in range(q_ref.shape[0] // tq):
            rows = slice(i * tq, (i + 1) * tq)
            o = _diff_attn(q_ref[rows, sl].astype(F32), k, v1, lam, ws_ref[...], lam_init)
            o_ref[rows, sl] = o.astype(o_ref.dtype)


def _attn_ctx(lam_p, proj, k, v, wsub, nbatch, n, lam_init):
    w = A_HEADS * LANES
    blk = pl.BlockSpec((n, w), lambda b: (b, 0))
    return pl.pallas_call(
        functools.partial(_attn_ctx_kernel, lam_init=lam_init, tq=ATTN_TQ),
        grid=(nbatch,),
        in_specs=[pl.BlockSpec((4, A_HEAD_DIM), lambda b: (0, 0)),
                  pl.BlockSpec((n, w), lambda b: (b, PROJ_Q // w)), blk, blk,
                  pl.BlockSpec((1, A_V_DIM), lambda b: (0, 0))],
        out_specs=blk,
        out_shape=jax.ShapeDtypeStruct(k.shape, BF16),
        compiler_params=_cparams(("parallel",), VMEM_LIMIT),
        name="attn_ctx",
    )(lam_p, proj, k, v, wsub)


def _rope(x, cos, sin_signed):
    lane = lax.broadcasted_iota(I32, x.shape, 1)
    quarter = A_HEAD_DIM // 4
    partner = jnp.where((lane % (2 * quarter)) < quarter,
                        pltpu.roll(x, LANES - quarter, axis=1),
                        pltpu.roll(x, quarter, axis=1))
    return x * cos + partner * sin_signed


def _attn_lat_kernel(lp_ref, q_ref, k_ref, v_ref, kc_ref, vc_ref, cos_ref, sin_ref, ws_ref, o_ref, *, lam_init, tq):
    lam = _lam(lp_ref, lam_init)
    k = _rope(k_ref[...].astype(F32), cos_ref[...], sin_ref[...])
    k_all = jnp.concatenate([kc_ref[...].astype(BF16), k.astype(BF16)], axis=0)
    v_all = _with_ones(jnp.concatenate([vc_ref[...].astype(BF16), v_ref[...].astype(BF16)], axis=0))
    for i in range(q_ref.shape[0] // tq):
        rows = slice(i * tq, (i + 1) * tq)
        q = _rope(q_ref[rows, :].astype(F32), cos_ref[rows, :], sin_ref[rows, :])
        o_ref[rows, :] = _diff_attn(q, k_all, v_all, lam, ws_ref[...], lam_init).astype(o_ref.dtype)


def _attn_lat(lam_p, proj, k, v, kc, vc, cos, sin, wsub, nbatch, n, npast, lam_init):
    blk = pl.BlockSpec((n, LANES), lambda b, h: (b, h))
    cblk = pl.BlockSpec((npast, LANES), lambda b, h: (b, h))
    tab = pl.BlockSpec((n, LANES), lambda b, h: (0, 0))
    return pl.pallas_call(
        functools.partial(_attn_lat_kernel, lam_init=lam_init, tq=ATTN_TQ),
        grid=(nbatch, A_HEADS),
        in_specs=[pl.BlockSpec((4, A_HEAD_DIM), lambda b, h: (0, 0)),
                  pl.BlockSpec((n, LANES), lambda b, h: (b, PROJ_Q // LANES + h)), blk, blk, cblk, cblk, tab, tab,
                  pl.BlockSpec((1, A_V_DIM), lambda b, h: (0, 0))],
        out_specs=blk,
        out_shape=jax.ShapeDtypeStruct(k.shape, BF16),
        compiler_params=_cparams(("parallel", "parallel"), VMEM_LIMIT),
        name="attn_lat",
    )(lam_p, proj, k, v, kc, vc, cos, sin, wsub)


def _rope_tables(n):
    pos = np.arange(n)
    rowcol = np.stack([pos // GRID_W, pos % GRID_W], axis=-1).astype(np.float64)
    nf = A_HEAD_DIM // 4
    inv_freq = np.power(ROPE_BASE, -np.arange(nf, dtype=np.float64) / nf)
    lane = np.arange(LANES)
    axis = (lane % A_HEAD_DIM) // (2 * nf)
    ang = rowcol[:, axis] * inv_freq[lane % nf][None, :]
    sign = np.where((lane % (2 * nf)) < nf, -1.0, 1.0)
    return jnp.asarray(np.cos(ang), F32), jnp.asarray(np.sin(ang) * sign[None, :], F32)


S_PAIRS = S_INNER // LANES
SSD_ROWS = 64


def _ssd_direction(xa_ref, dt_raw_t, dtb_col, alog_col, ht_ref, y_ref, backward):
    col0 = S_HEADS if backward else 0
    row = lax.broadcasted_iota(I32, (S_CHUNK, S_CHUNK), 0)
    col = lax.broadcasted_iota(I32, (S_CHUNK, S_CHUNK), 1)
    dt_t = jax.nn.softplus(dt_raw_t + dtb_col)
    da_t = dt_t * (-jnp.exp(alog_col) * LOG2E)
    if backward:
        last = 0
        keep = row <= col
    else:
        last = S_CHUNK - 1
        keep = row >= col
    tri = jnp.where(keep, 1.0, 0.0).astype(BF16)
    pieces = _split3(da_t)
    cum = sum(_dot_nt(tri, p) for p in pieces[1:]) + _dot_nt(tri, pieces[0])
    cum_t = sum(_dot_nt(p, tri) for p in pieces[1:]) + _dot_nt(pieces[0], tri)
    w_t = dt_t * jnp.exp2(cum_t[:, last:last + 1] - cum_t)
    e_tot = jnp.exp2(cum[last:last + 1, :])
    lo_half = col < S_HEADDIM
    lo_bf = jnp.where(lo_half, 1.0, 0.0).astype(BF16)
    hi_bf = jnp.where(lo_half, 0.0, 1.0).astype(BF16)
    for g in range(S_GROUPS):
        b_bf = xa_ref[:, S_INNER + g * S_STATE:S_INNER + (g + 1) * S_STATE]
        c_bf = xa_ref[:, S_INNER + (S_GROUPS + g) * S_STATE:S_INNER + (S_GROUPS + g + 1) * S_STATE]
        cb = _dot_nt(c_bf, b_bf)
        bt = b_bf.astype(F32).T
        c_f = c_bf.astype(F32)
        for pr in range(S_PAIRS // S_GROUPS):
            hp = g * (S_PAIRS // S_GROUPS) + pr
            c0 = col0 + 2 * hp
            x_pair = xa_ref[:, hp * LANES:(hp + 1) * LANES]
            bd_x = jnp.concatenate([x_pair * lo_bf, x_pair * hi_bf], axis=0)
            ht = ht_ref[hp]
            bd_h = jnp.concatenate([jnp.where(lo_half, ht, 0.0), jnp.where(lo_half, 0.0, ht)], axis=0).astype(BF16)
            rhs_y = jnp.concatenate([bd_x, bd_h], axis=0)
            bws = [(bt * w_t[cc:cc + 1, :]).astype(BF16) for cc in (c0, c0 + 1)]
            for lh in range(S_CHUNK // SSD_ROWS):
                rows = slice(lh * SSD_ROWS, (lh + 1) * SSD_ROWS)
                ms, ces = [], []
                for cc in (c0, c0 + 1):
                    bc = jnp.broadcast_to(cum[rows, cc:cc + 1], (SSD_ROWS, S_CHUNK))
                    decay = jnp.exp2(jnp.where(keep[rows], bc - cum_t[cc:cc + 1, :], -1e30))
                    ms.append((cb[rows] * decay * dt_t[cc:cc + 1, :]).astype(BF16))
                    ces.append((c_f[rows] * jnp.exp2(bc)).astype(BF16))
                y_ref[rows, hp * LANES:(hp + 1) * LANES] = _dot(jnp.concatenate(ms + ces, axis=1),
                                                                rhs_y).astype(y_ref.dtype)
            tot = jnp.where(lo_half[0:1], e_tot[:, c0:c0 + 1], e_tot[:, c0 + 1:c0 + 2])
            ht_ref[hp] = ht * tot + _dot(jnp.concatenate(bws, axis=1), bd_x)


def _ssd_kernel(*refs, has_h0, emit_final):
    xf_ref, xb_ref, dtft_ref, dtbt_ref, bc_ref, ac_ref = refs[:6]
    refs = refs[6:]
    if has_h0:
        h0_ref, refs = refs[0], refs[1:]
    yf_ref, yb_ref = refs[:2]
    refs = refs[2:]
    if emit_final:
        fin_ref, refs = refs[0], refs[1:]
    htf_ref, htb_ref = refs
    c = pl.program_id(1)
    nc = pl.num_programs(1)

    @pl.when(c == 0)
    def _():
        for d, ht_ref in enumerate((htf_ref, htb_ref)):
            for hp in range(S_PAIRS):
                if has_h0:
                    ht_ref[hp] = h0_ref[0, d, hp * LANES:(hp + 1) * LANES, :].T
                else:
                    ht_ref[hp] = jnp.zeros((S_STATE, LANES), F32)

    _ssd_direction(xf_ref, dtft_ref[...], bc_ref[...], ac_ref[...], htf_ref, yf_ref, backward=False)
    _ssd_direction(xb_ref, dtbt_ref[...], bc_ref[...], ac_ref[...], htb_ref, yb_ref, backward=True)

    if emit_final:
        @pl.when(c == nc - 1)
        def _():
            for d, ht_ref in enumerate((htf_ref, htb_ref)):
                for hp in range(S_PAIRS):
                    fin_ref[0, d, hp * LANES:(hp + 1) * LANES, :] = ht_ref[hp].T


def _ssd(xa, dt_t, dtb_col, alog_col, h0, nbatch, n, emit_final):
    nc = n // S_CHUNK
    t = xa.shape[0]
    fwd = lambda b, c: (b * nc + c, 0)
    bwd = lambda b, c: (b * nc + nc - 1 - c, 0)
    const = lambda b, c: (0, 0)
    in_specs = [pl.BlockSpec((S_CHUNK, S_CONV_CH), fwd),
                pl.BlockSpec((S_CHUNK, S_CONV_CH), bwd),
                pl.BlockSpec((LANES, S_CHUNK), lambda b, c: (0, b * nc + c)),
                pl.BlockSpec((LANES, S_CHUNK), lambda b, c: (0, b * nc + nc - 1 - c)),
                pl.BlockSpec((LANES, 1), const), pl.BlockSpec((LANES, 1), const)]
    args = [xa, xa, dt_t, dt_t, dtb_col, alog_col]
    if h0 is not None:
        in_specs.append(pl.BlockSpec((1, 2, S_INNER, S_STATE), lambda b, c: (b, 0, 0, 0)))
        args.append(h0)
    out_specs = [pl.BlockSpec((S_CHUNK, S_INNER), fwd), pl.BlockSpec((S_CHUNK, S_INNER), bwd)]
    out_shape = [jax.ShapeDtypeStruct((t, S_INNER), BF16), jax.ShapeDtypeStruct((t, S_INNER), BF16)]
    if emit_final:
        out_specs.append(pl.BlockSpec((1, 2, S_INNER, S_STATE), lambda b, c: (b, 0, 0, 0)))
        out_shape.append(jax.ShapeDtypeStruct((nbatch, 2, S_INNER, S_STATE), F32))
    return pl.pallas_call(
        functools.partial(_ssd_kernel, has_h0=h0 is not None, emit_final=emit_final),
        grid=(nbatch, nc),
        in_specs=in_specs,
        out_specs=out_specs,
        out_shape=out_shape,
        scratch_shapes=[pltpu.VMEM((S_PAIRS, S_STATE, LANES), F32),
                        pltpu.VMEM((S_PAIRS, S_STATE, LANES), F32)],
        compiler_params=_cparams(("parallel", "arbitrary"), VMEM_LIMIT),
        name="ssd",
    )(*args)


MIXOUT_SPLIT = 2


def _mixout_kernel(yf_ref, yb_ref, xs_ref, z_ref, oa_ref, ga_ref, gs_ref, x_ref, g1_ref, sh2_ref, sc2_ref,
                   dskip_ref, wn_ref, woa_ref, wos_ref, wout_ref, gpost_ref, gpre_ref, wr_ref,
                   x1_ref, h2_ref, aff_ref):
    wn = wn_ref[...]
    sub = x_ref.shape[0] // MIXOUT_SPLIT
    for r in range(MIXOUT_SPLIT):
        rows = slice(r * sub, (r + 1) * sub)
        y = (yf_ref[rows, :].astype(F32) + yb_ref[rows, :].astype(F32)
             + xs_ref[rows, :].astype(F32) * dskip_ref[...])
        y = y * _silu(z_ref[rows, :].astype(F32))
        parts = []
        for g in range(S_GROUPS):
            sl = slice(g * S_GROUP_W, (g + 1) * S_GROUP_W)
            parts.append(_rms(y[:, sl], wn[:, sl]).astype(BF16))
        o_ssd = jnp.concatenate(parts, axis=1)
        a = _dot(oa_ref[rows, :], woa_ref[...])
        s = _dot(o_ssd, wos_ref[...])
        merged = (jax.nn.sigmoid(ga_ref[rows, :].astype(F32)) * a
                  + jax.nn.sigmoid(gs_ref[rows, :].astype(F32)) * s)
        ym = _dot(merged.astype(BF16), wout_ref[...])
        x1 = x_ref[rows, :] + g1_ref[0] * _rms(ym, gpost_ref[...])
        x1_ref[rows, :] = x1
        h2 = _rms(x1, gpre_ref[...]) * (1.0 + sc2_ref[0]) + sh2_ref[0]
        h2_ref[rows, :] = h2.astype(BF16)
        logits = _dot_hi(h2, wr_ref[...])
        lane = lax.broadcasted_iota(I32, logits.shape, 1)
        logits = jnp.where(lane < N_EXPERTS, logits, -1e30)
        p = jnp.exp(logits - jnp.max(logits, axis=-1, keepdims=True))
        aff_ref[rows, :] = p / jnp.sum(p, axis=-1, keepdims=True)


def _mixout(yf, yb, xa, proj, o_attn, x, mod3, dskip, wn, woa, wos, wout, gpost, gpre, wr, row0, rows_per_cond):
    t = x.shape[0]
    tm = 512
    tok = lambda w: pl.BlockSpec((tm, w), lambda i: (i, 0))
    const = lambda a: pl.BlockSpec(a.shape, lambda i: (0,) * a.ndim, pipeline_mode=pl.Buffered(1))
    return pl.pallas_call(
        _mixout_kernel,
        grid=(t // tm,),
        in_specs=[tok(S_INNER), tok(S_INNER), tok(S_INNER),
                  pl.BlockSpec((tm, S_INNER), lambda i: (i, PROJ_Z // S_INNER)), tok(D_MODEL),
                  pl.BlockSpec((tm, D_MODEL), lambda i: (i, PROJ_GATES // D_MODEL)),
                  pl.BlockSpec((tm, D_MODEL), lambda i: (i, PROJ_GATES // D_MODEL + 1)),
                  tok(D_MODEL),
                  _mod_spec(2, tm, row0, rows_per_cond),
                  _mod_spec(3, tm, row0, rows_per_cond),
                  _mod_spec(4, tm, row0, rows_per_cond),
                  const(dskip), const(wn), const(woa), const(wos), const(wout), const(gpost), const(gpre),
                  const(wr)],
        out_specs=[tok(D_MODEL), tok(D_MODEL), tok(LANES)],
        out_shape=[jax.ShapeDtypeStruct((t, D_MODEL), F32), jax.ShapeDtypeStruct((t, D_MODEL), BF16),
                   jax.ShapeDtypeStruct((t, LANES), F32)],
        compiler_params=_cparams(("parallel",), VMEM_LIMIT),
        name="mixout",
    )(yf, yb, xa, proj, o_attn, proj, proj, x, mod3, mod3, mod3, dskip, wn, woa, wos, wout, gpost, gpre, wr)


COMB_TILE = 256
COMB_WIN = 64
BF16_ROWS = 16


def _router_kernel(aff_ref, pos_ref, cnt_ref, *, cap):
    aff = aff_ref[...]
    e, t = aff.shape
    idx = lax.broadcasted_iota(I32, (e, t), 1)

    def count(m):
        return jnp.sum(jnp.where(m, 1.0, 0.0), axis=-1, keepdims=True)

    def value_bit(i, thr):
        cand = thr | jnp.left_shift(jnp.int32(1), 30 - i)
        return jnp.where(count(aff >= pltpu.bitcast(cand, F32)) >= cap, cand, thr)

    thr = pltpu.bitcast(lax.fori_loop(0, 31, value_bit, jnp.zeros((e, 1), I32)), F32)
    gt = aff > thr
    eq = aff == thr
    need = cap - count(gt)
    nbits = (t - 1).bit_length()

    def index_bit(i, v):
        cand = v | jnp.left_shift(jnp.int32(1), nbits - 1 - i)
        return jnp.where(count(jnp.logical_and(eq, idx < cand)) < need, cand, v)

    last = lax.fori_loop(0, nbits, index_bit, jnp.zeros((e, 1), I32))
    sel = jnp.logical_or(gt, jnp.logical_and(eq, idx <= last))
    sel_bf = jnp.where(sel, 1.0, 0.0).astype(BF16)
    blk = 512
    r = lax.broadcasted_iota(I32, (blk, blk), 0)
    c = lax.broadcasted_iota(I32, (blk, blk), 1)
    upper = jnp.where(r <= c, 1.0, 0.0).astype(BF16)
    carry = jnp.zeros((e, 1), F32)
    tile_cnt = []
    for b in range(t // blk):
        cum = _dot(sel_bf[:, b * blk:(b + 1) * blk], upper) + carry
        carry = cum[:, blk - 1:blk]
        slot = jnp.where(sel[:, b * blk:(b + 1) * blk], cum - 1.0, -1.0)
        pos_ref[:, b * blk:(b + 1) * blk] = slot.astype(I32)
        for j in range(blk // COMB_TILE):
            tile_cnt.append(cum[:, (j + 1) * COMB_TILE - 1:(j + 1) * COMB_TILE])
    lane = lax.broadcasted_iota(I32, (e, LANES), 1)
    cnt = jnp.zeros((e, LANES), F32)
    for i, v in enumerate(tile_cnt):
        cnt = jnp.where(lane == i, v, cnt)
    cnt_ref[...] = cnt.astype(I32)


class _Route(NamedTuple):
    pos: jax.Array
    start: jax.Array
    end: jax.Array
    win: jax.Array
    fits: jax.Array


def _router(aff_t, cap):
    e, t = aff_t.shape
    ntile = t // COMB_TILE
    pos, cnt = pl.pallas_call(
        functools.partial(_router_kernel, cap=cap),
        out_shape=[jax.ShapeDtypeStruct((e, t), I32), jax.ShapeDtypeStruct((e, LANES), I32)],
        name="router",
    )(aff_t)
    end = cnt[:, :ntile]
    start = jnp.concatenate([jnp.zeros((e, 1), I32), end[:, :-1]], axis=1)
    win = jnp.minimum(start // BF16_ROWS * BF16_ROWS, cap - COMB_WIN)
    fits = jnp.all(end - win <= COMB_WIN, axis=0).astype(I32)
    flat = lambda a: a.T.reshape(ntile * e)
    return _Route(pos, flat(start), flat(end), flat(win), fits)


def _dispatch_kernel(start_ref, end_ref, win_ref, fits_ref, pos_ref, h_ref, o_ref):
    i = pl.program_id(0)
    ne, cap, _ = o_ref.shape
    tm = h_ref.shape[0]

    @pl.when(i == 0)
    def _():
        o_ref[...] = jnp.zeros(o_ref.shape, o_ref.dtype)

    h = h_ref[...]
    row = lax.broadcasted_iota(I32, (COMB_WIN, tm), 0)
    row_l = lax.broadcasted_iota(I32, (COMB_WIN, LANES), 0)

    def place(e, r0, rows):
        rel = r0 - start_ref[i * ne + e] + row_l
        mine = jnp.logical_and(rel >= 0, rel < end_ref[i * ne + e] - start_ref[i * ne + e])
        for c in range(D_MODEL // LANES):
            sl = slice(c * LANES, (c + 1) * LANES)
            old = o_ref[e, pl.ds(r0, COMB_WIN), sl].astype(F32)
            o_ref[e, pl.ds(r0, COMB_WIN), sl] = jnp.where(mine, rows[:, sl], old).astype(o_ref.dtype)

    @pl.when(fits_ref[i] == 1)
    def _():
        sel = [jnp.where(pos_ref[e:e + 1, :] == win_ref[i * ne + e] + row, 1.0, 0.0).astype(BF16) for e in range(ne)]
        rows = _dot(jnp.concatenate(sel, axis=0), h)
        for e in range(ne):
            place(e, pl.multiple_of(win_ref[i * ne + e], BF16_ROWS), rows[e * COMB_WIN:(e + 1) * COMB_WIN])

    @pl.when(fits_ref[i] == 0)
    def _():
        for e in range(ne):
            w0 = win_ref[i * ne + e]

            def window(w, carry):
                r0 = pl.multiple_of(jnp.minimum(w0 + w * COMB_WIN, cap - COMB_WIN), BF16_ROWS)
                sel = jnp.where(pos_ref[e:e + 1, :] == r0 + row, 1.0, 0.0).astype(BF16)
                place(e, r0, _dot(sel, h))
                return carry

            lax.fori_loop(0, (end_ref[i * ne + e] - w0 + COMB_WIN - 1) // COMB_WIN, window, 0)


def _dispatch(route, h2, cap):
    e, t = route.pos.shape
    tm = COMB_TILE
    return pl.pallas_call(
        _dispatch_kernel,
        grid_spec=pltpu.PrefetchScalarGridSpec(
            num_scalar_prefetch=4,
            grid=(t // tm,),
            in_specs=[pl.BlockSpec((e, tm), lambda i, *_: (0, i)),
                      pl.BlockSpec((tm, D_MODEL), lambda i, *_: (i, 0))],
            out_specs=pl.BlockSpec((e, cap, D_MODEL), lambda i, *_: (0, 0, 0))),
        out_shape=jax.ShapeDtypeStruct((e, cap, D_MODEL), BF16),
        compiler_params=_cparams(("arbitrary",), VMEM_LIMIT),
        name="dispatch",
    )(route.start, route.end, route.win, route.fits, route.pos, h2)


def _ffn_kernel(xc_ref, xl_ref, wg_ref, wu_ref, wd_ref, oc_ref, ol_ref, acc_ref):
    f = pl.program_id(1)
    cc = xc_ref.shape[1]
    xe = jnp.concatenate([xc_ref[0], xl_ref[0]], axis=0)
    hg = _dot(xe, wg_ref[0].astype(BF16))
    hu = _dot(xe, wu_ref[0].astype(BF16))
    part = _dot((_silu(hg) * hu).astype(BF16), wd_ref[0].astype(BF16))

    @pl.when(f == 0)
    def _():
        acc_ref[...] = part

    @pl.when(f > 0)
    def _():
        acc_ref[...] += part

    @pl.when(f == pl.num_programs(1) - 1)
    def _():
        oc_ref[0] = acc_ref[:cc, :].astype(BF16)
        ol_ref[0] = acc_ref[cc:, :].astype(BF16)


def _ffn(xe_c, xe_l, w_gate, w_up, w_down):
    e, cc, d = xe_c.shape
    cl = xe_l.shape[1]
    tf = 1024
    return pl.pallas_call(
        _ffn_kernel,
        grid=(e, D_FF // tf),
        in_specs=[pl.BlockSpec((1, cc, d), lambda i, f: (i, 0, 0)),
                  pl.BlockSpec((1, cl, d), lambda i, f: (i, 0, 0)),
                  pl.BlockSpec((1, d, tf), lambda i, f: (i, 0, f)),
                  pl.BlockSpec((1, d, tf), lambda i, f: (i, 0, f)),
                  pl.BlockSpec((1, tf, d), lambda i, f: (i, f, 0))],
        out_specs=[pl.BlockSpec((1, cc, d), lambda i, f: (i, 0, 0)),
                   pl.BlockSpec((1, cl, d), lambda i, f: (i, 0, 0))],
        out_shape=[jax.ShapeDtypeStruct((e, cc, d), BF16), jax.ShapeDtypeStruct((e, cl, d), BF16)],
        scratch_shapes=[pltpu.VMEM((cc + cl, d), F32)],
        compiler_params=_cparams(("parallel", "arbitrary"), VMEM_LIMIT),
        name="ffn",
    )(xe_c, xe_l, w_gate, w_up, w_down)


def _combine_kernel(win_ref, fits_ref, pos_ref, aff_ref, ye_ref, x1_ref, g2_ref, gp_ref, o_ref, rhs_ref, y_ref):
    i = pl.program_id(0)
    tm = pos_ref.shape[0]
    cap = ye_ref.shape[1]
    pos = pos_ref[...]
    aff = aff_ref[...]

    @pl.when(fits_ref[i] == 1)
    def _():
        per_col = LANES // COMB_WIN
        lane = lax.broadcasted_iota(I32, (tm, LANES), 1)
        cols = []
        for c in range(N_EXPERTS // per_col):
            col = jnp.zeros((tm, LANES), F32)
            for j in range(per_col):
                e = c * per_col + j
                w0 = pl.multiple_of(win_ref[i * N_EXPERTS + e], BF16_ROWS)
                rhs_ref[e * COMB_WIN:(e + 1) * COMB_WIN, :] = ye_ref[e, pl.ds(w0, COMB_WIN), :]
                p = pos[:, e:e + 1]
                tgt = jnp.where(p >= 0, p - w0 + j * COMB_WIN, -1)
                col = jnp.where(lane == tgt, aff[:, e:e + 1], col)
            cols.append(col)
        sel = jnp.concatenate(cols, axis=1)
        hi = sel.astype(BF16)
        lo = (sel - hi.astype(F32)).astype(BF16)
        both = _dot(jnp.concatenate([hi, lo], axis=0), rhs_ref[...])
        y_ref[...] = both[:tm] + both[tm:]

    @pl.when(fits_ref[i] == 0)
    def _():
        slot = lax.broadcasted_iota(I32, (tm, cap), 1)
        y = jnp.zeros((tm, D_MODEL), F32)
        for e in range(N_EXPERTS):
            onehot = jnp.where(pos[:, e:e + 1] == slot, 1.0, 0.0).astype(BF16)
            y = y + aff[:, e:e + 1] * _dot(onehot, ye_ref[e])
        y_ref[...] = y

    o_ref[...] = x1_ref[...] + g2_ref[0] * _rms(y_ref[...], gp_ref[...])


def _combine(route, aff, ye, x1, mod3, gpost, row0, rows_per_cond):
    t = x1.shape[0]
    pos_t, win, fits = route.pos.T, route.win, route.fits
    e, cap, d = ye.shape
    tm = COMB_TILE
    return pl.pallas_call(
        _combine_kernel,
        grid_spec=pltpu.PrefetchScalarGridSpec(
            num_scalar_prefetch=2,
            grid=(t // tm,),
            in_specs=[pl.BlockSpec((tm, e), lambda i, *_: (i, 0)),
                      pl.BlockSpec((tm, LANES), lambda i, *_: (i, 0)),
                      pl.BlockSpec((e, cap, d), lambda i, *_: (0, 0, 0)),
                      pl.BlockSpec((tm, d), lambda i, *_: (i, 0)),
                      _mod_spec(5, tm, row0, rows_per_cond),
                      pl.BlockSpec((1, d), lambda i, *_: (0, 0))],
            out_specs=pl.BlockSpec((tm, d), lambda i, *_: (i, 0)),
            scratch_shapes=[pltpu.VMEM((e * COMB_WIN, d), BF16), pltpu.VMEM((tm, d), F32)]),
        out_shape=jax.ShapeDtypeStruct((t, d), F32),
        compiler_params=_cparams(("parallel",), VMEM_LIMIT),
        name="combine",
    )(win, fits, pos_t, aff, ye, x1, mod3, gpost.reshape(1, d))


def _head_col(v):
    return jnp.concatenate([v.reshape(2 * S_HEADS), jnp.zeros((LANES - 2 * S_HEADS,), F32)]).reshape(LANES, 1)


def kernel(x_prompt, x_sample, cache_k, cache_v, state_ssd, c, c_ctx, w_ada, b_ada, g_pre_mix, g_post_mix, g_pre_ffn, g_post_ffn, w_in, lam_q1, lam_k1, lam_q2, lam_k2, w_subln, conv_w, conv_b, dt_bias, a_log, d_skip, w_ssd_norm, w_o_attn, w_o_ssd, w_out, w_router, w_gate, w_up, w_down):
    nb_c, n_c, d = x_prompt.shape
    nb_l, n_l, _ = x_sample.shape
    npast = cache_k.shape[2]
    depth = w_in.shape[0]
    assert depth == 1 and d == D_MODEL
    t_c, t_l = nb_c * n_c, nb_l * n_l
    lyr = 0
    lam_init = 0.8 - 0.6 * math.exp(-0.3 * lyr)

    cond8 = jnp.concatenate([c_ctx[None, :], c, jnp.zeros((8 - 1 - nb_l, d), F32)], axis=0)
    mod3 = _ada(cond8, w_ada[lyr], b_ada[lyr]).reshape(8, 1, N_MOD * d)

    w_in_t = w_in[lyr].T
    w_dt_t = jnp.concatenate([w_in_t[IN_DT:IN_GATES], jnp.zeros((LANES - 2 * S_HEADS, d), F32)], axis=0).astype(BF16)
    tn = INPROJ_TN
    rows_of = lambda col0, width: [col0 + j * tn for j in range(width // tn)]
    proj_rows = rows_of(IN_XBC, S_CONV_CH) + rows_of(IN_Q, d) + rows_of(IN_Z, S_INNER) + rows_of(IN_GATES, 2 * d)
    gate_tiles = (PROJ_GATES // tn, (PROJ_GATES + 2 * d) // tn)
    conv_tiles = (PROJ_XBC // tn, (PROJ_XBC + S_CONV_CH) // tn)
    cw, cb = conv_w[lyr], conv_b[lyr].reshape(1, S_CONV_CH)
    lam_p = jnp.stack([lam_q1[lyr], lam_k1[lyr], lam_q2[lyr], lam_k2[lyr]], axis=0)
    wsub = w_subln[lyr].reshape(1, A_V_DIM)
    dtb_col = _head_col(dt_bias[lyr])
    alog_col = _head_col(a_log[lyr])
    dskip = jnp.repeat(d_skip[lyr], S_HEADDIM).reshape(1, S_INNER)
    wn = w_ssd_norm[lyr].reshape(1, S_INNER)
    woa = w_o_attn[lyr].astype(BF16)
    wos = w_o_ssd[lyr].astype(BF16)
    wout = w_out[lyr].astype(BF16)
    wr = jnp.concatenate([w_router[lyr], jnp.zeros((d, LANES - N_EXPERTS), F32)], axis=1)
    gpost = g_post_mix[lyr].reshape(1, d)
    gpre = g_pre_ffn[lyr].reshape(1, d)
    cos, sin = _rope_tables(n_l)

    xc = x_prompt.reshape(t_c, d)
    xl = x_sample.reshape(t_l, d)
    h_c, dtt_c = _prenorm(xc, mod3, g_pre_mix[lyr], w_dt_t, 0, t_c)
    h_l, dtt_l = _prenorm(xl, mod3, g_pre_mix[lyr], w_dt_t, 1, n_l)
    proj_c, proj_l = _inproj(h_c, h_l, w_in_t, proj_rows, BF16, cw, cb, (n_c, n_l), gate_tiles, conv_tiles)
    k_c, k_l = _inproj(h_c, h_l, w_in_t, rows_of(IN_K, d), F32, cw, cb, (n_c, n_l))
    v_c, v_l = _inproj(h_c, h_l, w_in_t, rows_of(IN_V, d), F32, cw, cb, (n_c, n_l))

    def mixer(x, dt_t, proj, k, v, row0, rows_per_cond, nbatch, n, ctx):
        if ctx is None:
            o_attn = _attn_ctx(lam_p, proj, k, v, wsub, nbatch, n, lam_init)
            h0 = None
        else:
            kc, vc, h0 = ctx
            o_attn = _attn_lat(lam_p, proj, k, v, kc, vc, cos, sin, wsub, nbatch, n, npast, lam_init)
        xa = proj
        res = _ssd(xa, dt_t, dtb_col, alog_col, h0, nbatch, n, emit_final=ctx is None)
        yf, yb = res[0], res[1]
        x1, h2, aff = _mixout(yf, yb, xa, proj, o_attn, x, mod3, dskip, wn, woa, wos, wout, gpost, gpre, wr,
                              row0, rows_per_cond)
        fin = res[2] if ctx is None else None
        return x1, h2, aff, fin

    x1_c, h2_c, aff_c, fin_c = mixer(xc, dtt_c, proj_c, k_c, v_c, 0, t_c, nb_c, n_c, None)
    kc = cache_k[:, lyr].reshape(nb_l * npast, A_HEADS * 2 * A_HEAD_DIM)
    vc = cache_v[:, lyr].reshape(nb_l * npast, A_HEADS * A_V_DIM)
    h0 = state_ssd[:, lyr].reshape(nb_l, 2, S_INNER, S_STATE)
    x1_l, h2_l, aff_l, _ = mixer(xl, dtt_l, proj_l, k_l, v_l, 1, n_l, nb_l, n_l, (kc, vc, h0))

    cap_c = EC_FACTOR * t_c // N_EXPERTS
    cap_l = EC_FACTOR * t_l // N_EXPERTS
    route_c = _router(aff_c[:, :N_EXPERTS].T, cap_c)
    route_l = _router(aff_l[:, :N_EXPERTS].T, cap_l)
    xe_c = _dispatch(route_c, h2_c, cap_c)
    xe_l = _dispatch(route_l, h2_l, cap_l)
    ye_c, ye_l = _ffn(xe_c, xe_l, w_gate[lyr], w_up[lyr], w_down[lyr])
    out_c = _combine(route_c, aff_c, ye_c, x1_c, mod3, g_post_ffn[lyr], 0, t_c)
    out_l = _combine(route_l, aff_l, ye_l, x1_l, mod3, g_post_ffn[lyr], 1, n_l)

    y_prompt = out_c.reshape(nb_c, n_c, d)
    y_sample = out_l.reshape(nb_l, n_l, d)
    new_k = k_c.reshape(nb_c, 1, n_c, A_HEADS, 2 * A_HEAD_DIM)
    new_v = v_c.reshape(nb_c, 1, n_c, A_HEADS, A_V_DIM)
    new_state = fin_c.reshape(nb_c, 1, 2, S_HEADS, S_HEADDIM, S_STATE)
    return (y_prompt, y_sample, new_k, new_v, new_state)
```

```python
import functools
import math
from typing import NamedTuple

import jax
import jax.numpy as jnp
import numpy as np
from jax import lax
from jax.experimental import pallas as pl
from jax.experimental.pallas import tpu as pltpu

F32 = jnp.float32
BF16 = jnp.bfloat16
I32 = jnp.int32

D_MODEL = 1024
GRID_W = 64
EPS = 1e-6
N_MOD = 6
A_HEADS = 8
A_HEAD_DIM = 64
A_V_DIM = 128
ROPE_BASE = 10000.0
S_INNER = 2048
S_HEADDIM = 64
S_HEADS = 32
S_GROUPS = 4
S_GROUP_W = S_INNER // S_GROUPS
S_STATE = 128
S_CHUNK = 128
S_CONV_CH = S_INNER + 2 * S_GROUPS * S_STATE
N_EXPERTS = 16
EC_FACTOR = 2
D_FF = 2048
LANES = 128
IN_Q, IN_K, IN_V, IN_Z, IN_XBC = 0, 1024, 2048, 3072, 5120
IN_DT = IN_XBC + S_CONV_CH
IN_GATES = IN_DT + 2 * S_HEADS
PROJ_XBC, PROJ_Q, PROJ_Z, PROJ_GATES = 0, 3072, 4096, 6144
VMEM_LIMIT = 56 * 1024 * 1024
LOG2E = 1.4426950408889634


def _cparams(sem, vmem=None):
    return pltpu.CompilerParams(dimension_semantics=sem, vmem_limit_bytes=vmem)


def _dot(a, b):
    return jnp.dot(a, b, preferred_element_type=F32)


def _dot_nt(a, b):
    return lax.dot_general(a, b, (((1,), (1,)), ((), ())), preferred_element_type=F32)


def _split3(a):
    a1 = a.astype(BF16)
    r1 = a - a1.astype(F32)
    a2 = r1.astype(BF16)
    a3 = (r1 - a2.astype(F32)).astype(BF16)
    return a1, a2, a3


def _dot_hi(a, b):
    a1, a2, _ = _split3(a)
    b1, b2, _ = _split3(b)
    return _dot(a1, b1) + (_dot(a1, b2) + _dot(a2, b1))


def _silu(x):
    return x * jax.nn.sigmoid(x)


def _rms(x, w):
    return x * lax.rsqrt(jnp.mean(x * x, axis=-1, keepdims=True) + EPS) * w


def _ada_kernel(c_ref, w_ref, b_ref, o_ref):
    o_ref[...] = _dot_hi(_silu(c_ref[...]), w_ref[...]) + b_ref[...]


def _ada(cond8, w_ada, b_ada):
    n = w_ada.shape[1]
    tn = 1536
    return pl.pallas_call(
        _ada_kernel,
        grid=(n // tn,),
        in_specs=[pl.BlockSpec((8, D_MODEL), lambda j: (0, 0)),
                  pl.BlockSpec((D_MODEL, tn), lambda j: (0, j)),
                  pl.BlockSpec((1, tn), lambda j: (0, j))],
        out_specs=pl.BlockSpec((8, tn), lambda j: (0, j)),
        out_shape=jax.ShapeDtypeStruct((8, n), F32),
        compiler_params=_cparams(("arbitrary",), VMEM_LIMIT),
        name="ada",
    )(cond8, w_ada, b_ada.reshape(1, n))


def _mod_spec(col, tm, row0, rows_per_cond):
    return pl.BlockSpec((1, 1, D_MODEL), lambda i, *_: (row0 + (i * tm) // rows_per_cond, 0, col))


def _prenorm_kernel(x_ref, sh_ref, sc_ref, g_ref, wt_ref, o_ref, dtt_ref):
    xn = _rms(x_ref[...], g_ref[...])
    h = (xn * (1.0 + sc_ref[0]) + sh_ref[0]).astype(BF16)
    o_ref[...] = h
    dtt_ref[...] = _dot_nt(wt_ref[...], h)


def _prenorm(x, mod3, gain, w_dt_t, row0, rows_per_cond):
    t = x.shape[0]
    tm = 1024
    return pl.pallas_call(
        _prenorm_kernel,
        grid=(t // tm,),
        in_specs=[pl.BlockSpec((tm, D_MODEL), lambda i: (i, 0)),
                  _mod_spec(0, tm, row0, rows_per_cond),
                  _mod_spec(1, tm, row0, rows_per_cond),
                  pl.BlockSpec((1, D_MODEL), lambda i: (0, 0)),
                  pl.BlockSpec((LANES, D_MODEL), lambda i: (0, 0))],
        out_specs=[pl.BlockSpec((tm, D_MODEL), lambda i: (i, 0)),
                   pl.BlockSpec((LANES, tm), lambda i: (0, i))],
        out_shape=[jax.ShapeDtypeStruct((t, D_MODEL), BF16), jax.ShapeDtypeStruct((LANES, t), F32)],
        compiler_params=_cparams(("parallel",), VMEM_LIMIT),
        name="prenorm",
    )(x, mod3, mod3, gain.reshape(1, D_MODEL), w_dt_t)


INPROJ_TM = 4096
INPROJ_TN = 512


def _conv_silu(x, w, b, n):
    rows = x.shape[0]
    pos = lax.broadcasted_iota(I32, x.shape, 0) % n
    prev = jnp.where(pos == 0, 0.0, pltpu.roll(x, 1, axis=0))
    nxt = jnp.where(pos == n - 1, 0.0, pltpu.roll(x, rows - 1, axis=0))
    return _silu(prev * w[0:1] + x * w[1:2] + nxt * w[2:3] + b)


def _inproj_kernel(blk_ref, nxt_ref, hc_ref, hl_ref, w_ref, wn_ref, cw_ref, cb_ref, oc_ref, ol_ref, wbf_ref,
                   *, sh_lo, sh_hi, shift, n_ctx_tiles, cv_lo, cv_hi, n_c, n_l):
    del blk_ref, nxt_ref
    j = pl.program_id(0)
    i = pl.program_id(1)
    shifted = jnp.logical_and(j >= sh_lo, j < sh_hi)
    conv = jnp.logical_and(j >= cv_lo, j < cv_hi)

    if shift:
        @pl.when(jnp.logical_and(i == 0, shifted))
        def _():
            w = jnp.concatenate([w_ref[shift:, :], wn_ref[:shift, :]], axis=0)
            wbf_ref[...] = w.T.astype(BF16)

    @pl.when(jnp.logical_and(i == 0, jnp.logical_not(shifted)))
    def _():
        wbf_ref[...] = w_ref[...].T.astype(BF16)

    def ctx_rows():
        tm = oc_ref.shape[0]
        return pl.ds(pl.multiple_of(jnp.minimum(i, n_ctx_tiles - 1) * tm, tm), tm)

    is_ctx = i < n_ctx_tiles
    plain = jnp.logical_not(conv)

    @pl.when(jnp.logical_and(is_ctx, plain))
    def _():
        oc_ref[...] = _dot(hc_ref[ctx_rows(), :], wbf_ref[...]).astype(oc_ref.dtype)

    @pl.when(jnp.logical_and(jnp.logical_not(is_ctx), plain))
    def _():
        ol_ref[...] = _dot(hl_ref[...], wbf_ref[...]).astype(ol_ref.dtype)

    if cv_hi > cv_lo:
        @pl.when(jnp.logical_and(is_ctx, conv))
        def _():
            y = _dot(hc_ref[ctx_rows(), :], wbf_ref[...])
            oc_ref[...] = _conv_silu(y, cw_ref[...], cb_ref[...], n_c).astype(oc_ref.dtype)

        @pl.when(jnp.logical_and(jnp.logical_not(is_ctx), conv))
        def _():
            y = _dot(hl_ref[...], wbf_ref[...])
            ol_ref[...] = _conv_silu(y, cw_ref[...], cb_ref[...], n_l).astype(ol_ref.dtype)


def _inproj(h_c, h_l, w_t, w_rows, out_dtype, conv_w, conv_b, seq_lens, shifted_tiles=(0, 0), conv_tiles=(0, 0)):
    tm, tn = INPROJ_TM, INPROJ_TN
    t_c, k = h_c.shape
    t_l = h_l.shape[0]
    assert t_c % tm == 0
    nct = t_c // tm
    ncol = len(w_rows)
    sh_lo, sh_hi = shifted_tiles
    shifts = {w_rows[j] % tn for j in range(sh_lo, sh_hi)}
    assert len(shifts) <= 1 and all(w_rows[j] % tn == 0 for j in range(ncol) if not sh_lo <= j < sh_hi)
    shift = shifts.pop() if shifts else 0
    assert shift % 8 == 0
    blocks = [r // tn for r in w_rows]
    nxt = [b + 1 if sh_lo <= j < sh_hi else blocks[sh_lo] if sh_hi > sh_lo else 0 for j, b in enumerate(blocks)]
    cv_lo, cv_hi = conv_tiles
    assert conv_w.shape[1] >= (cv_hi - cv_lo) * tn
    conv_col = lambda j, i, blk, nx: (0, jnp.clip(j - cv_lo, 0, max(cv_hi - cv_lo - 1, 0)))
    return pl.pallas_call(
        functools.partial(_inproj_kernel, sh_lo=sh_lo, sh_hi=sh_hi, shift=shift, n_ctx_tiles=nct,
                          cv_lo=cv_lo, cv_hi=cv_hi, n_c=seq_lens[0], n_l=seq_lens[1]),
        grid_spec=pltpu.PrefetchScalarGridSpec(
            num_scalar_prefetch=2,
            grid=(ncol, nct + 1),
            in_specs=[pl.BlockSpec((t_c, k), lambda j, i, blk, nx: (0, 0), pipeline_mode=pl.Buffered(1)),
                      pl.BlockSpec((t_l, k), lambda j, i, blk, nx: (0, 0), pipeline_mode=pl.Buffered(1)),
                      pl.BlockSpec((tn, k), lambda j, i, blk, nx: (blk[j], 0)),
                      pl.BlockSpec((tn, k), lambda j, i, blk, nx: (nx[j], 0)),
                      pl.BlockSpec((3, tn), conv_col),
                      pl.BlockSpec((1, tn), conv_col)],
            out_specs=[pl.BlockSpec((tm, tn), lambda j, i, blk, nx: (jnp.minimum(i, nct - 1), j)),
                       pl.BlockSpec((t_l, tn), lambda j, i, blk, nx: (0, j))],
            scratch_shapes=[pltpu.VMEM((k, tn), BF16)]),
        out_shape=[jax.ShapeDtypeStruct((t_c, ncol * tn), out_dtype),
                   jax.ShapeDtypeStruct((t_l, ncol * tn), BF16)],
        compiler_params=_cparams(("arbitrary", "arbitrary"), VMEM_LIMIT),
        name="inproj",
    )(jnp.asarray(blocks, I32), jnp.asarray(nxt, I32), h_c, h_l, w_t, w_t, conv_w, conv_b)


def _lam(lp_ref, lam_init):
    lp = lp_ref[...]
    s1 = jnp.sum(lp[0:1] * lp[1:2], axis=-1, keepdims=True)
    s2 = jnp.sum(lp[2:3] * lp[3:4], axis=-1, keepdims=True)
    return jnp.exp(s1) - jnp.exp(s2) + lam_init


ATTN_TQ = 64


def _with_ones(v):
    return jnp.concatenate([v, jnp.ones(v.shape, v.dtype)], axis=1)


def _diff_attn(q, k, v1, lam, wsub, lam_init):
    nq = q.shape[0]
    lane = lax.broadcasted_iota(I32, q.shape, 1)
    q = q * (A_HEAD_DIM ** -0.5 * LOG2E)
    q1 = jnp.where(lane < A_HEAD_DIM, q, 0.0).astype(BF16)
    q2 = jnp.where(lane >= A_HEAD_DIM, q, 0.0).astype(BF16)
    s = _dot_nt(jnp.concatenate([q1, q2], axis=0), k)
    p = jnp.exp2(s - jnp.max(s, axis=-1, keepdims=True)).astype(BF16)
    pv = _dot(p, v1)
    pv = pv[:, :A_V_DIM] / pv[:, A_V_DIM:]
    o = pv[:nq] - lam * pv[nq:]
    return _rms(o, wsub) * (1.0 - lam_init)


def _attn_ctx_kernel(lp_ref, q_ref, k_ref, v_ref, ws_ref, o_ref, *, lam_init, tq):
    lam = _lam(lp_ref, lam_init)
    for h in range(A_HEADS):
        sl = slice(h * LANES, (h + 1) * LANES)
        k = k_ref[:, sl].astype(BF16)
        v1 = _with_ones(v_ref[:, sl].astype(BF16))
        for i in range(q_ref.shape[0] // tq):
            rows = slice(i * tq, (i + 1) * tq)
            o = _diff_attn(q_ref[rows, sl].astype(F32), k, v1, lam, ws_ref[...], lam_init)
            o_ref[rows, sl] = o.astype(o_ref.dtype)


def _attn_ctx(lam_p, proj, k, v, wsub, nbatch, n, lam_init):
    w = A_HEADS * LANES
    blk = pl.BlockSpec((n, w), lambda b: (b, 0))
    return pl.pallas_call(
        functools.partial(_attn_ctx_kernel, lam_init=lam_init, tq=ATTN_TQ),
        grid=(nbatch,),
        in_specs=[pl.BlockSpec((4, A_HEAD_DIM), lambda b: (0, 0)),
                  pl.BlockSpec((n, w), lambda b: (b, PROJ_Q // w)), blk, blk,
                  pl.BlockSpec((1, A_V_DIM), lambda b: (0, 0))],
        out_specs=blk,
        out_shape=jax.ShapeDtypeStruct(k.shape, BF16),
        compiler_params=_cparams(("parallel",), VMEM_LIMIT),
        name="attn_ctx",
    )(lam_p, proj, k, v, wsub)


def _rope(x, cos, sin_signed):
    lane = lax.broadcasted_iota(I32, x.shape, 1)
    quarter = A_HEAD_DIM // 4
    partner = jnp.where((lane % (2 * quarter)) < quarter,
                        pltpu.roll(x, LANES - quarter, axis=1),
                        pltpu.roll(x, quarter, axis=1))
    return x * cos + partner * sin_signed


def _attn_lat_kernel(lp_ref, q_ref, k_ref, v_ref, kc_ref, vc_ref, cos_ref, sin_ref, ws_ref, o_ref, *, lam_init, tq):
    lam = _lam(lp_ref, lam_init)
    k = _rope(k_ref[...].astype(F32), cos_ref[...], sin_ref[...])
    k_all = jnp.concatenate([kc_ref[...].astype(BF16), k.astype(BF16)], axis=0)
    v_all = _with_ones(jnp.concatenate([vc_ref[...].astype(BF16), v_ref[...].astype(BF16)], axis=0))
    for i in range(q_ref.shape[0] // tq):
        rows = slice(i * tq, (i + 1) * tq)
        q = _rope(q_ref[rows, :].astype(F32), cos_ref[rows, :], sin_ref[rows, :])
        o_ref[rows, :] = _diff_attn(q, k_all, v_all, lam, ws_ref[...], lam_init).astype(o_ref.dtype)


def _attn_lat(lam_p, proj, k, v, kc, vc, cos, sin, wsub, nbatch, n, npast, lam_init):
    blk = pl.BlockSpec((n, LANES), lambda b, h: (b, h))
    cblk = pl.BlockSpec((npast, LANES), lambda b, h: (b, h))
    tab = pl.BlockSpec((n, LANES), lambda b, h: (0, 0))
    return pl.pallas_call(
        functools.partial(_attn_lat_kernel, lam_init=lam_init, tq=ATTN_TQ),
        grid=(nbatch, A_HEADS),
        in_specs=[pl.BlockSpec((4, A_HEAD_DIM), lambda b, h: (0, 0)),
                  pl.BlockSpec((n, LANES), lambda b, h: (b, PROJ_Q // LANES + h)), blk, blk, cblk, cblk, tab, tab,
                  pl.BlockSpec((1, A_V_DIM), lambda b, h: (0, 0))],
        out_specs=blk,
        out_shape=jax.ShapeDtypeStruct(k.shape, BF16),
        compiler_params=_cparams(("parallel", "parallel"), VMEM_LIMIT),
        name="attn_lat",
    )(lam_p, proj, k, v, kc, vc, cos, sin, wsub)


def _rope_tables(n):
    pos = np.arange(n)
    rowcol = np.stack([pos // GRID_W, pos % GRID_W], axis=-1).astype(np.float64)
    nf = A_HEAD_DIM // 4
    inv_freq = np.power(ROPE_BASE, -np.arange(nf, dtype=np.float64) / nf)
    lane = np.arange(LANES)
    axis = (lane % A_HEAD_DIM) // (2 * nf)
    ang = rowcol[:, axis] * inv_freq[lane % nf][None, :]
    sign = np.where((lane % (2 * nf)) < nf, -1.0, 1.0)
    return jnp.asarray(np.cos(ang), F32), jnp.asarray(np.sin(ang) * sign[None, :], F32)


S_PAIRS = S_INNER // LANES
SSD_ROWS = 64


def _ssd_direction(xa_ref, dt_raw_t, dtb_col, alog_col, ht_ref, y_ref, backward):
    col0 = S_HEADS if backward else 0
    row = lax.broadcasted_iota(I32, (S_CHUNK, S_CHUNK), 0)
    col = lax.broadcasted_iota(I32, (S_CHUNK, S_CHUNK), 1)
    dt_t = jax.nn.softplus(dt_raw_t + dtb_col)
    da_t = dt_t * (-jnp.exp(alog_col) * LOG2E)
    if backward:
        last = 0
        keep = row <= col
    else:
        last = S_CHUNK - 1
        keep = row >= col
    tri = jnp.where(keep, 1.0, 0.0).astype(BF16)
    pieces = _split3(da_t)
    cum = sum(_dot_nt(tri, p) for p in pieces[1:]) + _dot_nt(tri, pieces[0])
    cum_t = sum(_dot_nt(p, tri) for p in pieces[1:]) + _dot_nt(pieces[0], tri)
    w_t = dt_t * jnp.exp2(cum_t[:, last:last + 1] - cum_t)
    e_tot = jnp.exp2(cum[last:last + 1, :])
    lo_half = col < S_HEADDIM
    lo_bf = jnp.where(lo_half, 1.0, 0.0).astype(BF16)
    hi_bf = jnp.where(lo_half, 0.0, 1.0).astype(BF16)
    for g in range(S_GROUPS):
        b_bf = xa_ref[:, S_INNER + g * S_STATE:S_INNER + (g + 1) * S_STATE]
        c_bf = xa_ref[:, S_INNER + (S_GROUPS + g) * S_STATE:S_INNER + (S_GROUPS + g + 1) * S_STATE]
        cb = _dot_nt(c_bf, b_bf)
        bt = b_bf.astype(F32).T
        c_f = c_bf.astype(F32)
        for pr in range(S_PAIRS // S_GROUPS):
            hp = g * (S_PAIRS // S_GROUPS) + pr
            c0 = col0 + 2 * hp
            x_pair = xa_ref[:, hp * LANES:(hp + 1) * LANES]
            bd_x = jnp.concatenate([x_pair * lo_bf, x_pair * hi_bf], axis=0)
            ht = ht_ref[hp]
            bd_h = jnp.concatenate([jnp.where(lo_half, ht, 0.0), jnp.where(lo_half, 0.0, ht)], axis=0).astype(BF16)
            rhs_y = jnp.concatenate([bd_x, bd_h], axis=0)
            bws = [(bt * w_t[cc:cc + 1, :]).astype(BF16) for cc in (c0, c0 + 1)]
            for lh in range(S_CHUNK // SSD_ROWS):
                rows = slice(lh * SSD_ROWS, (lh + 1) * SSD_ROWS)
                ms, ces = [], []
                for cc in (c0, c0 + 1):
                    bc = jnp.broadcast_to(cum[rows, cc:cc + 1], (SSD_ROWS, S_CHUNK))
                    decay = jnp.exp2(jnp.where(keep[rows], bc - cum_t[cc:cc + 1, :], -1e30))
                    ms.append((cb[rows] * decay * dt_t[cc:cc + 1, :]).astype(BF16))
                    ces.append((c_f[rows] * jnp.exp2(bc)).astype(BF16))
                y_ref[rows, hp * LANES:(hp + 1) * LANES] = _dot(jnp.concatenate(ms + ces, axis=1),
                                                                rhs_y).astype(y_ref.dtype)
            tot = jnp.where(lo_half[0:1], e_tot[:, c0:c0 + 1], e_tot[:, c0 + 1:c0 + 2])
            ht_ref[hp] = ht * tot + _dot(jnp.concatenate(bws, axis=1), bd_x)


def _ssd_kernel(*refs, has_h0, emit_final):
    xf_ref, xb_ref, dtft_ref, dtbt_ref, bc_ref, ac_ref = refs[:6]
    refs = refs[6:]
    if has_h0:
        h0_ref, refs = refs[0], refs[1:]
    yf_ref, yb_ref = refs[:2]
    refs = refs[2:]
    if emit_final:
        fin_ref, refs = refs[0], refs[1:]
    htf_ref, htb_ref = refs
    c = pl.program_id(1)
    nc = pl.num_programs(1)

    @pl.when(c == 0)
    def _():
        for d, ht_ref in enumerate((htf_ref, htb_ref)):
            for hp in range(S_PAIRS):
                if has_h0:
                    ht_ref[hp] = h0_ref[0, d, hp * LANES:(hp + 1) * LANES, :].T
                else:
                    ht_ref[hp] = jnp.zeros((S_STATE, LANES), F32)

    _ssd_direction(xf_ref, dtft_ref[...], bc_ref[...], ac_ref[...], htf_ref, yf_ref, backward=False)
    _ssd_direction(xb_ref, dtbt_ref[...], bc_ref[...], ac_ref[...], htb_ref, yb_ref, backward=True)

    if emit_final:
        @pl.when(c == nc - 1)
        def _():
            for d, ht_ref in enumerate((htf_ref, htb_ref)):
                for hp in range(S_PAIRS):
                    fin_ref[0, d, hp * LANES:(hp + 1) * LANES, :] = ht_ref[hp].T


def _ssd(xa, dt_t, dtb_col, alog_col, h0, nbatch, n, emit_final):
    nc = n // S_CHUNK
    t = xa.shape[0]
    fwd = lambda b, c: (b * nc + c, 0)
    bwd = lambda b, c: (b * nc + nc - 1 - c, 0)
    const = lambda b, c: (0, 0)
    in_specs = [pl.BlockSpec((S_CHUNK, S_CONV_CH), fwd),
                pl.BlockSpec((S_CHUNK, S_CONV_CH), bwd),
                pl.BlockSpec((LANES, S_CHUNK), lambda b, c: (0, b * nc + c)),
                pl.BlockSpec((LANES, S_CHUNK), lambda b, c: (0, b * nc + nc - 1 - c)),
                pl.BlockSpec((LANES, 1), const), pl.BlockSpec((LANES, 1), const)]
    args = [xa, xa, dt_t, dt_t, dtb_col, alog_col]
    if h0 is not None:
        in_specs.append(pl.BlockSpec((1, 2, S_INNER, S_STATE), lambda b, c: (b, 0, 0, 0)))
        args.append(h0)
    out_specs = [pl.BlockSpec((S_CHUNK, S_INNER), fwd), pl.BlockSpec((S_CHUNK, S_INNER), bwd)]
    out_shape = [jax.ShapeDtypeStruct((t, S_INNER), BF16), jax.ShapeDtypeStruct((t, S_INNER), BF16)]
    if emit_final:
        out_specs.append(pl.BlockSpec((1, 2, S_INNER, S_STATE), lambda b, c: (b, 0, 0, 0)))
        out_shape.append(jax.ShapeDtypeStruct((nbatch, 2, S_INNER, S_STATE), F32))
    return pl.pallas_call(
        functools.partial(_ssd_kernel, has_h0=h0 is not None, emit_final=emit_final),
        grid=(nbatch, nc),
        in_specs=in_specs,
        out_specs=out_specs,
        out_shape=out_shape,
        scratch_shapes=[pltpu.VMEM((S_PAIRS, S_STATE, LANES), F32),
                        pltpu.VMEM((S_PAIRS, S_STATE, LANES), F32)],
        compiler_params=_cparams(("parallel", "arbitrary"), VMEM_LIMIT),
        name="ssd",
    )(*args)


MIXOUT_SPLIT = 2


def _mixout_kernel(yf_ref, yb_ref, xs_ref, z_ref, oa_ref, ga_ref, gs_ref, x_ref, g1_ref, sh2_ref, sc2_ref,
                   dskip_ref, wn_ref, woa_ref, wos_ref, wout_ref, gpost_ref, gpre_ref, wr_ref,
                   x1_ref, h2_ref, aff_ref):
    wn = wn_ref[...]
    sub = x_ref.shape[0] // MIXOUT_SPLIT
    for r in range(MIXOUT_SPLIT):
        rows = slice(r * sub, (r + 1) * sub)
        y = (yf_ref[rows, :].astype(F32) + yb_ref[rows, :].astype(F32)
             + xs_ref[rows, :].astype(F32) * dskip_ref[...])
        y = y * _silu(z_ref[rows, :].astype(F32))
        parts = []
        for g in range(S_GROUPS):
            sl = slice(g * S_GROUP_W, (g + 1) * S_GROUP_W)
            parts.append(_rms(y[:, sl], wn[:, sl]).astype(BF16))
        o_ssd = jnp.concatenate(parts, axis=1)
        a = _dot(oa_ref[rows, :], woa_ref[...])
        s = _dot(o_ssd, wos_ref[...])
        merged = (jax.nn.sigmoid(ga_ref[rows, :].astype(F32)) * a
                  + jax.nn.sigmoid(gs_ref[rows, :].astype(F32)) * s)
        ym = _dot(merged.astype(BF16), wout_ref[...])
        x1 = x_ref[rows, :] + g1_ref[0] * _rms(ym, gpost_ref[...])
        x1_ref[rows, :] = x1
        h2 = _rms(x1, gpre_ref[...]) * (1.0 + sc2_ref[0]) + sh2_ref[0]
        h2_ref[rows, :] = h2.astype(BF16)
        logits = _dot_hi(h2, wr_ref[...])
        lane = lax.broadcasted_iota(I32, logits.shape, 1)
        logits = jnp.where(lane < N_EXPERTS, logits, -1e30)
        p = jnp.exp(logits - jnp.max(logits, axis=-1, keepdims=True))
        aff_ref[rows, :] = p / jnp.sum(p, axis=-1, keepdims=True)


def _mixout(yf, yb, xa, proj, o_attn, x, mod3, dskip, wn, woa, wos, wout, gpost, gpre, wr, row0, rows_per_cond):
    t = x.shape[0]
    tm = 512
    tok = lambda w: pl.BlockSpec((tm, w), lambda i: (i, 0))
    const = lambda a: pl.BlockSpec(a.shape, lambda i: (0,) * a.ndim, pipeline_mode=pl.Buffered(1))
    return pl.pallas_call(
        _mixout_kernel,
        grid=(t // tm,),
        in_specs=[tok(S_INNER), tok(S_INNER), tok(S_INNER),
                  pl.BlockSpec((tm, S_INNER), lambda i: (i, PROJ_Z // S_INNER)), tok(D_MODEL),
                  pl.BlockSpec((tm, D_MODEL), lambda i: (i, PROJ_GATES // D_MODEL)),
                  pl.BlockSpec((tm, D_MODEL), lambda i: (i, PROJ_GATES // D_MODEL + 1)),
                  tok(D_MODEL),
                  _mod_spec(2, tm, row0, rows_per_cond),
                  _mod_spec(3, tm, row0, rows_per_cond),
                  _mod_spec(4, tm, row0, rows_per_cond),
                  const(dskip), const(wn), const(woa), const(wos), const(wout), const(gpost), const(gpre),
                  const(wr)],
        out_specs=[tok(D_MODEL), tok(D_MODEL), tok(LANES)],
        out_shape=[jax.ShapeDtypeStruct((t, D_MODEL), F32), jax.ShapeDtypeStruct((t, D_MODEL), BF16),
                   jax.ShapeDtypeStruct((t, LANES), F32)],
        compiler_params=_cparams(("parallel",), VMEM_LIMIT),
        name="mixout",
    )(yf, yb, xa, proj, o_attn, proj, proj, x, mod3, mod3, mod3, dskip, wn, woa, wos, wout, gpost, gpre, wr)


COMB_TILE = 256
COMB_WIN = 64
BF16_ROWS = 16


def _router_kernel(aff_ref, pos_ref, cnt_ref, *, cap):
    aff = aff_ref[...]
    e, t = aff.shape
    idx = lax.broadcasted_iota(I32, (e, t), 1)

    def count(m):
        return jnp.sum(jnp.where(m, 1.0, 0.0), axis=-1, keepdims=True)

    def value_bit(i, thr):
        cand = thr | jnp.left_shift(jnp.int32(1), 30 - i)
        return jnp.where(count(aff >= pltpu.bitcast(cand, F32)) >= cap, cand, thr)

    thr = pltpu.bitcast(lax.fori_loop(0, 31, value_bit, jnp.zeros((e, 1), I32)), F32)
    gt = aff > thr
    eq = aff == thr
    need = cap - count(gt)
    nbits = (t - 1).bit_length()

    def index_bit(i, v):
        cand = v | jnp.left_shift(jnp.int32(1), nbits - 1 - i)
        return jnp.where(count(jnp.logical_and(eq, idx < cand)) < need, cand, v)

    last = lax.fori_loop(0, nbits, index_bit, jnp.zeros((e, 1), I32))
    sel = jnp.logical_or(gt, jnp.logical_and(eq, idx <= last))
    sel_bf = jnp.where(sel, 1.0, 0.0).astype(BF16)
    blk = 512
    r = lax.broadcasted_iota(I32, (blk, blk), 0)
    c = lax.broadcasted_iota(I32, (blk, blk), 1)
    upper = jnp.where(r <= c, 1.0, 0.0).astype(BF16)
    carry = jnp.zeros((e, 1), F32)
    tile_cnt = []
    for b in range(t // blk):
        cum = _dot(sel_bf[:, b * blk:(b + 1) * blk], upper) + carry
        carry = cum[:, blk - 1:blk]
        slot = jnp.where(sel[:, b * blk:(b + 1) * blk], cum - 1.0, -1.0)
        pos_ref[:, b * blk:(b + 1) * blk] = slot.astype(I32)
        for j in range(blk // COMB_TILE):
            tile_cnt.append(cum[:, (j + 1) * COMB_TILE - 1:(j + 1) * COMB_TILE])
    lane = lax.broadcasted_iota(I32, (e, LANES), 1)
    cnt = jnp.zeros((e, LANES), F32)
    for i, v in enumerate(tile_cnt):
        cnt = jnp.where(lane == i, v, cnt)
    cnt_ref[...] = cnt.astype(I32)


class _Route(NamedTuple):
    pos: jax.Array
    start: jax.Array
    end: jax.Array
    win: jax.Array
    fits: jax.Array


def _router(aff_t, cap):
    e, t = aff_t.shape
    ntile = t // COMB_TILE
    pos, cnt = pl.pallas_call(
        functools.partial(_router_kernel, cap=cap),
        out_shape=[jax.ShapeDtypeStruct((e, t), I32), jax.ShapeDtypeStruct((e, LANES), I32)],
        name="router",
    )(aff_t)
    end = cnt[:, :ntile]
    start = jnp.concatenate([jnp.zeros((e, 1), I32), end[:, :-1]], axis=1)
    win = jnp.minimum(start // BF16_ROWS * BF16_ROWS, cap - COMB_WIN)
    fits = jnp.all(end - win <= COMB_WIN, axis=0).astype(I32)
    flat = lambda a: a.T.reshape(ntile * e)
    return _Route(pos, flat(start), flat(end), flat(win), fits)


def _dispatch_kernel(start_ref, end_ref, win_ref, fits_ref, pos_ref, h_ref, o_ref):
    i = pl.program_id(0)
    ne, cap, _ = o_ref.shape
    tm = h_ref.shape[0]

    @pl.when(i == 0)
    def _():
        o_ref[...] = jnp.zeros(o_ref.shape, o_ref.dtype)

    h = h_ref[...]
    row = lax.broadcasted_iota(I32, (COMB_WIN, tm), 0)
    row_l = lax.broadcasted_iota(I32, (COMB_WIN, LANES), 0)

    def place(e, r0, rows):
        rel = r0 - start_ref[i * ne + e] + row_l
        mine = jnp.logical_and(rel >= 0, rel < end_ref[i * ne + e] - start_ref[i * ne + e])
        for c in range(D_MODEL // LANES):
            sl = slice(c * LANES, (c + 1) * LANES)
            old = o_ref[e, pl.ds(r0, COMB_WIN), sl].astype(F32)
            o_ref[e, pl.ds(r0, COMB_WIN), sl] = jnp.where(mine, rows[:, sl], old).astype(o_ref.dtype)

    @pl.when(fits_ref[i] == 1)
    def _():
        sel = [jnp.where(pos_ref[e:e + 1, :] == win_ref[i * ne + e] + row, 1.0, 0.0).astype(BF16) for e in range(ne)]
        rows = _dot(jnp.concatenate(sel, axis=0), h)
        for e in range(ne):
            place(e, pl.multiple_of(win_ref[i * ne + e], BF16_ROWS), rows[e * COMB_WIN:(e + 1) * COMB_WIN])

    @pl.when(fits_ref[i] == 0)
    def _():
        for e in range(ne):
            w0 = win_ref[i * ne + e]

            def window(w, carry):
                r0 = pl.multiple_of(jnp.minimum(w0 + w * COMB_WIN, cap - COMB_WIN), BF16_ROWS)
                sel = jnp.where(pos_ref[e:e + 1, :] == r0 + row, 1.0, 0.0).astype(BF16)
                place(e, r0, _dot(sel, h))
                return carry

            lax.fori_loop(0, (end_ref[i * ne + e] - w0 + COMB_WIN - 1) // COMB_WIN, window, 0)


def _dispatch(route, h2, cap):
    e, t = route.pos.shape
    tm = COMB_TILE
    return pl.pallas_call(
        _dispatch_kernel,
        grid_spec=pltpu.PrefetchScalarGridSpec(
            num_scalar_prefetch=4,
            grid=(t // tm,),
            in_specs=[pl.BlockSpec((e, tm), lambda i, *_: (0, i)),
                      pl.BlockSpec((tm, D_MODEL), lambda i, *_: (i, 0))],
            out_specs=pl.BlockSpec((e, cap, D_MODEL), lambda i, *_: (0, 0, 0))),
        out_shape=jax.ShapeDtypeStruct((e, cap, D_MODEL), BF16),
        compiler_params=_cparams(("arbitrary",), VMEM_LIMIT),
        name="dispatch",
    )(route.start, route.end, route.win, route.fits, route.pos, h2)


def _ffn_kernel(xc_ref, xl_ref, wg_ref, wu_ref, wd_ref, oc_ref, ol_ref, acc_ref):
    f = pl.program_id(1)
    cc = xc_ref.shape[1]
    xe = jnp.concatenate([xc_ref[0], xl_ref[0]], axis=0)
    hg = _dot(xe, wg_ref[0].astype(BF16))
    hu = _dot(xe, wu_ref[0].astype(BF16))
    part = _dot((_silu(hg) * hu).astype(BF16), wd_ref[0].astype(BF16))

    @pl.when(f == 0)
    def _():
        acc_ref[...] = part

    @pl.when(f > 0)
    def _():
        acc_ref[...] += part

    @pl.when(f == pl.num_programs(1) - 1)
    def _():
        oc_ref[0] = acc_ref[:cc, :].astype(BF16)
        ol_ref[0] = acc_ref[cc:, :].astype(BF16)


def _ffn(xe_c, xe_l, w_gate, w_up, w_down):
    e, cc, d = xe_c.shape
    cl = xe_l.shape[1]
    tf = 1024
    return pl.pallas_call(
        _ffn_kernel,
        grid=(e, D_FF // tf),
        in_specs=[pl.BlockSpec((1, cc, d), lambda i, f: (i, 0, 0)),
                  pl.BlockSpec((1, cl, d), lambda i, f: (i, 0, 0)),
                  pl.BlockSpec((1, d, tf), lambda i, f: (i, 0, f)),
                  pl.BlockSpec((1, d, tf), lambda i, f: (i, 0, f)),
                  pl.BlockSpec((1, tf, d), lambda i, f: (i, f, 0))],
        out_specs=[pl.BlockSpec((1, cc, d), lambda i, f: (i, 0, 0)),
                   pl.BlockSpec((1, cl, d), lambda i, f: (i, 0, 0))],
        out_shape=[jax.ShapeDtypeStruct((e, cc, d), BF16), jax.ShapeDtypeStruct((e, cl, d), BF16)],
        scratch_shapes=[pltpu.VMEM((cc + cl, d), F32)],
        compiler_params=_cparams(("parallel", "arbitrary"), VMEM_LIMIT),
        name="ffn",
    )(xe_c, xe_l, w_gate, w_up, w_down)


def _combine_kernel(win_ref, fits_ref, pos_ref, aff_ref, ye_ref, x1_ref, g2_ref, gp_ref, o_ref, rhs_ref, y_ref):
    i = pl.program_id(0)
    tm = pos_ref.shape[0]
    cap = ye_ref.shape[1]
    pos = pos_ref[...]
    aff = aff_ref[...]

    @pl.when(fits_ref[i] == 1)
    def _():
        per_col = LANES // COMB_WIN
        lane = lax.broadcasted_iota(I32, (tm, LANES), 1)
        cols = []
        for c in range(N_EXPERTS // per_col):
            col = jnp.zeros((tm, LANES), F32)
            for j in range(per_col):
                e = c * per_col + j
                w0 = pl.multiple_of(win_ref[i * N_EXPERTS + e], BF16_ROWS)
                rhs_ref[e * COMB_WIN:(e + 1) * COMB_WIN, :] = ye_ref[e, pl.ds(w0, COMB_WIN), :]
                p = pos[:, e:e + 1]
                tgt = jnp.where(p >= 0, p - w0 + j * COMB_WIN, -1)
                col = jnp.where(lane == tgt, aff[:, e:e + 1], col)
            cols.append(col)
        sel = jnp.concatenate(cols, axis=1)
        hi = sel.astype(BF16)
        lo = (sel - hi.astype(F32)).astype(BF16)
        both = _dot(jnp.concatenate([hi, lo], axis=0), rhs_ref[...])
        y_ref[...] = both[:tm] + both[tm:]

    @pl.when(fits_ref[i] == 0)
    def _():
        slot = lax.broadcasted_iota(I32, (tm, cap), 1)
        y = jnp.zeros((tm, D_MODEL), F32)
        for e in range(N_EXPERTS):
            onehot = jnp.where(pos[:, e:e + 1] == slot, 1.0, 0.0).astype(BF16)
            y = y + aff[:, e:e + 1] * _dot(onehot, ye_ref[e])
        y_ref[...] = y

    o_ref[...] = x1_ref[...] + g2_ref[0] * _rms(y_ref[...], gp_ref[...])


def _combine(route, aff, ye, x1, mod3, gpost, row0, rows_per_cond):
    t = x1.shape[0]
    pos_t, win, fits = route.pos.T, route.win, route.fits
    e, cap, d = ye.shape
    tm = COMB_TILE
    return pl.pallas_call(
        _combine_kernel,
        grid_spec=pltpu.PrefetchScalarGridSpec(
            num_scalar_prefetch=2,
            grid=(t // tm,),
            in_specs=[pl.BlockSpec((tm, e), lambda i, *_: (i, 0)),
                      pl.BlockSpec((tm, LANES), lambda i, *_: (i, 0)),
                      pl.BlockSpec((e, cap, d), lambda i, *_: (0, 0, 0)),
                      pl.BlockSpec((tm, d), lambda i, *_: (i, 0)),
                      _mod_spec(5, tm, row0, rows_per_cond),
                      pl.BlockSpec((1, d), lambda i, *_: (0, 0))],
            out_specs=pl.BlockSpec((tm, d), lambda i, *_: (i, 0)),
            scratch_shapes=[pltpu.VMEM((e * COMB_WIN, d), BF16), pltpu.VMEM((tm, d), F32)]),
        out_shape=jax.ShapeDtypeStruct((t, d), F32),
        compiler_params=_cparams(("parallel",), VMEM_LIMIT),
        name="combine",
    )(win, fits, pos_t, aff, ye, x1, mod3, gpost.reshape(1, d))


def _head_col(v):
    return jnp.concatenate([v.reshape(2 * S_HEADS), jnp.zeros((LANES - 2 * S_HEADS,), F32)]).reshape(LANES, 1)


def kernel(x_prompt, x_sample, cache_k, cache_v, state_ssd, c, c_ctx, w_ada, b_ada, g_pre_mix, g_post_mix, g_pre_ffn, g_post_ffn, w_in, lam_q1, lam_k1, lam_q2, lam_k2, w_subln, conv_w, conv_b, dt_bias, a_log, d_skip, w_ssd_norm, w_o_attn, w_o_ssd, w_out, w_router, w_gate, w_up, w_down):
    nb_c, n_c, d = x_prompt.shape
    nb_l, n_l, _ = x_sample.shape
    npast = cache_k.shape[2]
    depth = w_in.shape[0]
    assert depth == 1 and d == D_MODEL
    t_c, t_l = nb_c * n_c, nb_l * n_l
    lyr = 0
    lam_init = 0.8 - 0.6 * math.exp(-0.3 * lyr)

    cond8 = jnp.concatenate([c_ctx[None, :], c, jnp.zeros((8 - 1 - nb_l, d), F32)], axis=0)
    mod3 = _ada(cond8, w_ada[lyr], b_ada[lyr]).reshape(8, 1, N_MOD * d)

    w_in_t = w_in[lyr].T
    w_dt_t = jnp.concatenate([w_in_t[IN_DT:IN_GATES], jnp.zeros((LANES - 2 * S_HEADS, d), F32)], axis=0).astype(BF16)
    tn = INPROJ_TN
    rows_of = lambda col0, width: [col0 + j * tn for j in range(width // tn)]
    proj_rows = rows_of(IN_XBC, S_CONV_CH) + rows_of(IN_Q, d) + rows_of(IN_Z, S_INNER) + rows_of(IN_GATES, 2 * d)
    gate_tiles = (PROJ_GATES // tn, (PROJ_GATES + 2 * d) // tn)
    conv_tiles = (PROJ_XBC // tn, (PROJ_XBC + S_CONV_CH) // tn)
    cw, cb = conv_w[lyr], conv_b[lyr].reshape(1, S_CONV_CH)
    lam_p = jnp.stack([lam_q1[lyr], lam_k1[lyr], lam_q2[lyr], lam_k2[lyr]], axis=0)
    wsub = w_subln[lyr].reshape(1, A_V_DIM)
    dtb_col = _head_col(dt_bias[lyr])
    alog_col = _head_col(a_log[lyr])
    dskip = jnp.repeat(d_skip[lyr], S_HEADDIM).reshape(1, S_INNER)
    wn = w_ssd_norm[lyr].reshape(1, S_INNER)
    woa = w_o_attn[lyr].astype(BF16)
    wos = w_o_ssd[lyr].astype(BF16)
    wout = w_out[lyr].astype(BF16)
    wr = jnp.concatenate([w_router[lyr], jnp.zeros((d, LANES - N_EXPERTS), F32)], axis=1)
    gpost = g_post_mix[lyr].reshape(1, d)
    gpre = g_pre_ffn[lyr].reshape(1, d)
    cos, sin = _rope_tables(n_l)

    xc = x_prompt.reshape(t_c, d)
    xl = x_sample.reshape(t_l, d)
    h_c, dtt_c = _prenorm(xc, mod3, g_pre_mix[lyr], w_dt_t, 0, t_c)
    h_l, dtt_l = _prenorm(xl, mod3, g_pre_mix[lyr], w_dt_t, 1, n_l)
    proj_c, proj_l = _inproj(h_c, h_l, w_in_t, proj_rows, BF16, cw, cb, (n_c, n_l), gate_tiles, conv_tiles)
    k_c, k_l = _inproj(h_c, h_l, w_in_t, rows_of(IN_K, d), F32, cw, cb, (n_c, n_l))
    v_c, v_l = _inproj(h_c, h_l, w_in_t, rows_of(IN_V, d), F32, cw, cb, (n_c, n_l))

    def mixer(x, dt_t, proj, k, v, row0, rows_per_cond, nbatch, n, ctx):
        if ctx is None:
            o_attn = _attn_ctx(lam_p, proj, k, v, wsub, nbatch, n, lam_init)
            h0 = None
        else:
            kc, vc, h0 = ctx
            o_attn = _attn_lat(lam_p, proj, k, v, kc, vc, cos, sin, wsub, nbatch, n, npast, lam_init)
        xa = proj
        res = _ssd(xa, dt_t, dtb_col, alog_col, h0, nbatch, n, emit_final=ctx is None)
        yf, yb = res[0], res[1]
        x1, h2, aff = _mixout(yf, yb, xa, proj, o_attn, x, mod3, dskip, wn, woa, wos, wout, gpost, gpre, wr,
                              row0, rows_per_cond)
        fin = res[2] if ctx is None else None
        return x1, h2, aff, fin

    x1_c, h2_c, aff_c, fin_c = mixer(xc, dtt_c, proj_c, k_c, v_c, 0, t_c, nb_c, n_c, None)
    kc = cache_k[:, lyr].reshape(nb_l * npast, A_HEADS * 2 * A_HEAD_DIM)
    vc = cache_v[:, lyr].reshape(nb_l * npast, A_HEADS * A_V_DIM)
    h0 = state_ssd[:, lyr].reshape(nb_l, 2, S_INNER, S_STATE)
    x1_l, h2_l, aff_l, _ = mixer(xl, dtt_l, proj_l, k_l, v_l, 1, n_l, nb_l, n_l, (kc, vc, h0))

    cap_c = EC_FACTOR * t_c // N_EXPERTS
    cap_l = EC_FACTOR * t_l // N_EXPERTS
    route_c = _router(aff_c[:, :N_EXPERTS].T, cap_c)
    route_l = _router(aff_l[:, :N_EXPERTS].T, cap_l)
    xe_c = _dispatch(route_c, h2_c, cap_c)
    xe_l = _dispatch(route_l, h2_l, cap_l)
    ye_c, ye_l = _ffn(xe_c, xe_l, w_gate[lyr], w_up[lyr], w_down[lyr])
    out_c = _combine(route_c, aff_c, ye_c, x1_c, mod3, g_post_ffn[lyr], 0, t_c)
    out_l = _combine(route_l, aff_l, ye_l, x1_l, mod3, g_post_ffn[lyr], 1, n_l)

    y_prompt = out_c.reshape(nb_c, n_c, d)
    y_sample = out_l.reshape(nb_l, n_l, d)
    new_k = k_c.reshape(nb_c, 1, n_c, A_HEADS, 2 * A_HEAD_DIM)
    new_v = v_c.reshape(nb_c, 1, n_c, A_HEADS, A_V_DIM)
    new_state = fin_c.reshape(nb_c, 1, 2, S_HEADS, S_HEADDIM, S_STATE)
    return (y_prompt, y_sample, new_k, new_v, new_state)
```

```python
import functools
import math
from typing import NamedTuple

import jax
import jax.numpy as jnp
import numpy as np
from jax import lax
from jax.experimental import pallas as pl
from jax.experimental.pallas import tpu as pltpu

F32 = jnp.float32
BF16 = jnp.bfloat16
I32 = jnp.int32

D_MODEL = 1024
GRID_W = 64
EPS = 1e-6
N_MOD = 6
A_HEADS = 8
A_HEAD_DIM = 64
A_V_DIM = 128
ROPE_BASE = 10000.0
S_INNER = 2048
S_HEADDIM = 64
S_HEADS = 32
S_GROUPS = 4
S_GROUP_W = S_INNER // S_GROUPS
S_STATE = 128
S_CHUNK = 128
S_CONV_CH = S_INNER + 2 * S_GROUPS * S_STATE
N_EXPERTS = 16
EC_FACTOR = 2
D_FF = 2048
LANES = 128
IN_Q, IN_K, IN_V, IN_Z, IN_XBC = 0, 1024, 2048, 3072, 5120
IN_DT = IN_XBC + S_CONV_CH
IN_GATES = IN_DT + 2 * S_HEADS
PROJ_XBC, PROJ_Q, PROJ_Z, PROJ_GATES = 0, 3072, 4096, 6144
VMEM_LIMIT = 56 * 1024 * 1024
LOG2E = 1.4426950408889634


def _cparams(sem, vmem=None):
    return pltpu.CompilerParams(dimension_semantics=sem, vmem_limit_bytes=vmem)


def _dot(a, b):
    return jnp.dot(a, b, preferred_element_type=F32)


def _dot_nt(a, b):
    return lax.dot_general(a, b, (((1,), (1,)), ((), ())), preferred_element_type=F32)


def _split3(a):
    a1 = a.astype(BF16)
    r1 = a - a1.astype(F32)
    a2 = r1.astype(BF16)
    a3 = (r1 - a2.astype(F32)).astype(BF16)
    return a1, a2, a3


def _dot_hi(a, b):
    a1, a2, _ = _split3(a)
    b1, b2, _ = _split3(b)
    return _dot(a1, b1) + (_dot(a1, b2) + _dot(a2, b1))


def _silu(x):
    return x * jax.nn.sigmoid(x)


def _rms(x, w):
    return x * lax.rsqrt(jnp.mean(x * x, axis=-1, keepdims=True) + EPS) * w


def _ada_kernel(c_ref, w_ref, b_ref, o_ref):
    o_ref[...] = _dot_hi(_silu(c_ref[...]), w_ref[...]) + b_ref[...]


def _ada(cond8, w_ada, b_ada):
    n = w_ada.shape[1]
    tn = 1536
    return pl.pallas_call(
        _ada_kernel,
        grid=(n // tn,),
        in_specs=[pl.BlockSpec((8, D_MODEL), lambda j: (0, 0)),
                  pl.BlockSpec((D_MODEL, tn), lambda j: (0, j)),
                  pl.BlockSpec((1, tn), lambda j: (0, j))],
        out_specs=pl.BlockSpec((8, tn), lambda j: (0, j)),
        out_shape=jax.ShapeDtypeStruct((8, n), F32),
        compiler_params=_cparams(("arbitrary",), VMEM_LIMIT),
        name="ada",
    )(cond8, w_ada, b_ada.reshape(1, n))


def _mod_spec(col, tm, row0, rows_per_cond):
    return pl.BlockSpec((1, 1, D_MODEL), lambda i, *_: (row0 + (i * tm) // rows_per_cond, 0, col))


def _prenorm_kernel(x_ref, sh_ref, sc_ref, g_ref, wt_ref, o_ref, dtt_ref):
    xn = _rms(x_ref[...], g_ref[...])
    h = (xn * (1.0 + sc_ref[0]) + sh_ref[0]).astype(BF16)
    o_ref[...] = h
    dtt_ref[...] = _dot_nt(wt_ref[...], h)


def _prenorm(x, mod3, gain, w_dt_t, row0, rows_per_cond):
    t = x.shape[0]
    tm = 1024
    return pl.pallas_call(
        _prenorm_kernel,
        grid=(t // tm,),
        in_specs=[pl.BlockSpec((tm, D_MODEL), lambda i: (i, 0)),
                  _mod_spec(0, tm, row0, rows_per_cond),
                  _mod_spec(1, tm, row0, rows_per_cond),
                  pl.BlockSpec((1, D_MODEL), lambda i: (0, 0)),
                  pl.BlockSpec((LANES, D_MODEL), lambda i: (0, 0))],
        out_specs=[pl.BlockSpec((tm, D_MODEL), lambda i: (i, 0)),
                   pl.BlockSpec((LANES, tm), lambda i: (0, i))],
        out_shape=[jax.ShapeDtypeStruct((t, D_MODEL), BF16), jax.ShapeDtypeStruct((LANES, t), F32)],
        compiler_params=_cparams(("parallel",), VMEM_LIMIT),
        name="prenorm",
    )(x, mod3, mod3, gain.reshape(1, D_MODEL), w_dt_t)


INPROJ_TM = 4096
INPROJ_TN = 512


def _conv_silu(x, w, b, n):
    rows = x.shape[0]
    pos = lax.broadcasted_iota(I32, x.shape, 0) % n
    prev = jnp.where(pos == 0, 0.0, pltpu.roll(x, 1, axis=0))
    nxt = jnp.where(pos == n - 1, 0.0, pltpu.roll(x, rows - 1, axis=0))
    return _silu(prev * w[0:1] + x * w[1:2] + nxt * w[2:3] + b)


def _inproj_kernel(blk_ref, nxt_ref, hc_ref, hl_ref, w_ref, wn_ref, cw_ref, cb_ref, oc_ref, ol_ref, wbf_ref,
                   *, sh_lo, sh_hi, shift, n_ctx_tiles, cv_lo, cv_hi, n_c, n_l):
    del blk_ref, nxt_ref
    j = pl.program_id(0)
    i = pl.program_id(1)
    shifted = jnp.logical_and(j >= sh_lo, j < sh_hi)
    conv = jnp.logical_and(j >= cv_lo, j < cv_hi)

    if shift:
        @pl.when(jnp.logical_and(i == 0, shifted))
        def _():
            w = jnp.concatenate([w_ref[shift:, :], wn_ref[:shift, :]], axis=0)
            wbf_ref[...] = w.T.astype(BF16)

    @pl.when(jnp.logical_and(i == 0, jnp.logical_not(shifted)))
    def _():
        wbf_ref[...] = w_ref[...].T.astype(BF16)

    def ctx_rows():
        tm = oc_ref.shape[0]
        return pl.ds(pl.multiple_of(jnp.minimum(i, n_ctx_tiles - 1) * tm, tm), tm)

    is_ctx = i < n_ctx_tiles
    plain = jnp.logical_not(conv)

    @pl.when(jnp.logical_and(is_ctx, plain))
    def _():
        oc_ref[...] = _dot(hc_ref[ctx_rows(), :], wbf_ref[...]).astype(oc_ref.dtype)

    @pl.when(jnp.logical_and(jnp.logical_not(is_ctx), plain))
    def _():
        ol_ref[...] = _dot(hl_ref[...], wbf_ref[...]).astype(ol_ref.dtype)

    if cv_hi > cv_lo:
        @pl.when(jnp.logical_and(is_ctx, conv))
        def _():
            y = _dot(hc_ref[ctx_rows(), :], wbf_ref[...])
            oc_ref[...] = _conv_silu(y, cw_ref[...], cb_ref[...], n_c).astype(oc_ref.dtype)

        @pl.when(jnp.logical_and(jnp.logical_not(is_ctx), conv))
        def _():
            y = _dot(hl_ref[...], wbf_ref[...])
            ol_ref[...] = _conv_silu(y, cw_ref[...], cb_ref[...], n_l).astype(ol_ref.dtype)


def _inproj(h_c, h_l, w_t, w_rows, out_dtype, conv_w, conv_b, seq_lens, shifted_tiles=(0, 0), conv_tiles=(0, 0)):
    tm, tn = INPROJ_TM, INPROJ_TN
    t_c, k = h_c.shape
    t_l = h_l.shape[0]
    assert t_c % tm == 0
    nct = t_c // tm
    ncol = len(w_rows)
    sh_lo, sh_hi = shifted_tiles
    shifts = {w_rows[j] % tn for j in range(sh_lo, sh_hi)}
    assert len(shifts) <= 1 and all(w_rows[j] % tn == 0 for j in range(ncol) if not sh_lo <= j < sh_hi)
    shift = shifts.pop() if shifts else 0
    assert shift % 8 == 0
    blocks = [r // tn for r in w_rows]
    nxt = [b + 1 if sh_lo <= j < sh_hi else blocks[sh_lo] if sh_hi > sh_lo else 0 for j, b in enumerate(blocks)]
    cv_lo, cv_hi = conv_tiles
    assert conv_w.shape[1] >= (cv_hi - cv_lo) * tn
    conv_col = lambda j, i, blk, nx: (0, jnp.clip(j - cv_lo, 0, max(cv_hi - cv_lo - 1, 0)))
    return pl.pallas_call(
        functools.partial(_inproj_kernel, sh_lo=sh_lo, sh_hi=sh_hi, shift=shift, n_ctx_tiles=nct,
                          cv_lo=cv_lo, cv_hi=cv_hi, n_c=seq_lens[0], n_l=seq_lens[1]),
        grid_spec=pltpu.PrefetchScalarGridSpec(
            num_scalar_prefetch=2,
            grid=(ncol, nct + 1),
            in_specs=[pl.BlockSpec((t_c, k), lambda j, i, blk, nx: (0, 0), pipeline_mode=pl.Buffered(1)),
                      pl.BlockSpec((t_l, k), lambda j, i, blk, nx: (0, 0), pipeline_mode=pl.Buffered(1)),
                      pl.BlockSpec((tn, k), lambda j, i, blk, nx: (blk[j], 0)),
                      pl.BlockSpec((tn, k), lambda j, i, blk, nx: (nx[j], 0)),
                      pl.BlockSpec((3, tn), conv_col),
                      pl.BlockSpec((1, tn), conv_col)],
            out_specs=[pl.BlockSpec((tm, tn), lambda j, i, blk, nx: (jnp.minimum(i, nct - 1), j)),
                       pl.BlockSpec((t_l, tn), lambda j, i, blk, nx: (0, j))],
            scratch_shapes=[pltpu.VMEM((k, tn), BF16)]),
        out_shape=[jax.ShapeDtypeStruct((t_c, ncol * tn), out_dtype),
                   jax.ShapeDtypeStruct((t_l, ncol * tn), BF16)],
        compiler_params=_cparams(("arbitrary", "arbitrary"), VMEM_LIMIT),
        name="inproj",
    )(jnp.asarray(blocks, I32), jnp.asarray(nxt, I32), h_c, h_l, w_t, w_t, conv_w, conv_b)


def _lam(lp_ref, lam_init):
    lp = lp_ref[...]
    s1 = jnp.sum(lp[0:1] * lp[1:2], axis=-1, keepdims=True)
    s2 = jnp.sum(lp[2:3] * lp[3:4], axis=-1, keepdims=True)
    return jnp.exp(s1) - jnp.exp(s2) + lam_init


ATTN_TQ = 64


def _with_ones(v):
    return jnp.concatenate([v, jnp.ones(v.shape, v.dtype)], axis=1)


def _diff_attn(q, k, v1, lam, wsub, lam_init):
    nq = q.shape[0]
    lane = lax.broadcasted_iota(I32, q.shape, 1)
    q = q * (A_HEAD_DIM ** -0.5 * LOG2E)
    q1 = jnp.where(lane < A_HEAD_DIM, q, 0.0).astype(BF16)
    q2 = jnp.where(lane >= A_HEAD_DIM, q, 0.0).astype(BF16)
    s = _dot_nt(jnp.concatenate([q1, q2], axis=0), k)
    p = jnp.exp2(s - jnp.max(s, axis=-1, keepdims=True)).astype(BF16)
    pv = _dot(p, v1)
    pv = pv[:, :A_V_DIM] / pv[:, A_V_DIM:]
    o = pv[:nq] - lam * pv[nq:]
    return _rms(o, wsub) * (1.0 - lam_init)


def _attn_ctx_kernel(lp_ref, q_ref, k_ref, v_ref, ws_ref, o_ref, *, lam_init, tq):
    lam = _lam(lp_ref, lam_init)
    for h in range(A_HEADS):
        sl = slice(h * LANES, (h + 1) * LANES)
        k = k_ref[:, sl].astype(BF16)
        v1 = _with_ones(v_ref[:, sl].astype(BF16))
        for i in range(q_ref.shape[0] // tq):
            rows = slice(i * tq, (i + 1) * tq)
            o = _diff_attn(q_ref[rows, sl].astype(F32), k, v1, lam, ws_ref[...], lam_init)
            o_ref[rows, sl] = o.astype(o_ref.dtype)


def _attn_ctx(lam_p, proj, k, v, wsub, nbatch, n, lam_init):
    w = A_HEADS * LANES
    blk = pl.BlockSpec((n, w), lambda b: (b, 0))
    return pl.pallas_call(
        functools.partial(_attn_ctx_kernel, lam_init=lam_init, tq=ATTN_TQ),
        grid=(nbatch,),
        in_specs=[pl.BlockSpec((4, A_HEAD_DIM), lambda b: (0, 0)),
                  pl.BlockSpec((n, w), lambda b: (b, PROJ_Q // w)), blk, blk,
                  pl.BlockSpec((1, A_V_DIM), lambda b: (0, 0))],
        out_specs=blk,
        out_shape=jax.ShapeDtypeStruct(k.shape, BF16),
        compiler_params=_cparams(("parallel",), VMEM_LIMIT),
        name="attn_ctx",
    )(lam_p, proj, k, v, wsub)


def _rope(x, cos, sin_signed):
    lane = lax.broadcasted_iota(I32, x.shape, 1)
    quarter = A_HEAD_DIM // 4
    partner = jnp.where((lane % (2 * quarter)) < quarter,
                        pltpu.roll(x, LANES - quarter, axis=1),
                        pltpu.roll(x, quarter, axis=1))
    return x * cos + partner * sin_signed


def _attn_lat_kernel(lp_ref, q_ref, k_ref, v_ref, kc_ref, vc_ref, cos_ref, sin_ref, ws_ref, o_ref, *, lam_init, tq):
    lam = _lam(lp_ref, lam_init)
    k = _rope(k_ref[...].astype(F32), cos_ref[...], sin_ref[...])
    k_all = jnp.concatenate([kc_ref[...].astype(BF16), k.astype(BF16)], axis=0)
    v_all = _with_ones(jnp.concatenate([vc_ref[...].astype(BF16), v_ref[...].astype(BF16)], axis=0))
    for i in range(q_ref.shape[0] // tq):
        rows = slice(i * tq, (i + 1) * tq)
        q = _rope(q_ref[rows, :].astype(F32), cos_ref[rows, :], sin_ref[rows, :])
        o_ref[rows, :] = _diff_attn(q, k_all, v_all, lam, ws_ref[...], lam_init).astype(o_ref.dtype)


def _attn_lat(lam_p, proj, k, v, kc, vc, cos, sin, wsub, nbatch, n, npast, lam_init):
    blk = pl.BlockSpec((n, LANES), lambda b, h: (b, h))
    cblk = pl.BlockSpec((npast, LANES), lambda b, h: (b, h))
    tab = pl.BlockSpec((n, LANES), lambda b, h: (0, 0))
    return pl.pallas_call(
        functools.partial(_attn_lat_kernel, lam_init=lam_init, tq=ATTN_TQ),
        grid=(nbatch, A_HEADS),
        in_specs=[pl.BlockSpec((4, A_HEAD_DIM), lambda b, h: (0, 0)),
                  pl.BlockSpec((n, LANES), lambda b, h: (b, PROJ_Q // LANES + h)), blk, blk, cblk, cblk, tab, tab,
                  pl.BlockSpec((1, A_V_DIM), lambda b, h: (0, 0))],
        out_specs=blk,
        out_shape=jax.ShapeDtypeStruct(k.shape, BF16),
        compiler_params=_cparams(("parallel", "parallel"), VMEM_LIMIT),
        name="attn_lat",
    )(lam_p, proj, k, v, kc, vc, cos, sin, wsub)


def _rope_tables(n):
    pos = np.arange(n)
    rowcol = np.stack([pos // GRID_W, pos % GRID_W], axis=-1).astype(np.float64)
    nf = A_HEAD_DIM // 4
    inv_freq = np.power(ROPE_BASE, -np.arange(nf, dtype=np.float64) / nf)
    lane = np.arange(LANES)
    axis = (lane % A_HEAD_DIM) // (2 * nf)
    ang = rowcol[:, axis] * inv_freq[lane % nf][None, :]
    sign = np.where((lane % (2 * nf)) < nf, -1.0, 1.0)
    return jnp.asarray(np.cos(ang), F32), jnp.asarray(np.sin(ang) * sign[None, :], F32)


S_PAIRS = S_INNER // LANES
SSD_ROWS = 64
SSD_STEP_CHUNKS = 2


def _ssd_direction(xa_ref, r0, dt_raw_t, dtb_col, alog_col, ht_ref, y_ref, backward):
    col0 = S_HEADS if backward else 0
    chunk = slice(r0, r0 + S_CHUNK)
    row = lax.broadcasted_iota(I32, (S_CHUNK, S_CHUNK), 0)
    col = lax.broadcasted_iota(I32, (S_CHUNK, S_CHUNK), 1)
    dt_t = jax.nn.softplus(dt_raw_t + dtb_col)
    da_t = dt_t * (-jnp.exp(alog_col) * LOG2E)
    if backward:
        last = 0
        keep = row <= col
    else:
        last = S_CHUNK - 1
        keep = row >= col
    tri = jnp.where(keep, 1.0, 0.0).astype(BF16)
    pieces = _split3(da_t)
    cum = sum(_dot_nt(tri, p) for p in pieces[1:]) + _dot_nt(tri, pieces[0])
    cum_t = sum(_dot_nt(p, tri) for p in pieces[1:]) + _dot_nt(pieces[0], tri)
    w_t = dt_t * jnp.exp2(cum_t[:, last:last + 1] - cum_t)
    e_tot = jnp.exp2(cum[last:last + 1, :])
    lo_half = col < S_HEADDIM
    lo_bf = jnp.where(lo_half, 1.0, 0.0).astype(BF16)
    hi_bf = jnp.where(lo_half, 0.0, 1.0).astype(BF16)
    for g in range(S_GROUPS):
        b_bf = xa_ref[chunk, S_INNER + g * S_STATE:S_INNER + (g + 1) * S_STATE]
        c_bf = xa_ref[chunk, S_INNER + (S_GROUPS + g) * S_STATE:S_INNER + (S_GROUPS + g + 1) * S_STATE]
        cb = _dot_nt(c_bf, b_bf)
        bt = b_bf.astype(F32).T
        c_f = c_bf.astype(F32)
        for pr in range(S_PAIRS // S_GROUPS):
            hp = g * (S_PAIRS // S_GROUPS) + pr
            c0 = col0 + 2 * hp
            x_pair = xa_ref[chunk, hp * LANES:(hp + 1) * LANES]
            bd_x = jnp.concatenate([x_pair * lo_bf, x_pair * hi_bf], axis=0)
            ht = ht_ref[hp]
            bd_h = jnp.concatenate([jnp.where(lo_half, ht, 0.0), jnp.where(lo_half, 0.0, ht)], axis=0).astype(BF16)
            rhs_y = jnp.concatenate([bd_x, bd_h], axis=0)
            bws = [(bt * w_t[cc:cc + 1, :]).astype(BF16) for cc in (c0, c0 + 1)]
            for lh in range(S_CHUNK // SSD_ROWS):
                rows = slice(lh * SSD_ROWS, (lh + 1) * SSD_ROWS)
                ms, ces = [], []
                for cc in (c0, c0 + 1):
                    bc = jnp.broadcast_to(cum[rows, cc:cc + 1], (SSD_ROWS, S_CHUNK))
                    decay = jnp.exp2(jnp.where(keep[rows], bc - cum_t[cc:cc + 1, :], -1e30))
                    ms.append((cb[rows] * decay * dt_t[cc:cc + 1, :]).astype(BF16))
                    ces.append((c_f[rows] * jnp.exp2(bc)).astype(BF16))
                y_ref[r0 + lh * SSD_ROWS:r0 + (lh + 1) * SSD_ROWS, hp * LANES:(hp + 1) * LANES] = _dot(jnp.concatenate(ms + ces, axis=1),
                                                                rhs_y).astype(y_ref.dtype)
            tot = jnp.where(lo_half[0:1], e_tot[:, c0:c0 + 1], e_tot[:, c0 + 1:c0 + 2])
            ht_ref[hp] = ht * tot + _dot(jnp.concatenate(bws, axis=1), bd_x)


def _ssd_kernel(*refs, has_h0, emit_final):
    xf_ref, xb_ref, dtft_ref, dtbt_ref, bc_ref, ac_ref = refs[:6]
    refs = refs[6:]
    if has_h0:
        h0_ref, refs = refs[0], refs[1:]
    yf_ref, yb_ref = refs[:2]
    refs = refs[2:]
    if emit_final:
        fin_ref, refs = refs[0], refs[1:]
    htf_ref, htb_ref = refs
    c = pl.program_id(1)
    nc = pl.num_programs(1)

    @pl.when(c == 0)
    def _():
        for d, ht_ref in enumerate((htf_ref, htb_ref)):
            for hp in range(S_PAIRS):
                if has_h0:
                    ht_ref[hp] = h0_ref[0, d, hp * LANES:(hp + 1) * LANES, :].T
                else:
                    ht_ref[hp] = jnp.zeros((S_STATE, LANES), F32)

    for k in range(SSD_STEP_CHUNKS):
        rf = k * S_CHUNK
        rb = (SSD_STEP_CHUNKS - 1 - k) * S_CHUNK
        _ssd_direction(xf_ref, rf, dtft_ref[:, rf:rf + S_CHUNK], bc_ref[...], ac_ref[...], htf_ref, yf_ref,
                       backward=False)
        _ssd_direction(xb_ref, rb, dtbt_ref[:, rb:rb + S_CHUNK], bc_ref[...], ac_ref[...], htb_ref, yb_ref,
                       backward=True)

    if emit_final:
        @pl.when(c == nc - 1)
        def _():
            for d, ht_ref in enumerate((htf_ref, htb_ref)):
                for hp in range(S_PAIRS):
                    fin_ref[0, d, hp * LANES:(hp + 1) * LANES, :] = ht_ref[hp].T


def _ssd(xa, dt_t, dtb_col, alog_col, h0, nbatch, n, emit_final):
    rows = SSD_STEP_CHUNKS * S_CHUNK
    nc = n // rows
    t = xa.shape[0]
    fwd = lambda b, c: (b * nc + c, 0)
    bwd = lambda b, c: (b * nc + nc - 1 - c, 0)
    const = lambda b, c: (0, 0)
    in_specs = [pl.BlockSpec((rows, S_CONV_CH), fwd),
                pl.BlockSpec((rows, S_CONV_CH), bwd),
                pl.BlockSpec((LANES, rows), lambda b, c: (0, b * nc + c)),
                pl.BlockSpec((LANES, rows), lambda b, c: (0, b * nc + nc - 1 - c)),
                pl.BlockSpec((LANES, 1), const), pl.BlockSpec((LANES, 1), const)]
    args = [xa, xa, dt_t, dt_t, dtb_col, alog_col]
    if h0 is not None:
        in_specs.append(pl.BlockSpec((1, 2, S_INNER, S_STATE), lambda b, c: (b, 0, 0, 0)))
        args.append(h0)
    out_specs = [pl.BlockSpec((rows, S_INNER), fwd), pl.BlockSpec((rows, S_INNER), bwd)]
    out_shape = [jax.ShapeDtypeStruct((t, S_INNER), BF16), jax.ShapeDtypeStruct((t, S_INNER), BF16)]
    if emit_final:
        out_specs.append(pl.BlockSpec((1, 2, S_INNER, S_STATE), lambda b, c: (b, 0, 0, 0)))
        out_shape.append(jax.ShapeDtypeStruct((nbatch, 2, S_INNER, S_STATE), F32))
    return pl.pallas_call(
        functools.partial(_ssd_kernel, has_h0=h0 is not None, emit_final=emit_final),
        grid=(nbatch, nc),
        in_specs=in_specs,
        out_specs=out_specs,
        out_shape=out_shape,
        scratch_shapes=[pltpu.VMEM((S_PAIRS, S_STATE, LANES), F32),
                        pltpu.VMEM((S_PAIRS, S_STATE, LANES), F32)],
        compiler_params=_cparams(("parallel", "arbitrary"), VMEM_LIMIT),
        name="ssd",
    )(*args)


MIXOUT_SPLIT = 2


def _mixout_kernel(yf_ref, yb_ref, xs_ref, z_ref, oa_ref, ga_ref, gs_ref, x_ref, g1_ref, sh2_ref, sc2_ref,
                   dskip_ref, wn_ref, woa_ref, wos_ref, wout_ref, gpost_ref, gpre_ref, wr_ref,
                   x1_ref, h2_ref, aff_ref):
    wn = wn_ref[...]
    sub = x_ref.shape[0] // MIXOUT_SPLIT
    for r in range(MIXOUT_SPLIT):
        rows = slice(r * sub, (r + 1) * sub)
        y = (yf_ref[rows, :].astype(F32) + yb_ref[rows, :].astype(F32)
             + xs_ref[rows, :].astype(F32) * dskip_ref[...])
        y = y * _silu(z_ref[rows, :].astype(F32))
        parts = []
        for g in range(S_GROUPS):
            sl = slice(g * S_GROUP_W, (g + 1) * S_GROUP_W)
            parts.append(_rms(y[:, sl], wn[:, sl]).astype(BF16))
        o_ssd = jnp.concatenate(parts, axis=1)
        a = _dot(oa_ref[rows, :], woa_ref[...])
        s = _dot(o_ssd, wos_ref[...])
        merged = (jax.nn.sigmoid(ga_ref[rows, :].astype(F32)) * a
                  + jax.nn.sigmoid(gs_ref[rows, :].astype(F32)) * s)
        ym = _dot(merged.astype(BF16), wout_ref[...])
        x1 = x_ref[rows, :] + g1_ref[0] * _rms(ym, gpost_ref[...])
        x1_ref[rows, :] = x1
        h2 = _rms(x1, gpre_ref[...]) * (1.0 + sc2_ref[0]) + sh2_ref[0]
        h2_ref[rows, :] = h2.astype(BF16)
        logits = _dot_hi(h2, wr_ref[...])
        lane = lax.broadcasted_iota(I32, logits.shape, 1)
        logits = jnp.where(lane < N_EXPERTS, logits, -1e30)
        p = jnp.exp(logits - jnp.max(logits, axis=-1, keepdims=True))
        aff_ref[rows, :] = p / jnp.sum(p, axis=-1, keepdims=True)


def _mixout(yf, yb, xa, proj, o_attn, x, mod3, dskip, wn, woa, wos, wout, gpost, gpre, wr, row0, rows_per_cond):
    t = x.shape[0]
    tm = 512
    tok = lambda w: pl.BlockSpec((tm, w), lambda i: (i, 0))
    const = lambda a: pl.BlockSpec(a.shape, lambda i: (0,) * a.ndim, pipeline_mode=pl.Buffered(1))
    return pl.pallas_call(
        _mixout_kernel,
        grid=(t // tm,),
        in_specs=[tok(S_INNER), tok(S_INNER), tok(S_INNER),
                  pl.BlockSpec((tm, S_INNER), lambda i: (i, PROJ_Z // S_INNER)), tok(D_MODEL),
                  pl.BlockSpec((tm, D_MODEL), lambda i: (i, PROJ_GATES // D_MODEL)),
                  pl.BlockSpec((tm, D_MODEL), lambda i: (i, PROJ_GATES // D_MODEL + 1)),
                  tok(D_MODEL),
                  _mod_spec(2, tm, row0, rows_per_cond),
                  _mod_spec(3, tm, row0, rows_per_cond),
                  _mod_spec(4, tm, row0, rows_per_cond),
                  const(dskip), const(wn), const(woa), const(wos), const(wout), const(gpost), const(gpre),
                  const(wr)],
        out_specs=[tok(D_MODEL), tok(D_MODEL), tok(LANES)],
        out_shape=[jax.ShapeDtypeStruct((t, D_MODEL), F32), jax.ShapeDtypeStruct((t, D_MODEL), BF16),
                   jax.ShapeDtypeStruct((t, LANES), F32)],
        compiler_params=_cparams(("parallel",), VMEM_LIMIT),
        name="mixout",
    )(yf, yb, xa, proj, o_attn, proj, proj, x, mod3, mod3, mod3, dskip, wn, woa, wos, wout, gpost, gpre, wr)


COMB_TILE = 256
COMB_WIN = 64
BF16_ROWS = 16


def _router_kernel(aff_ref, pos_ref, cnt_ref, *, cap):
    aff = aff_ref[...]
    e, t = aff.shape
    idx = lax.broadcasted_iota(I32, (e, t), 1)

    def count(m):
        return jnp.sum(jnp.where(m, 1.0, 0.0), axis=-1, keepdims=True)

    def value_bit(i, thr):
        cand = thr | jnp.left_shift(jnp.int32(1), 30 - i)
        return jnp.where(count(aff >= pltpu.bitcast(cand, F32)) >= cap, cand, thr)

    thr = pltpu.bitcast(lax.fori_loop(0, 31, value_bit, jnp.zeros((e, 1), I32)), F32)
    gt = aff > thr
    eq = aff == thr
    need = cap - count(gt)
    nbits = (t - 1).bit_length()

    def index_bit(i, v):
        cand = v | jnp.left_shift(jnp.int32(1), nbits - 1 - i)
        return jnp.where(count(jnp.logical_and(eq, idx < cand)) < need, cand, v)

    last = lax.fori_loop(0, nbits, index_bit, jnp.zeros((e, 1), I32))
    sel = jnp.logical_or(gt, jnp.logical_and(eq, idx <= last))
    sel_bf = jnp.where(sel, 1.0, 0.0).astype(BF16)
    blk = 512
    r = lax.broadcasted_iota(I32, (blk, blk), 0)
    c = lax.broadcasted_iota(I32, (blk, blk), 1)
    upper = jnp.where(r <= c, 1.0, 0.0).astype(BF16)
    carry = jnp.zeros((e, 1), F32)
    tile_cnt = []
    for b in range(t // blk):
        cum = _dot(sel_bf[:, b * blk:(b + 1) * blk], upper) + carry
        carry = cum[:, blk - 1:blk]
        slot = jnp.where(sel[:, b * blk:(b + 1) * blk], cum - 1.0, -1.0)
        pos_ref[:, b * blk:(b + 1) * blk] = slot.astype(I32)
        for j in range(blk // COMB_TILE):
            tile_cnt.append(cum[:, (j + 1) * COMB_TILE - 1:(j + 1) * COMB_TILE])
    lane = lax.broadcasted_iota(I32, (e, LANES), 1)
    cnt = jnp.zeros((e, LANES), F32)
    for i, v in enumerate(tile_cnt):
        cnt = jnp.where(lane == i, v, cnt)
    cnt_ref[...] = cnt.astype(I32)


class _Route(NamedTuple):
    pos: jax.Array
    start: jax.Array
    end: jax.Array
    win: jax.Array
    fits: jax.Array


def _router(aff_t, cap):
    e, t = aff_t.shape
    ntile = t // COMB_TILE
    pos, cnt = pl.pallas_call(
        functools.partial(_router_kernel, cap=cap),
        out_shape=[jax.ShapeDtypeStruct((e, t), I32), jax.ShapeDtypeStruct((e, LANES), I32)],
        name="router",
    )(aff_t)
    end = cnt[:, :ntile]
    start = jnp.concatenate([jnp.zeros((e, 1), I32), end[:, :-1]], axis=1)
    win = jnp.minimum(start // BF16_ROWS * BF16_ROWS, cap - COMB_WIN)
    fits = jnp.all(end - win <= COMB_WIN, axis=0).astype(I32)
    flat = lambda a: a.T.reshape(ntile * e)
    return _Route(pos, flat(start), flat(end), flat(win), fits)


def _dispatch_kernel(start_ref, end_ref, win_ref, fits_ref, pos_ref, h_ref, o_ref):
    i = pl.program_id(0)
    ne, cap, _ = o_ref.shape
    tm = h_ref.shape[0]

    @pl.when(i == 0)
    def _():
        o_ref[...] = jnp.zeros(o_ref.shape, o_ref.dtype)

    h = h_ref[...]
    row = lax.broadcasted_iota(I32, (COMB_WIN, tm), 0)
    row_l = lax.broadcasted_iota(I32, (COMB_WIN, LANES), 0)

    def place(e, r0, rows):
        rel = r0 - start_ref[i * ne + e] + row_l
        mine = jnp.logical_and(rel >= 0, rel < end_ref[i * ne + e] - start_ref[i * ne + e])
        for c in range(D_MODEL // LANES):
            sl = slice(c * LANES, (c + 1) * LANES)
            old = o_ref[e, pl.ds(r0, COMB_WIN), sl].astype(F32)
            o_ref[e, pl.ds(r0, COMB_WIN), sl] = jnp.where(mine, rows[:, sl], old).astype(o_ref.dtype)

    @pl.when(fits_ref[i] == 1)
    def _():
        sel = [jnp.where(pos_ref[e:e + 1, :] == win_ref[i * ne + e] + row, 1.0, 0.0).astype(BF16) for e in range(ne)]
        rows = _dot(jnp.concatenate(sel, axis=0), h)
        for e in range(ne):
            place(e, pl.multiple_of(win_ref[i * ne + e], BF16_ROWS), rows[e * COMB_WIN:(e + 1) * COMB_WIN])

    @pl.when(fits_ref[i] == 0)
    def _():
        for e in range(ne):
            w0 = win_ref[i * ne + e]

            def window(w, carry):
                r0 = pl.multiple_of(jnp.minimum(w0 + w * COMB_WIN, cap - COMB_WIN), BF16_ROWS)
                sel = jnp.where(pos_ref[e:e + 1, :] == r0 + row, 1.0, 0.0).astype(BF16)
                place(e, r0, _dot(sel, h))
                return carry

            lax.fori_loop(0, (end_ref[i * ne + e] - w0 + COMB_WIN - 1) // COMB_WIN, window, 0)


def _dispatch(route, h2, cap):
    e, t = route.pos.shape
    tm = COMB_TILE
    return pl.pallas_call(
        _dispatch_kernel,
        grid_spec=pltpu.PrefetchScalarGridSpec(
            num_scalar_prefetch=4,
            grid=(t // tm,),
            in_specs=[pl.BlockSpec((e, tm), lambda i, *_: (0, i)),
                      pl.BlockSpec((tm, D_MODEL), lambda i, *_: (i, 0))],
            out_specs=pl.BlockSpec((e, cap, D_MODEL), lambda i, *_: (0, 0, 0))),
        out_shape=jax.ShapeDtypeStruct((e, cap, D_MODEL), BF16),
        compiler_params=_cparams(("arbitrary",), VMEM_LIMIT),
        name="dispatch",
    )(route.start, route.end, route.win, route.fits, route.pos, h2)


def _ffn_kernel(xc_ref, xl_ref, wg_ref, wu_ref, wd_ref, oc_ref, ol_ref, acc_ref):
    f = pl.program_id(1)
    cc = xc_ref.shape[1]
    xe = jnp.concatenate([xc_ref[0], xl_ref[0]], axis=0)
    hg = _dot(xe, wg_ref[0].astype(BF16))
    hu = _dot(xe, wu_ref[0].astype(BF16))
    part = _dot((_silu(hg) * hu).astype(BF16), wd_ref[0].astype(BF16))

    @pl.when(f == 0)
    def _():
        acc_ref[...] = part

    @pl.when(f > 0)
    def _():
        acc_ref[...] += part

    @pl.when(f == pl.num_programs(1) - 1)
    def _():
        oc_ref[0] = acc_ref[:cc, :].astype(BF16)
        ol_ref[0] = acc_ref[cc:, :].astype(BF16)


def _ffn(xe_c, xe_l, w_gate, w_up, w_down):
    e, cc, d = xe_c.shape
    cl = xe_l.shape[1]
    tf = 1024
    return pl.pallas_call(
        _ffn_kernel,
        grid=(e, D_FF // tf),
        in_specs=[pl.BlockSpec((1, cc, d), lambda i, f: (i, 0, 0)),
                  pl.BlockSpec((1, cl, d), lambda i, f: (i, 0, 0)),
                  pl.BlockSpec((1, d, tf), lambda i, f: (i, 0, f)),
                  pl.BlockSpec((1, d, tf), lambda i, f: (i, 0, f)),
                  pl.BlockSpec((1, tf, d), lambda i, f: (i, f, 0))],
        out_specs=[pl.BlockSpec((1, cc, d), lambda i, f: (i, 0, 0)),
                   pl.BlockSpec((1, cl, d), lambda i, f: (i, 0, 0))],
        out_shape=[jax.ShapeDtypeStruct((e, cc, d), BF16), jax.ShapeDtypeStruct((e, cl, d), BF16)],
        scratch_shapes=[pltpu.VMEM((cc + cl, d), F32)],
        compiler_params=_cparams(("parallel", "arbitrary"), VMEM_LIMIT),
        name="ffn",
    )(xe_c, xe_l, w_gate, w_up, w_down)


def _combine_kernel(win_ref, fits_ref, pos_ref, aff_ref, ye_ref, x1_ref, g2_ref, gp_ref, o_ref, rhs_ref, y_ref):
    i = pl.program_id(0)
    tm = pos_ref.shape[0]
    cap = ye_ref.shape[1]
    pos = pos_ref[...]
    aff = aff_ref[...]

    @pl.when(fits_ref[i] == 1)
    def _():
        per_col = LANES // COMB_WIN
        lane = lax.broadcasted_iota(I32, (tm, LANES), 1)
        cols = []
        for c in range(N_EXPERTS // per_col):
            col = jnp.zeros((tm, LANES), F32)
            for j in range(per_col):
                e = c * per_col + j
                w0 = pl.multiple_of(win_ref[i * N_EXPERTS + e], BF16_ROWS)
                rhs_ref[e * COMB_WIN:(e + 1) * COMB_WIN, :] = ye_ref[e, pl.ds(w0, COMB_WIN), :]
                p = pos[:, e:e + 1]
                tgt = jnp.where(p >= 0, p - w0 + j * COMB_WIN, -1)
                col = jnp.where(lane == tgt, aff[:, e:e + 1], col)
            cols.append(col)
        sel = jnp.concatenate(cols, axis=1)
        hi = sel.astype(BF16)
        lo = (sel - hi.astype(F32)).astype(BF16)
        both = _dot(jnp.concatenate([hi, lo], axis=0), rhs_ref[...])
        y_ref[...] = both[:tm] + both[tm:]

    @pl.when(fits_ref[i] == 0)
    def _():
        slot = lax.broadcasted_iota(I32, (tm, cap), 1)
        y = jnp.zeros((tm, D_MODEL), F32)
        for e in range(N_EXPERTS):
            onehot = jnp.where(pos[:, e:e + 1] == slot, 1.0, 0.0).astype(BF16)
            y = y + aff[:, e:e + 1] * _dot(onehot, ye_ref[e])
        y_ref[...] = y

    o_ref[...] = x1_ref[...] + g2_ref[0] * _rms(y_ref[...], gp_ref[...])


def _combine(route, aff, ye, x1, mod3, gpost, row0, rows_per_cond):
    t = x1.shape[0]
    pos_t, win, fits = route.pos.T, route.win, route.fits
    e, cap, d = ye.shape
    tm = COMB_TILE
    return pl.pallas_call(
        _combine_kernel,
        grid_spec=pltpu.PrefetchScalarGridSpec(
            num_scalar_prefetch=2,
            grid=(t // tm,),
            in_specs=[pl.BlockSpec((tm, e), lambda i, *_: (i, 0)),
                      pl.BlockSpec((tm, LANES), lambda i, *_: (i, 0)),
                      pl.BlockSpec((e, cap, d), lambda i, *_: (0, 0, 0)),
                      pl.BlockSpec((tm, d), lambda i, *_: (i, 0)),
                      _mod_spec(5, tm, row0, rows_per_cond),
                      pl.BlockSpec((1, d), lambda i, *_: (0, 0))],
            out_specs=pl.BlockSpec((tm, d), lambda i, *_: (i, 0)),
            scratch_shapes=[pltpu.VMEM((e * COMB_WIN, d), BF16), pltpu.VMEM((tm, d), F32)]),
        out_shape=jax.ShapeDtypeStruct((t, d), F32),
        compiler_params=_cparams(("parallel",), VMEM_LIMIT),
        name="combine",
    )(win, fits, pos_t, aff, ye, x1, mod3, gpost.reshape(1, d))


def _head_col(v):
    return jnp.concatenate([v.reshape(2 * S_HEADS), jnp.zeros((LANES - 2 * S_HEADS,), F32)]).reshape(LANES, 1)


def kernel(x_prompt, x_sample, cache_k, cache_v, state_ssd, c, c_ctx, w_ada, b_ada, g_pre_mix, g_post_mix, g_pre_ffn, g_post_ffn, w_in, lam_q1, lam_k1, lam_q2, lam_k2, w_subln, conv_w, conv_b, dt_bias, a_log, d_skip, w_ssd_norm, w_o_attn, w_o_ssd, w_out, w_router, w_gate, w_up, w_down):
    nb_c, n_c, d = x_prompt.shape
    nb_l, n_l, _ = x_sample.shape
    npast = cache_k.shape[2]
    depth = w_in.shape[0]
    assert depth == 1 and d == D_MODEL
    t_c, t_l = nb_c * n_c, nb_l * n_l
    lyr = 0
    lam_init = 0.8 - 0.6 * math.exp(-0.3 * lyr)

    cond8 = jnp.concatenate([c_ctx[None, :], c, jnp.zeros((8 - 1 - nb_l, d), F32)], axis=0)
    mod3 = _ada(cond8, w_ada[lyr], b_ada[lyr]).reshape(8, 1, N_MOD * d)

    w_in_t = w_in[lyr].T
    w_dt_t = jnp.concatenate([w_in_t[IN_DT:IN_GATES], jnp.zeros((LANES - 2 * S_HEADS, d), F32)], axis=0).astype(BF16)
    tn = INPROJ_TN
    rows_of = lambda col0, width: [col0 + j * tn for j in range(width // tn)]
    proj_rows = rows_of(IN_XBC, S_CONV_CH) + rows_of(IN_Q, d) + rows_of(IN_Z, S_INNER) + rows_of(IN_GATES, 2 * d)
    gate_tiles = (PROJ_GATES // tn, (PROJ_GATES + 2 * d) // tn)
    conv_tiles = (PROJ_XBC // tn, (PROJ_XBC + S_CONV_CH) // tn)
    cw, cb = conv_w[lyr], conv_b[lyr].reshape(1, S_CONV_CH)
    lam_p = jnp.stack([lam_q1[lyr], lam_k1[lyr], lam_q2[lyr], lam_k2[lyr]], axis=0)
    wsub = w_subln[lyr].reshape(1, A_V_DIM)
    dtb_col = _head_col(dt_bias[lyr])
    alog_col = _head_col(a_log[lyr])
    dskip = jnp.repeat(d_skip[lyr], S_HEADDIM).reshape(1, S_INNER)
    wn = w_ssd_norm[lyr].reshape(1, S_INNER)
    woa = w_o_attn[lyr].astype(BF16)
    wos = w_o_ssd[lyr].astype(BF16)
    wout = w_out[lyr].astype(BF16)
    wr = jnp.concatenate([w_router[lyr], jnp.zeros((d, LANES - N_EXPERTS), F32)], axis=1)
    gpost = g_post_mix[lyr].reshape(1, d)
    gpre = g_pre_ffn[lyr].reshape(1, d)
    cos, sin = _rope_tables(n_l)

    xc = x_prompt.reshape(t_c, d)
    xl = x_sample.reshape(t_l, d)
    h_c, dtt_c = _prenorm(xc, mod3, g_pre_mix[lyr], w_dt_t, 0, t_c)
    h_l, dtt_l = _prenorm(xl, mod3, g_pre_mix[lyr], w_dt_t, 1, n_l)
    proj_c, proj_l = _inproj(h_c, h_l, w_in_t, proj_rows, BF16, cw, cb, (n_c, n_l), gate_tiles, conv_tiles)
    k_c, k_l = _inproj(h_c, h_l, w_in_t, rows_of(IN_K, d), F32, cw, cb, (n_c, n_l))
    v_c, v_l = _inproj(h_c, h_l, w_in_t, rows_of(IN_V, d), F32, cw, cb, (n_c, n_l))

    def mixer(x, dt_t, proj, k, v, row0, rows_per_cond, nbatch, n, ctx):
        if ctx is None:
            o_attn = _attn_ctx(lam_p, proj, k, v, wsub, nbatch, n, lam_init)
            h0 = None
        else:
            kc, vc, h0 = ctx
            o_attn = _attn_lat(lam_p, proj, k, v, kc, vc, cos, sin, wsub, nbatch, n, npast, lam_init)
        xa = proj
        res = _ssd(xa, dt_t, dtb_col, alog_col, h0, nbatch, n, emit_final=ctx is None)
        yf, yb = res[0], res[1]
        x1, h2, aff = _mixout(yf, yb, xa, proj, o_attn, x, mod3, dskip, wn, woa, wos, wout, gpost, gpre, wr,
                              row0, rows_per_cond)
        fin = res[2] if ctx is None else None
        return x1, h2, aff, fin

    x1_c, h2_c, aff_c, fin_c = mixer(xc, dtt_c, proj_c, k_c, v_c, 0, t_c, nb_c, n_c, None)
    kc = cache_k[:, lyr].reshape(nb_l * npast, A_HEADS * 2 * A_HEAD_DIM)
    vc = cache_v[:, lyr].reshape(nb_l * npast, A_HEADS * A_V_DIM)
    h0 = state_ssd[:, lyr].reshape(nb_l, 2, S_INNER, S_STATE)
    x1_l, h2_l, aff_l, _ = mixer(xl, dtt_l, proj_l, k_l, v_l, 1, n_l, nb_l, n_l, (kc, vc, h0))

    cap_c = EC_FACTOR * t_c // N_EXPERTS
    cap_l = EC_FACTOR * t_l // N_EXPERTS
    route_c = _router(aff_c[:, :N_EXPERTS].T, cap_c)
    route_l = _router(aff_l[:, :N_EXPERTS].T, cap_l)
    xe_c = _dispatch(route_c, h2_c, cap_c)
    xe_l = _dispatch(route_l, h2_l, cap_l)
    ye_c, ye_l = _ffn(xe_c, xe_l, w_gate[lyr], w_up[lyr], w_down[lyr])
    out_c = _combine(route_c, aff_c, ye_c, x1_c, mod3, g_post_ffn[lyr], 0, t_c)
    out_l = _combine(route_l, aff_l, ye_l, x1_l, mod3, g_post_ffn[lyr], 1, n_l)

    y_prompt = out_c.reshape(nb_c, n_c, d)
    y_sample = out_l.reshape(nb_l, n_l, d)
    new_k = k_c.reshape(nb_c, 1, n_c, A_HEADS, 2 * A_HEAD_DIM)
    new_v = v_c.reshape(nb_c, 1, n_c, A_HEADS, A_V_DIM)
    new_state = fin_c.reshape(nb_c, 1, 2, S_HEADS, S_HEADDIM, S_STATE)
    return (y_prompt, y_sample, new_k, new_v, new_state)
```

```python
import functools
import math
from typing import NamedTuple

import jax
import jax.numpy as jnp
import numpy as np
from jax import lax
from jax.experimental import pallas as pl
from jax.experimental.pallas import tpu as pltpu

F32 = jnp.float32
BF16 = jnp.bfloat16
I32 = jnp.int32

D_MODEL = 1024
GRID_W = 64
EPS = 1e-6
N_MOD = 6
A_HEADS = 8
A_HEAD_DIM = 64
A_V_DIM = 128
ROPE_BASE = 10000.0
S_INNER = 2048
S_HEADDIM = 64
S_HEADS = 32
S_GROUPS = 4
S_GROUP_W = S_INNER // S_GROUPS
S_STATE = 128
S_CHUNK = 128
S_CONV_CH = S_INNER + 2 * S_GROUPS * S_STATE
N_EXPERTS = 16
EC_FACTOR = 2
D_FF = 2048
LANES = 128
IN_Q, IN_K, IN_V, IN_Z, IN_XBC = 0, 1024, 2048, 3072, 5120
IN_DT = IN_XBC + S_CONV_CH
IN_GATES = IN_DT + 2 * S_HEADS
PROJ_XBC, PROJ_Q, PROJ_Z, PROJ_GATES = 0, 3072, 4096, 6144
VMEM_LIMIT = 56 * 1024 * 1024
LOG2E = 1.4426950408889634


def _cparams(sem, vmem=None):
    return pltpu.CompilerParams(dimension_semantics=sem, vmem_limit_bytes=vmem)


def _dot(a, b):
    return jnp.dot(a, b, preferred_element_type=F32)


def _dot_nt(a, b):
    return lax.dot_general(a, b, (((1,), (1,)), ((), ())), preferred_element_type=F32)


def _split3(a):
    a1 = a.astype(BF16)
    r1 = a - a1.astype(F32)
    a2 = r1.astype(BF16)
    a3 = (r1 - a2.astype(F32)).astype(BF16)
    return a1, a2, a3


def _dot_hi(a, b):
    a1, a2, _ = _split3(a)
    b1, b2, _ = _split3(b)
    return _dot(a1, b1) + (_dot(a1, b2) + _dot(a2, b1))


def _silu(x):
    return x * jax.nn.sigmoid(x)


def _rms(x, w):
    return x * lax.rsqrt(jnp.mean(x * x, axis=-1, keepdims=True) + EPS) * w


def _ada_kernel(c_ref, w_ref, b_ref, o_ref):
    o_ref[...] = _dot_hi(_silu(c_ref[...]), w_ref[...]) + b_ref[...]


def _ada(cond8, w_ada, b_ada):
    n = w_ada.shape[1]
    tn = 1536
    return pl.pallas_call(
        _ada_kernel,
        grid=(n // tn,),
        in_specs=[pl.BlockSpec((8, D_MODEL), lambda j: (0, 0)),
                  pl.BlockSpec((D_MODEL, tn), lambda j: (0, j)),
                  pl.BlockSpec((1, tn), lambda j: (0, j))],
        out_specs=pl.BlockSpec((8, tn), lambda j: (0, j)),
        out_shape=jax.ShapeDtypeStruct((8, n), F32),
        compiler_params=_cparams(("arbitrary",), VMEM_LIMIT),
        name="ada",
    )(cond8, w_ada, b_ada.reshape(1, n))


def _mod_spec(col, tm, row0, rows_per_cond):
    return pl.BlockSpec((1, 1, D_MODEL), lambda i, *_: (row0 + (i * tm) // rows_per_cond, 0, col))


def _prenorm_kernel(x_ref, sh_ref, sc_ref, g_ref, wt_ref, o_ref, dtt_ref):
    xn = _rms(x_ref[...], g_ref[...])
    h = (xn * (1.0 + sc_ref[0]) + sh_ref[0]).astype(BF16)
    o_ref[...] = h
    dtt_ref[...] = _dot_nt(wt_ref[...], h)


def _prenorm(x, mod3, gain, w_dt_t, row0, rows_per_cond):
    t = x.shape[0]
    tm = 1024
    return pl.pallas_call(
        _prenorm_kernel,
        grid=(t // tm,),
        in_specs=[pl.BlockSpec((tm, D_MODEL), lambda i: (i, 0)),
                  _mod_spec(0, tm, row0, rows_per_cond),
                  _mod_spec(1, tm, row0, rows_per_cond),
                  pl.BlockSpec((1, D_MODEL), lambda i: (0, 0)),
                  pl.BlockSpec((LANES, D_MODEL), lambda i: (0, 0))],
        out_specs=[pl.BlockSpec((tm, D_MODEL), lambda i: (i, 0)),
                   pl.BlockSpec((LANES, tm), lambda i: (0, i))],
        out_shape=[jax.ShapeDtypeStruct((t, D_MODEL), BF16), jax.ShapeDtypeStruct((LANES, t), F32)],
        compiler_params=_cparams(("parallel",), VMEM_LIMIT),
        name="prenorm",
    )(x, mod3, mod3, gain.reshape(1, D_MODEL), w_dt_t)


INPROJ_TM = 4096
INPROJ_TN = 512


def _conv_silu(x, w, b, n):
    rows = x.shape[0]
    pos = lax.broadcasted_iota(I32, x.shape, 0) % n
    prev = jnp.where(pos == 0, 0.0, pltpu.roll(x, 1, axis=0))
    nxt = jnp.where(pos == n - 1, 0.0, pltpu.roll(x, rows - 1, axis=0))
    return _silu(prev * w[0:1] + x * w[1:2] + nxt * w[2:3] + b)


def _inproj_kernel(blk_ref, nxt_ref, hc_ref, hl_ref, w_ref, wn_ref, cw_ref, cb_ref, oc_ref, ol_ref, wbf_ref,
                   *, sh_lo, sh_hi, shift, n_ctx_tiles, cv_lo, cv_hi, n_c, n_l):
    del blk_ref, nxt_ref
    j = pl.program_id(0)
    i = pl.program_id(1)
    shifted = jnp.logical_and(j >= sh_lo, j < sh_hi)
    conv = jnp.logical_and(j >= cv_lo, j < cv_hi)

    if shift:
        @pl.when(jnp.logical_and(i == 0, shifted))
        def _():
            w = jnp.concatenate([w_ref[shift:, :], wn_ref[:shift, :]], axis=0)
            wbf_ref[...] = w.T.astype(BF16)

    @pl.when(jnp.logical_and(i == 0, jnp.logical_not(shifted)))
    def _():
        wbf_ref[...] = w_ref[...].T.astype(BF16)

    def ctx_rows():
        tm = oc_ref.shape[0]
        return pl.ds(pl.multiple_of(jnp.minimum(i, n_ctx_tiles - 1) * tm, tm), tm)

    is_ctx = i < n_ctx_tiles
    plain = jnp.logical_not(conv)

    @pl.when(jnp.logical_and(is_ctx, plain))
    def _():
        oc_ref[...] = _dot(hc_ref[ctx_rows(), :], wbf_ref[...]).astype(oc_ref.dtype)

    @pl.when(jnp.logical_and(jnp.logical_not(is_ctx), plain))
    def _():
        ol_ref[...] = _dot(hl_ref[...], wbf_ref[...]).astype(ol_ref.dtype)

    if cv_hi > cv_lo:
        @pl.when(jnp.logical_and(is_ctx, conv))
        def _():
            y = _dot(hc_ref[ctx_rows(), :], wbf_ref[...])
            oc_ref[...] = _conv_silu(y, cw_ref[...], cb_ref[...], n_c).astype(oc_ref.dtype)

        @pl.when(jnp.logical_and(jnp.logical_not(is_ctx), conv))
        def _():
            y = _dot(hl_ref[...], wbf_ref[...])
            ol_ref[...] = _conv_silu(y, cw_ref[...], cb_ref[...], n_l).astype(ol_ref.dtype)


def _inproj(h_c, h_l, w_t, w_rows, out_dtype, conv_w, conv_b, seq_lens, shifted_tiles=(0, 0), conv_tiles=(0, 0)):
    tm, tn = INPROJ_TM, INPROJ_TN
    t_c, k = h_c.shape
    t_l = h_l.shape[0]
    assert t_c % tm == 0
    nct = t_c // tm
    ncol = len(w_rows)
    sh_lo, sh_hi = shifted_tiles
    shifts = {w_rows[j] % tn for j in range(sh_lo, sh_hi)}
    assert len(shifts) <= 1 and all(w_rows[j] % tn == 0 for j in range(ncol) if not sh_lo <= j < sh_hi)
    shift = shifts.pop() if shifts else 0
    assert shift % 8 == 0
    blocks = [r // tn for r in w_rows]
    nxt = [b + 1 if sh_lo <= j < sh_hi else blocks[sh_lo] if sh_hi > sh_lo else 0 for j, b in enumerate(blocks)]
    cv_lo, cv_hi = conv_tiles
    assert conv_w.shape[1] >= (cv_hi - cv_lo) * tn
    conv_col = lambda j, i, blk, nx: (0, jnp.clip(j - cv_lo, 0, max(cv_hi - cv_lo - 1, 0)))
    return pl.pallas_call(
        functools.partial(_inproj_kernel, sh_lo=sh_lo, sh_hi=sh_hi, shift=shift, n_ctx_tiles=nct,
                          cv_lo=cv_lo, cv_hi=cv_hi, n_c=seq_lens[0], n_l=seq_lens[1]),
        grid_spec=pltpu.PrefetchScalarGridSpec(
            num_scalar_prefetch=2,
            grid=(ncol, nct + 1),
            in_specs=[pl.BlockSpec((t_c, k), lambda j, i, blk, nx: (0, 0), pipeline_mode=pl.Buffered(1)),
                      pl.BlockSpec((t_l, k), lambda j, i, blk, nx: (0, 0), pipeline_mode=pl.Buffered(1)),
                      pl.BlockSpec((tn, k), lambda j, i, blk, nx: (blk[j], 0)),
                      pl.BlockSpec((tn, k), lambda j, i, blk, nx: (nx[j], 0)),
                      pl.BlockSpec((3, tn), conv_col),
                      pl.BlockSpec((1, tn), conv_col)],
            out_specs=[pl.BlockSpec((tm, tn), lambda j, i, blk, nx: (jnp.minimum(i, nct - 1), j)),
                       pl.BlockSpec((t_l, tn), lambda j, i, blk, nx: (0, j))],
            scratch_shapes=[pltpu.VMEM((k, tn), BF16)]),
        out_shape=[jax.ShapeDtypeStruct((t_c, ncol * tn), out_dtype),
                   jax.ShapeDtypeStruct((t_l, ncol * tn), BF16)],
        compiler_params=_cparams(("arbitrary", "arbitrary"), VMEM_LIMIT),
        name="inproj",
    )(jnp.asarray(blocks, I32), jnp.asarray(nxt, I32), h_c, h_l, w_t, w_t, conv_w, conv_b)


def _lam(lp_ref, lam_init):
    lp = lp_ref[...]
    s1 = jnp.sum(lp[0:1] * lp[1:2], axis=-1, keepdims=True)
    s2 = jnp.sum(lp[2:3] * lp[3:4], axis=-1, keepdims=True)
    return jnp.exp(s1) - jnp.exp(s2) + lam_init


ATTN_TQ = 64


def _with_ones(v):
    return jnp.concatenate([v, jnp.ones(v.shape, v.dtype)], axis=1)


def _diff_attn(q, k, v1, lam, wsub, lam_init):
    nq = q.shape[0]
    lane = lax.broadcasted_iota(I32, q.shape, 1)
    q = q * (A_HEAD_DIM ** -0.5 * LOG2E)
    q1 = jnp.where(lane < A_HEAD_DIM, q, 0.0).astype(BF16)
    q2 = jnp.where(lane >= A_HEAD_DIM, q, 0.0).astype(BF16)
    s = _dot_nt(jnp.concatenate([q1, q2], axis=0), k)
    p = jnp.exp2(s - jnp.max(s, axis=-1, keepdims=True)).astype(BF16)
    pv = _dot(p, v1)
    pv = pv[:, :A_V_DIM] / pv[:, A_V_DIM:]
    o = pv[:nq] - lam * pv[nq:]
    return _rms(o, wsub) * (1.0 - lam_init)


def _attn_ctx_kernel(lp_ref, q_ref, k_ref, v_ref, ws_ref, o_ref, *, lam_init, tq):
    lam = _lam(lp_ref, lam_init)
    for h in range(A_HEADS):
        sl = slice(h * LANES, (h + 1) * LANES)
        k = k_ref[:, sl].astype(BF16)
        v1 = _with_ones(v_ref[:, sl].astype(BF16))
        for i in range(q_ref.shape[0] // tq):
            rows = slice(i * tq, (i + 1) * tq)
            o = _diff_attn(q_ref[rows, sl].astype(F32), k, v1, lam, ws_ref[...], lam_init)
            o_ref[rows, sl] = o.astype(o_ref.dtype)


def _attn_ctx(lam_p, proj, k, v, wsub, nbatch, n, lam_init):
    w = A_HEADS * LANES
    blk = pl.BlockSpec((n, w), lambda b: (b, 0))
    return pl.pallas_call(
        functools.partial(_attn_ctx_kernel, lam_init=lam_init, tq=ATTN_TQ),
        grid=(nbatch,),
        in_specs=[pl.BlockSpec((4, A_HEAD_DIM), lambda b: (0, 0)),
                  pl.BlockSpec((n, w), lambda b: (b, PROJ_Q // w)), blk, blk,
                  pl.BlockSpec((1, A_V_DIM), lambda b: (0, 0))],
        out_specs=blk,
        out_shape=jax.ShapeDtypeStruct(k.shape, BF16),
        compiler_params=_cparams(("parallel",), VMEM_LIMIT),
        name="attn_ctx",
    )(lam_p, proj, k, v, wsub)


def _rope(x, cos, sin_signed):
    lane = lax.broadcasted_iota(I32, x.shape, 1)
    quarter = A_HEAD_DIM // 4
    partner = jnp.where((lane % (2 * quarter)) < quarter,
                        pltpu.roll(x, LANES - quarter, axis=1),
                        pltpu.roll(x, quarter, axis=1))
    return x * cos + partner * sin_signed


def _attn_lat_kernel(lp_ref, q_ref, k_ref, v_ref, kc_ref, vc_ref, cos_ref, sin_ref, ws_ref, o_ref, *, lam_init, tq):
    lam = _lam(lp_ref, lam_init)
    k = _rope(k_ref[...].astype(F32), cos_ref[...], sin_ref[...])
    k_all = jnp.concatenate([kc_ref[...].astype(BF16), k.astype(BF16)], axis=0)
    v_all = _with_ones(jnp.concatenate([vc_ref[...].astype(BF16), v_ref[...].astype(BF16)], axis=0))
    for i in range(q_ref.shape[0] // tq):
        rows = slice(i * tq, (i + 1) * tq)
        q = _rope(q_ref[rows, :].astype(F32), cos_ref[rows, :], sin_ref[rows, :])
        o_ref[rows, :] = _diff_attn(q, k_all, v_all, lam, ws_ref[...], lam_init).astype(o_ref.dtype)


def _attn_lat(lam_p, proj, k, v, kc, vc, cos, sin, wsub, nbatch, n, npast, lam_init):
    blk = pl.BlockSpec((n, LANES), lambda b, h: (b, h))
    cblk = pl.BlockSpec((npast, LANES), lambda b, h: (b, h))
    tab = pl.BlockSpec((n, LANES), lambda b, h: (0, 0))
    return pl.pallas_call(
        functools.partial(_attn_lat_kernel, lam_init=lam_init, tq=ATTN_TQ),
        grid=(nbatch, A_HEADS),
        in_specs=[pl.BlockSpec((4, A_HEAD_DIM), lambda b, h: (0, 0)),
                  pl.BlockSpec((n, LANES), lambda b, h: (b, PROJ_Q // LANES + h)), blk, blk, cblk, cblk, tab, tab,
                  pl.BlockSpec((1, A_V_DIM), lambda b, h: (0, 0))],
        out_specs=blk,
        out_shape=jax.ShapeDtypeStruct(k.shape, BF16),
        compiler_params=_cparams(("parallel", "parallel"), VMEM_LIMIT),
        name="attn_lat",
    )(lam_p, proj, k, v, kc, vc, cos, sin, wsub)


def _rope_tables(n):
    pos = np.arange(n)
    rowcol = np.stack([pos // GRID_W, pos % GRID_W], axis=-1).astype(np.float64)
    nf = A_HEAD_DIM // 4
    inv_freq = np.power(ROPE_BASE, -np.arange(nf, dtype=np.float64) / nf)
    lane = np.arange(LANES)
    axis = (lane % A_HEAD_DIM) // (2 * nf)
    ang = rowcol[:, axis] * inv_freq[lane % nf][None, :]
    sign = np.where((lane % (2 * nf)) < nf, -1.0, 1.0)
    return jnp.asarray(np.cos(ang), F32), jnp.asarray(np.sin(ang) * sign[None, :], F32)


S_PAIRS = S_INNER // LANES
SSD_ROWS = 64
SSD_STEP_CHUNKS = 4


def _ssd_direction(xa_ref, r0, dt_raw_t, dtb_col, alog_col, ht_ref, y_ref, backward):
    col0 = S_HEADS if backward else 0
    chunk = slice(r0, r0 + S_CHUNK)
    row = lax.broadcasted_iota(I32, (S_CHUNK, S_CHUNK), 0)
    col = lax.broadcasted_iota(I32, (S_CHUNK, S_CHUNK), 1)
    dt_t = jax.nn.softplus(dt_raw_t + dtb_col)
    da_t = dt_t * (-jnp.exp(alog_col) * LOG2E)
    if backward:
        last = 0
        keep = row <= col
    else:
        last = S_CHUNK - 1
        keep = row >= col
    tri = jnp.where(keep, 1.0, 0.0).astype(BF16)
    pieces = _split3(da_t)
    cum = sum(_dot_nt(tri, p) for p in pieces[1:]) + _dot_nt(tri, pieces[0])
    cum_t = sum(_dot_nt(p, tri) for p in pieces[1:]) + _dot_nt(pieces[0], tri)
    w_t = dt_t * jnp.exp2(cum_t[:, last:last + 1] - cum_t)
    e_tot = jnp.exp2(cum[last:last + 1, :])
    lo_half = col < S_HEADDIM
    lo_bf = jnp.where(lo_half, 1.0, 0.0).astype(BF16)
    hi_bf = jnp.where(lo_half, 0.0, 1.0).astype(BF16)
    for g in range(S_GROUPS):
        b_bf = xa_ref[chunk, S_INNER + g * S_STATE:S_INNER + (g + 1) * S_STATE]
        c_bf = xa_ref[chunk, S_INNER + (S_GROUPS + g) * S_STATE:S_INNER + (S_GROUPS + g + 1) * S_STATE]
        cb = _dot_nt(c_bf, b_bf)
        bt = b_bf.astype(F32).T
        c_f = c_bf.astype(F32)
        for pr in range(S_PAIRS // S_GROUPS):
            hp = g * (S_PAIRS // S_GROUPS) + pr
            c0 = col0 + 2 * hp
            x_pair = xa_ref[chunk, hp * LANES:(hp + 1) * LANES]
            bd_x = jnp.concatenate([x_pair * lo_bf, x_pair * hi_bf], axis=0)
            ht = ht_ref[hp]
            bd_h = jnp.concatenate([jnp.where(lo_half, ht, 0.0), jnp.where(lo_half, 0.0, ht)], axis=0).astype(BF16)
            rhs_y = jnp.concatenate([bd_x, bd_h], axis=0)
            bws = [(bt * w_t[cc:cc + 1, :]).astype(BF16) for cc in (c0, c0 + 1)]
            for lh in range(S_CHUNK // SSD_ROWS):
                rows = slice(lh * SSD_ROWS, (lh + 1) * SSD_ROWS)
                ms, ces = [], []
                for cc in (c0, c0 + 1):
                    bc = jnp.broadcast_to(cum[rows, cc:cc + 1], (SSD_ROWS, S_CHUNK))
                    decay = jnp.exp2(jnp.where(keep[rows], bc - cum_t[cc:cc + 1, :], -1e30))
                    ms.append((cb[rows] * decay * dt_t[cc:cc + 1, :]).astype(BF16))
                    ces.append((c_f[rows] * jnp.exp2(bc)).astype(BF16))
                y_ref[r0 + lh * SSD_ROWS:r0 + (lh + 1) * SSD_ROWS, hp * LANES:(hp + 1) * LANES] = _dot(jnp.concatenate(ms + ces, axis=1),
                                                                rhs_y).astype(y_ref.dtype)
            tot = jnp.where(lo_half[0:1], e_tot[:, c0:c0 + 1], e_tot[:, c0 + 1:c0 + 2])
            ht_ref[hp] = ht * tot + _dot(jnp.concatenate(bws, axis=1), bd_x)


def _ssd_kernel(*refs, has_h0, emit_final, step_chunks):
    xf_ref, xb_ref, dtft_ref, dtbt_ref, bc_ref, ac_ref = refs[:6]
    refs = refs[6:]
    if has_h0:
        h0_ref, refs = refs[0], refs[1:]
    yf_ref, yb_ref = refs[:2]
    refs = refs[2:]
    if emit_final:
        fin_ref, refs = refs[0], refs[1:]
    htf_ref, htb_ref = refs
    c = pl.program_id(1)
    nc = pl.num_programs(1)

    @pl.when(c == 0)
    def _():
        for d, ht_ref in enumerate((htf_ref, htb_ref)):
            for hp in range(S_PAIRS):
                if has_h0:
                    ht_ref[hp] = h0_ref[0, d, hp * LANES:(hp + 1) * LANES, :].T
                else:
                    ht_ref[hp] = jnp.zeros((S_STATE, LANES), F32)

    for k in range(step_chunks):
        rf = k * S_CHUNK
        rb = (step_chunks - 1 - k) * S_CHUNK
        _ssd_direction(xf_ref, rf, dtft_ref[:, rf:rf + S_CHUNK], bc_ref[...], ac_ref[...], htf_ref, yf_ref,
                       backward=False)
        _ssd_direction(xb_ref, rb, dtbt_ref[:, rb:rb + S_CHUNK], bc_ref[...], ac_ref[...], htb_ref, yb_ref,
                       backward=True)

    if emit_final:
        @pl.when(c == nc - 1)
        def _():
            for d, ht_ref in enumerate((htf_ref, htb_ref)):
                for hp in range(S_PAIRS):
                    fin_ref[0, d, hp * LANES:(hp + 1) * LANES, :] = ht_ref[hp].T


def _ssd(xa, dt_t, dtb_col, alog_col, h0, nbatch, n, emit_final):
    step_chunks = min(SSD_STEP_CHUNKS, n // S_CHUNK)
    rows = step_chunks * S_CHUNK
    nc = n // rows
    t = xa.shape[0]
    fwd = lambda b, c: (b * nc + c, 0)
    bwd = lambda b, c: (b * nc + nc - 1 - c, 0)
    const = lambda b, c: (0, 0)
    in_specs = [pl.BlockSpec((rows, S_CONV_CH), fwd),
                pl.BlockSpec((rows, S_CONV_CH), bwd),
                pl.BlockSpec((LANES, rows), lambda b, c: (0, b * nc + c)),
                pl.BlockSpec((LANES, rows), lambda b, c: (0, b * nc + nc - 1 - c)),
                pl.BlockSpec((LANES, 1), const), pl.BlockSpec((LANES, 1), const)]
    args = [xa, xa, dt_t, dt_t, dtb_col, alog_col]
    if h0 is not None:
        in_specs.append(pl.BlockSpec((1, 2, S_INNER, S_STATE), lambda b, c: (b, 0, 0, 0)))
        args.append(h0)
    out_specs = [pl.BlockSpec((rows, S_INNER), fwd), pl.BlockSpec((rows, S_INNER), bwd)]
    out_shape = [jax.ShapeDtypeStruct((t, S_INNER), BF16), jax.ShapeDtypeStruct((t, S_INNER), BF16)]
    if emit_final:
        out_specs.append(pl.BlockSpec((1, 2, S_INNER, S_STATE), lambda b, c: (b, 0, 0, 0)))
        out_shape.append(jax.ShapeDtypeStruct((nbatch, 2, S_INNER, S_STATE), F32))
    return pl.pallas_call(
        functools.partial(_ssd_kernel, has_h0=h0 is not None, emit_final=emit_final, step_chunks=step_chunks),
        grid=(nbatch, nc),
        in_specs=in_specs,
        out_specs=out_specs,
        out_shape=out_shape,
        scratch_shapes=[pltpu.VMEM((S_PAIRS, S_STATE, LANES), F32),
                        pltpu.VMEM((S_PAIRS, S_STATE, LANES), F32)],
        compiler_params=_cparams(("parallel", "arbitrary"), VMEM_LIMIT),
        name="ssd",
    )(*args)


MIXOUT_SPLIT = 2


def _mixout_kernel(yf_ref, yb_ref, xs_ref, z_ref, oa_ref, ga_ref, gs_ref, x_ref, g1_ref, sh2_ref, sc2_ref,
                   dskip_ref, wn_ref, woa_ref, wos_ref, wout_ref, gpost_ref, gpre_ref, wr_ref,
                   x1_ref, h2_ref, aff_ref):
    wn = wn_ref[...]
    sub = x_ref.shape[0] // MIXOUT_SPLIT
    for r in range(MIXOUT_SPLIT):
        rows = slice(r * sub, (r + 1) * sub)
        y = (yf_ref[rows, :].astype(F32) + yb_ref[rows, :].astype(F32)
             + xs_ref[rows, :].astype(F32) * dskip_ref[...])
        y = y * _silu(z_ref[rows, :].astype(F32))
        parts = []
        for g in range(S_GROUPS):
            sl = slice(g * S_GROUP_W, (g + 1) * S_GROUP_W)
            parts.append(_rms(y[:, sl], wn[:, sl]).astype(BF16))
        o_ssd = jnp.concatenate(parts, axis=1)
        a = _dot(oa_ref[rows, :], woa_ref[...])
        s = _dot(o_ssd, wos_ref[...])
        merged = (jax.nn.sigmoid(ga_ref[rows, :].astype(F32)) * a
                  + jax.nn.sigmoid(gs_ref[rows, :].astype(F32)) * s)
        ym = _dot(merged.astype(BF16), wout_ref[...])
        x1 = x_ref[rows, :] + g1_ref[0] * _rms(ym, gpost_ref[...])
        x1_ref[rows, :] = x1
        h2 = _rms(x1, gpre_ref[...]) * (1.0 + sc2_ref[0]) + sh2_ref[0]
        h2_ref[rows, :] = h2.astype(BF16)
        logits = _dot_hi(h2, wr_ref[...])
        lane = lax.broadcasted_iota(I32, logits.shape, 1)
        logits = jnp.where(lane < N_EXPERTS, logits, -1e30)
        p = jnp.exp(logits - jnp.max(logits, axis=-1, keepdims=True))
        aff_ref[rows, :] = p / jnp.sum(p, axis=-1, keepdims=True)


def _mixout(yf, yb, xa, proj, o_attn, x, mod3, dskip, wn, woa, wos, wout, gpost, gpre, wr, row0, rows_per_cond):
    t = x.shape[0]
    tm = 512
    tok = lambda w: pl.BlockSpec((tm, w), lambda i: (i, 0))
    const = lambda a: pl.BlockSpec(a.shape, lambda i: (0,) * a.ndim, pipeline_mode=pl.Buffered(1))
    return pl.pallas_call(
        _mixout_kernel,
        grid=(t // tm,),
        in_specs=[tok(S_INNER), tok(S_INNER), tok(S_INNER),
                  pl.BlockSpec((tm, S_INNER), lambda i: (i, PROJ_Z // S_INNER)), tok(D_MODEL),
                  pl.BlockSpec((tm, D_MODEL), lambda i: (i, PROJ_GATES // D_MODEL)),
                  pl.BlockSpec((tm, D_MODEL), lambda i: (i, PROJ_GATES // D_MODEL + 1)),
                  tok(D_MODEL),
                  _mod_spec(2, tm, row0, rows_per_cond),
                  _mod_spec(3, tm, row0, rows_per_cond),
                  _mod_spec(4, tm, row0, rows_per_cond),
                  const(dskip), const(wn), const(woa), const(wos), const(wout), const(gpost), const(gpre),
                  const(wr)],
        out_specs=[tok(D_MODEL), tok(D_MODEL), tok(LANES)],
        out_shape=[jax.ShapeDtypeStruct((t, D_MODEL), F32), jax.ShapeDtypeStruct((t, D_MODEL), BF16),
                   jax.ShapeDtypeStruct((t, LANES), F32)],
        compiler_params=_cparams(("parallel",), VMEM_LIMIT),
        name="mixout",
    )(yf, yb, xa, proj, o_attn, proj, proj, x, mod3, mod3, mod3, dskip, wn, woa, wos, wout, gpost, gpre, wr)


COMB_TILE = 256
COMB_WIN = 64
BF16_ROWS = 16


def _router_kernel(aff_ref, pos_ref, cnt_ref, *, cap):
    aff = aff_ref[...]
    e, t = aff.shape
    idx = lax.broadcasted_iota(I32, (e, t), 1)

    def count(m):
        return jnp.sum(jnp.where(m, 1.0, 0.0), axis=-1, keepdims=True)

    def value_bit(i, thr):
        cand = thr | jnp.left_shift(jnp.int32(1), 30 - i)
        return jnp.where(count(aff >= pltpu.bitcast(cand, F32)) >= cap, cand, thr)

    thr = pltpu.bitcast(lax.fori_loop(0, 31, value_bit, jnp.zeros((e, 1), I32)), F32)
    gt = aff > thr
    eq = aff == thr
    need = cap - count(gt)
    nbits = (t - 1).bit_length()

    def index_bit(i, v):
        cand = v | jnp.left_shift(jnp.int32(1), nbits - 1 - i)
        return jnp.where(count(jnp.logical_and(eq, idx < cand)) < need, cand, v)

    last = lax.fori_loop(0, nbits, index_bit, jnp.zeros((e, 1), I32))
    sel = jnp.logical_or(gt, jnp.logical_and(eq, idx <= last))
    sel_bf = jnp.where(sel, 1.0, 0.0).astype(BF16)
    blk = 512
    r = lax.broadcasted_iota(I32, (blk, blk), 0)
    c = lax.broadcasted_iota(I32, (blk, blk), 1)
    upper = jnp.where(r <= c, 1.0, 0.0).astype(BF16)
    carry = jnp.zeros((e, 1), F32)
    tile_cnt = []
    for b in range(t // blk):
        cum = _dot(sel_bf[:, b * blk:(b + 1) * blk], upper) + carry
        carry = cum[:, blk - 1:blk]
        slot = jnp.where(sel[:, b * blk:(b + 1) * blk], cum - 1.0, -1.0)
        pos_ref[:, b * blk:(b + 1) * blk] = slot.astype(I32)
        for j in range(blk // COMB_TILE):
            tile_cnt.append(cum[:, (j + 1) * COMB_TILE - 1:(j + 1) * COMB_TILE])
    lane = lax.broadcasted_iota(I32, (e, LANES), 1)
    cnt = jnp.zeros((e, LANES), F32)
    for i, v in enumerate(tile_cnt):
        cnt = jnp.where(lane == i, v, cnt)
    cnt_ref[...] = cnt.astype(I32)


class _Route(NamedTuple):
    pos: jax.Array
    start: jax.Array
    end: jax.Array
    win: jax.Array
    fits: jax.Array


def _router(aff_t, cap):
    e, t = aff_t.shape
    ntile = t // COMB_TILE
    pos, cnt = pl.pallas_call(
        functools.partial(_router_kernel, cap=cap),
        out_shape=[jax.ShapeDtypeStruct((e, t), I32), jax.ShapeDtypeStruct((e, LANES), I32)],
        name="router",
    )(aff_t)
    end = cnt[:, :ntile]
    start = jnp.concatenate([jnp.zeros((e, 1), I32), end[:, :-1]], axis=1)
    win = jnp.minimum(start // BF16_ROWS * BF16_ROWS, cap - COMB_WIN)
    fits = jnp.all(end - win <= COMB_WIN, axis=0).astype(I32)
    flat = lambda a: a.T.reshape(ntile * e)
    return _Route(pos, flat(start), flat(end), flat(win), fits)


def _dispatch_kernel(start_ref, end_ref, win_ref, fits_ref, pos_ref, h_ref, o_ref):
    i = pl.program_id(0)
    ne, cap, _ = o_ref.shape
    tm = h_ref.shape[0]

    @pl.when(i == 0)
    def _():
        o_ref[...] = jnp.zeros(o_ref.shape, o_ref.dtype)

    h = h_ref[...]
    row = lax.broadcasted_iota(I32, (COMB_WIN, tm), 0)
    row_l = lax.broadcasted_iota(I32, (COMB_WIN, LANES), 0)

    def place(e, r0, rows):
        rel = r0 - start_ref[i * ne + e] + row_l
        mine = jnp.logical_and(rel >= 0, rel < end_ref[i * ne + e] - start_ref[i * ne + e])
        for c in range(D_MODEL // LANES):
            sl = slice(c * LANES, (c + 1) * LANES)
            old = o_ref[e, pl.ds(r0, COMB_WIN), sl].astype(F32)
            o_ref[e, pl.ds(r0, COMB_WIN), sl] = jnp.where(mine, rows[:, sl], old).astype(o_ref.dtype)

    @pl.when(fits_ref[i] == 1)
    def _():
        sel = [jnp.where(pos_ref[e:e + 1, :] == win_ref[i * ne + e] + row, 1.0, 0.0).astype(BF16) for e in range(ne)]
        rows = _dot(jnp.concatenate(sel, axis=0), h)
        for e in range(ne):
            place(e, pl.multiple_of(win_ref[i * ne + e], BF16_ROWS), rows[e * COMB_WIN:(e + 1) * COMB_WIN])

    @pl.when(fits_ref[i] == 0)
    def _():
        for e in range(ne):
            w0 = win_ref[i * ne + e]

            def window(w, carry):
                r0 = pl.multiple_of(jnp.minimum(w0 + w * COMB_WIN, cap - COMB_WIN), BF16_ROWS)
                sel = jnp.where(pos_ref[e:e + 1, :] == r0 + row, 1.0, 0.0).astype(BF16)
                place(e, r0, _dot(sel, h))
                return carry

            lax.fori_loop(0, (end_ref[i * ne + e] - w0 + COMB_WIN - 1) // COMB_WIN, window, 0)


def _dispatch(route, h2, cap):
    e, t = route.pos.shape
    tm = COMB_TILE
    return pl.pallas_call(
        _dispatch_kernel,
        grid_spec=pltpu.PrefetchScalarGridSpec(
            num_scalar_prefetch=4,
            grid=(t // tm,),
            in_specs=[pl.BlockSpec((e, tm), lambda i, *_: (0, i)),
                      pl.BlockSpec((tm, D_MODEL), lambda i, *_: (i, 0))],
            out_specs=pl.BlockSpec((e, cap, D_MODEL), lambda i, *_: (0, 0, 0))),
        out_shape=jax.ShapeDtypeStruct((e, cap, D_MODEL), BF16),
        compiler_params=_cparams(("arbitrary",), VMEM_LIMIT),
        name="dispatch",
    )(route.start, route.end, route.win, route.fits, route.pos, h2)


def _ffn_kernel(xc_ref, xl_ref, wg_ref, wu_ref, wd_ref, oc_ref, ol_ref, acc_ref):
    f = pl.program_id(1)
    cc = xc_ref.shape[1]
    xe = jnp.concatenate([xc_ref[0], xl_ref[0]], axis=0)
    hg = _dot(xe, wg_ref[0].astype(BF16))
    hu = _dot(xe, wu_ref[0].astype(BF16))
    part = _dot((_silu(hg) * hu).astype(BF16), wd_ref[0].astype(BF16))

    @pl.when(f == 0)
    def _():
        acc_ref[...] = part

    @pl.when(f > 0)
    def _():
        acc_ref[...] += part

    @pl.when(f == pl.num_programs(1) - 1)
    def _():
        oc_ref[0] = acc_ref[:cc, :].astype(BF16)
        ol_ref[0] = acc_ref[cc:, :].astype(BF16)


def _ffn(xe_c, xe_l, w_gate, w_up, w_down):
    e, cc, d = xe_c.shape
    cl = xe_l.shape[1]
    tf = 1024
    return pl.pallas_call(
        _ffn_kernel,
        grid=(e, D_FF // tf),
        in_specs=[pl.BlockSpec((1, cc, d), lambda i, f: (i, 0, 0)),
                  pl.BlockSpec((1, cl, d), lambda i, f: (i, 0, 0)),
                  pl.BlockSpec((1, d, tf), lambda i, f: (i, 0, f)),
                  pl.BlockSpec((1, d, tf), lambda i, f: (i, 0, f)),
                  pl.BlockSpec((1, tf, d), lambda i, f: (i, f, 0))],
        out_specs=[pl.BlockSpec((1, cc, d), lambda i, f: (i, 0, 0)),
                   pl.BlockSpec((1, cl, d), lambda i, f: (i, 0, 0))],
        out_shape=[jax.ShapeDtypeStruct((e, cc, d), BF16), jax.ShapeDtypeStruct((e, cl, d), BF16)],
        scratch_shapes=[pltpu.VMEM((cc + cl, d), F32)],
        compiler_params=_cparams(("parallel", "arbitrary"), VMEM_LIMIT),
        name="ffn",
    )(xe_c, xe_l, w_gate, w_up, w_down)


def _combine_kernel(win_ref, fits_ref, pos_ref, aff_ref, ye_ref, x1_ref, g2_ref, gp_ref, o_ref, rhs_ref, y_ref):
    i = pl.program_id(0)
    tm = pos_ref.shape[0]
    cap = ye_ref.shape[1]
    pos = pos_ref[...]
    aff = aff_ref[...]

    @pl.when(fits_ref[i] == 1)
    def _():
        per_col = LANES // COMB_WIN
        lane = lax.broadcasted_iota(I32, (tm, LANES), 1)
        cols = []
        for c in range(N_EXPERTS // per_col):
            col = jnp.zeros((tm, LANES), F32)
            for j in range(per_col):
                e = c * per_col + j
                w0 = pl.multiple_of(win_ref[i * N_EXPERTS + e], BF16_ROWS)
                rhs_ref[e * COMB_WIN:(e + 1) * COMB_WIN, :] = ye_ref[e, pl.ds(w0, COMB_WIN), :]
                p = pos[:, e:e + 1]
                tgt = jnp.where(p >= 0, p - w0 + j * COMB_WIN, -1)
                col = jnp.where(lane == tgt, aff[:, e:e + 1], col)
            cols.append(col)
        sel = jnp.concatenate(cols, axis=1)
        hi = sel.astype(BF16)
        lo = (sel - hi.astype(F32)).astype(BF16)
        both = _dot(jnp.concatenate([hi, lo], axis=0), rhs_ref[...])
        y_ref[...] = both[:tm] + both[tm:]

    @pl.when(fits_ref[i] == 0)
    def _():
        slot = lax.broadcasted_iota(I32, (tm, cap), 1)
        y = jnp.zeros((tm, D_MODEL), F32)
        for e in range(N_EXPERTS):
            onehot = jnp.where(pos[:, e:e + 1] == slot, 1.0, 0.0).astype(BF16)
            y = y + aff[:, e:e + 1] * _dot(onehot, ye_ref[e])
        y_ref[...] = y

    o_ref[...] = x1_ref[...] + g2_ref[0] * _rms(y_ref[...], gp_ref[...])


def _combine(route, aff, ye, x1, mod3, gpost, row0, rows_per_cond):
    t = x1.shape[0]
    pos_t, win, fits = route.pos.T, route.win, route.fits
    e, cap, d = ye.shape
    tm = COMB_TILE
    return pl.pallas_call(
        _combine_kernel,
        grid_spec=pltpu.PrefetchScalarGridSpec(
            num_scalar_prefetch=2,
            grid=(t // tm,),
            in_specs=[pl.BlockSpec((tm, e), lambda i, *_: (i, 0)),
                      pl.BlockSpec((tm, LANES), lambda i, *_: (i, 0)),
                      pl.BlockSpec((e, cap, d), lambda i, *_: (0, 0, 0)),
                      pl.BlockSpec((tm, d), lambda i, *_: (i, 0)),
                      _mod_spec(5, tm, row0, rows_per_cond),
                      pl.BlockSpec((1, d), lambda i, *_: (0, 0))],
            out_specs=pl.BlockSpec((tm, d), lambda i, *_: (i, 0)),
            scratch_shapes=[pltpu.VMEM((e * COMB_WIN, d), BF16), pltpu.VMEM((tm, d), F32)]),
        out_shape=jax.ShapeDtypeStruct((t, d), F32),
        compiler_params=_cparams(("parallel",), VMEM_LIMIT),
        name="combine",
    )(win, fits, pos_t, aff, ye, x1, mod3, gpost.reshape(1, d))


def _head_col(v):
    return jnp.concatenate([v.reshape(2 * S_HEADS), jnp.zeros((LANES - 2 * S_HEADS,), F32)]).reshape(LANES, 1)


def kernel(x_prompt, x_sample, cache_k, cache_v, state_ssd, c, c_ctx, w_ada, b_ada, g_pre_mix, g_post_mix, g_pre_ffn, g_post_ffn, w_in, lam_q1, lam_k1, lam_q2, lam_k2, w_subln, conv_w, conv_b, dt_bias, a_log, d_skip, w_ssd_norm, w_o_attn, w_o_ssd, w_out, w_router, w_gate, w_up, w_down):
    nb_c, n_c, d = x_prompt.shape
    nb_l, n_l, _ = x_sample.shape
    npast = cache_k.shape[2]
    depth = w_in.shape[0]
    assert depth == 1 and d == D_MODEL
    t_c, t_l = nb_c * n_c, nb_l * n_l
    lyr = 0
    lam_init = 0.8 - 0.6 * math.exp(-0.3 * lyr)

    cond8 = jnp.concatenate([c_ctx[None, :], c, jnp.zeros((8 - 1 - nb_l, d), F32)], axis=0)
    mod3 = _ada(cond8, w_ada[lyr], b_ada[lyr]).reshape(8, 1, N_MOD * d)

    w_in_t = w_in[lyr].T
    w_dt_t = jnp.concatenate([w_in_t[IN_DT:IN_GATES], jnp.zeros((LANES - 2 * S_HEADS, d), F32)], axis=0).astype(BF16)
    tn = INPROJ_TN
    rows_of = lambda col0, width: [col0 + j * tn for j in range(width // tn)]
    proj_rows = rows_of(IN_XBC, S_CONV_CH) + rows_of(IN_Q, d) + rows_of(IN_Z, S_INNER) + rows_of(IN_GATES, 2 * d)
    gate_tiles = (PROJ_GATES // tn, (PROJ_GATES + 2 * d) // tn)
    conv_tiles = (PROJ_XBC // tn, (PROJ_XBC + S_CONV_CH) // tn)
    cw, cb = conv_w[lyr], conv_b[lyr].reshape(1, S_CONV_CH)
    lam_p = jnp.stack([lam_q1[lyr], lam_k1[lyr], lam_q2[lyr], lam_k2[lyr]], axis=0)
    wsub = w_subln[lyr].reshape(1, A_V_DIM)
    dtb_col = _head_col(dt_bias[lyr])
    alog_col = _head_col(a_log[lyr])
    dskip = jnp.repeat(d_skip[lyr], S_HEADDIM).reshape(1, S_INNER)
    wn = w_ssd_norm[lyr].reshape(1, S_INNER)
    woa = w_o_attn[lyr].astype(BF16)
    wos = w_o_ssd[lyr].astype(BF16)
    wout = w_out[lyr].astype(BF16)
    wr = jnp.concatenate([w_router[lyr], jnp.zeros((d, LANES - N_EXPERTS), F32)], axis=1)
    gpost = g_post_mix[lyr].reshape(1, d)
    gpre = g_pre_ffn[lyr].reshape(1, d)
    cos, sin = _rope_tables(n_l)

    xc = x_prompt.reshape(t_c, d)
    xl = x_sample.reshape(t_l, d)
    h_c, dtt_c = _prenorm(xc, mod3, g_pre_mix[lyr], w_dt_t, 0, t_c)
    h_l, dtt_l = _prenorm(xl, mod3, g_pre_mix[lyr], w_dt_t, 1, n_l)
    proj_c, proj_l = _inproj(h_c, h_l, w_in_t, proj_rows, BF16, cw, cb, (n_c, n_l), gate_tiles, conv_tiles)
    k_c, k_l = _inproj(h_c, h_l, w_in_t, rows_of(IN_K, d), F32, cw, cb, (n_c, n_l))
    v_c, v_l = _inproj(h_c, h_l, w_in_t, rows_of(IN_V, d), F32, cw, cb, (n_c, n_l))

    def mixer(x, dt_t, proj, k, v, row0, rows_per_cond, nbatch, n, ctx):
        if ctx is None:
            o_attn = _attn_ctx(lam_p, proj, k, v, wsub, nbatch, n, lam_init)
            h0 = None
        else:
            kc, vc, h0 = ctx
            o_attn = _attn_lat(lam_p, proj, k, v, kc, vc, cos, sin, wsub, nbatch, n, npast, lam_init)
        xa = proj
        res = _ssd(xa, dt_t, dtb_col, alog_col, h0, nbatch, n, emit_final=ctx is None)
        yf, yb = res[0], res[1]
        x1, h2, aff = _mixout(yf, yb, xa, proj, o_attn, x, mod3, dskip, wn, woa, wos, wout, gpost, gpre, wr,
                              row0, rows_per_cond)
        fin = res[2] if ctx is None else None
        return x1, h2, aff, fin

    x1_c, h2_c, aff_c, fin_c = mixer(xc, dtt_c, proj_c, k_c, v_c, 0, t_c, nb_c, n_c, None)
    kc = cache_k[:, lyr].reshape(nb_l * npast, A_HEADS * 2 * A_HEAD_DIM)
    vc = cache_v[:, lyr].reshape(nb_l * npast, A_HEADS * A_V_DIM)
    h0 = state_ssd[:, lyr].reshape(nb_l, 2, S_INNER, S_STATE)
    x1_l, h2_l, aff_l, _ = mixer(xl, dtt_l, proj_l, k_l, v_l, 1, n_l, nb_l, n_l, (kc, vc, h0))

    cap_c = EC_FACTOR * t_c // N_EXPERTS
    cap_l = EC_FACTOR * t_l // N_EXPERTS
    route_c = _router(aff_c[:, :N_EXPERTS].T, cap_c)
    route_l = _router(aff_l[:, :N_EXPERTS].T, cap_l)
    xe_c = _dispatch(route_c, h2_c, cap_c)
    xe_l = _dispatch(route_l, h2_l, cap_l)
    ye_c, ye_l = _ffn(xe_c, xe_l, w_gate[lyr], w_up[lyr], w_down[lyr])
    out_c = _combine(route_c, aff_c, ye_c, x1_c, mod3, g_post_ffn[lyr], 0, t_c)
    out_l = _combine(route_l, aff_l, ye_l, x1_l, mod3, g_post_ffn[lyr], 1, n_l)

    y_prompt = out_c.reshape(nb_c, n_c, d)
    y_sample = out_l.reshape(nb_l, n_l, d)
    new_k = k_c.reshape(nb_c, 1, n_c, A_HEADS, 2 * A_HEAD_DIM)
    new_v = v_c.reshape(nb_c, 1, n_c, A_HEADS, A_V_DIM)
    new_state = fin_c.reshape(nb_c, 1, 2, S_HEADS, S_HEADDIM, S_STATE)
    return (y_prompt, y_sample, new_k, new_v, new_state)
```
